```python
import jax
import jax.numpy as jnp
from jax import lax
import numpy as np

D_MODEL = 1024
BATCH = 4
SEQ = 4096
DEPTH = 2

GRID_W = 64
CTX_LEN = 256
NORM_EPS = 1e-6
N_MOD = 6

ATT_HEADS = 8
ATT_KV_HEADS = 2
ATT_GROUP = ATT_HEADS // ATT_KV_HEADS
HEAD_DIM = 128
ATT_WIDTH = ATT_HEADS * HEAD_DIM
KV_WIDTH = ATT_KV_HEADS * HEAD_DIM
Q_BLOCK = 128
ROPE_THETA = 10000.0
ROPE_FREQS_PER_AXIS = HEAD_DIM // 4

HGRN_EXPAND = 128
HGRN_HEADS = D_MODEL // HGRN_EXPAND
HGRN_DV = D_MODEL // HGRN_HEADS
HGRN_WIDTH = HGRN_HEADS * HGRN_EXPAND
HGRN_VWIDTH = HGRN_HEADS * HGRN_DV
HGRN_CHUNK = 64

PROJ_SIZES = (ATT_WIDTH, KV_WIDTH, KV_WIDTH, HGRN_WIDTH, HGRN_WIDTH, HGRN_WIDTH,
              HGRN_VWIDTH, HGRN_VWIDTH, D_MODEL, D_MODEL)
PROJ_WIDTH = sum(PROJ_SIZES)
PROJ_SPLITS = tuple(int(v) for v in np.cumsum(PROJ_SIZES)[:-1])

D_FF_DENSE = 256 * ((8 * D_MODEL // 3 + 255) // 256)
N_EXPERTS = 8
TOP_K = 2
D_FF_EXPERT = 7 * D_MODEL // 2
MOE_BLOCK = 256
N_DENSE_LAYERS = (DEPTH + 1) // 2
N_MOE_LAYERS = DEPTH // 2

kernel_name = 'hybrid_gqa_hgrn2_moe_dit_prefix'


def rms_norm(x, gain):
    xf = x.astype(jnp.float32)
    y = xf * lax.rsqrt(jnp.mean(xf * xf, axis=-1, keepdims=True) + NORM_EPS)
    return (y * gain.astype(jnp.float32)).astype(x.dtype)


def modulate(h, shift, scale):
    return h * (1.0 + scale) + shift


def split_heads(t, n_heads):
    return t.reshape(*t.shape[:-1], n_heads, t.shape[-1] // n_heads)


def axial_rope_tables(n_tokens):
    rows = n_tokens // GRID_W
    row_ids = jnp.repeat(jnp.arange(rows), GRID_W).astype(jnp.float32)
    col_ids = jnp.tile(jnp.arange(GRID_W), rows).astype(jnp.float32)
    inv_freq = ROPE_THETA ** (-jnp.arange(ROPE_FREQS_PER_AXIS, dtype=jnp.float32) / ROPE_FREQS_PER_AXIS)
    ang = jnp.concatenate([row_ids[:, None] * inv_freq, col_ids[:, None] * inv_freq], axis=-1)
    ang = jnp.concatenate([ang, ang], axis=-1)
    return jnp.cos(ang), jnp.sin(ang)


def apply_rope(x, cos, sin):
    half = x.shape[-1] // 2
    xf = x.astype(jnp.float32)
    rot = jnp.concatenate([-xf[..., half:], xf[..., :half]], axis=-1)
    return (xf * cos[None, :, None, :] + rot * sin[None, :, None, :]).astype(x.dtype)


def gqa_softmax(q, k, v):
    s = jnp.einsum('bqkgd,bskd->bkgqs', q, k).astype(jnp.float32) * (HEAD_DIM ** -0.5)
    p = jax.nn.softmax(s, axis=-1).astype(v.dtype)
    return jnp.einsum('bkgqs,bskd->bqkgd', p, v)


def latent_attention(q, k_lat, v_lat, k_ctx, v_ctx):
    b, n = q.shape[:2]
    k_all = jnp.concatenate([k_ctx, k_lat], axis=1)
    v_all = jnp.concatenate([v_ctx, v_lat], axis=1)
    qb = q.reshape(b, n // Q_BLOCK, Q_BLOCK, ATT_KV_HEADS, ATT_GROUP, HEAD_DIM).transpose(1, 0, 2, 3, 4, 5)
    ob = lax.map(lambda blk: gqa_softmax(blk, k_all, v_all), qb)
    return ob.transpose(1, 0, 2, 3, 4, 5).reshape(b, n, ATT_WIDTH)


def context_attention(q, k_ctx, v_ctx):
    b, l = q.shape[:2]
    qg = q.reshape(b, l, ATT_KV_HEADS, ATT_GROUP, HEAD_DIM)
    return gqa_softmax(qg, k_ctx, v_ctx).reshape(b, l, ATT_WIDTH)


def hgrn_lower_bounds(logits):
    lb = jnp.cumsum(jax.nn.softmax(logits.astype(jnp.float32), axis=0), axis=0)
    return lb - lb[0:1]


def hgrn_forget(f_pre, lb):
    f = lb + (1.0 - lb) * jax.nn.sigmoid(f_pre.astype(jnp.float32))
    return split_heads(1.0 - f, HGRN_HEADS), split_heads(jnp.log(f), HGRN_HEADS)


def hgrn2_chunk_scan(q, k, v, log_f, state0):
    b, n, h, _ = q.shape
    dv = v.shape[-1]
    nc = n // HGRN_CHUNK

    def to_chunks(t):
        return t.astype(jnp.float32).reshape(b, nc, HGRN_CHUNK, h, t.shape[-1]).transpose(1, 0, 3, 2, 4)

    causal = jnp.tril(jnp.ones((HGRN_CHUNK, HGRN_CHUNK), dtype=bool))[:, :, None]

    def step(state, chunk):
        qc, kc, vc, gc = chunk
        cum = jnp.cumsum(gc, axis=2)
        rel = jnp.where(causal, cum[:, :, :, None, :] - cum[:, :, None, :, :], -jnp.inf)
        scores = jnp.einsum('bhtd,bhsd,bhtsd->bhts', qc, kc, jnp.exp(rel))
        out = (jnp.einsum('bhts,bhsv->bhtv', scores, vc)
               + jnp.einsum('bhtd,bhdv->bhtv', qc * jnp.exp(cum), state))
        tail = cum[:, :, -1:, :]
        new_state = (jnp.exp(tail[:, :, 0, :, None]) * state
                     + jnp.einsum('bhsd,bhsv->bhdv', kc * jnp.exp(tail - cum), vc))
        return new_state, out

    final, outs = lax.scan(step, state0, tuple(to_chunks(t) for t in (q, k, v, log_f)))
    o = outs.transpose(1, 0, 3, 2, 4).reshape(b, n, h, dv)
    return o.astype(v.dtype), final


def hgrn_readout(o, gate, gain):
    o = rms_norm(o, gain)
    return o.reshape(*o.shape[:2], HGRN_VWIDTH) * jax.nn.silu(gate)


def merge_branches(att_o, hg_o, gate_att, gate_hg, w_att_br, w_hg_br, w_out_l):
    y = (jax.nn.sigmoid(gate_att) * (att_o @ w_att_br)
         + jax.nn.sigmoid(gate_hg) * (hg_o @ w_hg_br))
    return y @ w_out_l


def swiglu(t, w_gate, w_up, w_down):
    return (jax.nn.silu(t @ w_gate) * (t @ w_up)) @ w_down


def moe_swiglu(t, w_router, w_gate, w_up, w_down):
    n_tok, d = t.shape
    logits = (t @ w_router).astype(jnp.float32)
    top_logits, top_idx = lax.top_k(logits, TOP_K)
    top_w = jax.nn.softmax(top_logits, axis=-1).astype(t.dtype)
    flat_e = top_idx.reshape(-1)
    n_assign = n_tok * TOP_K
    order = jnp.argsort(flat_e)
    sorted_e = flat_e[order]
    counts = jnp.bincount(flat_e, length=N_EXPERTS)
    padded = (counts + MOE_BLOCK - 1) // MOE_BLOCK * MOE_BLOCK
    start = jnp.cumsum(counts) - counts
    pend = jnp.cumsum(padded)
    pstart = pend - padded
    dest_sorted = (pstart[sorted_e] + jnp.arange(n_assign) - start[sorted_e]).astype(jnp.int32)
    n_blocks = -(-n_assign // MOE_BLOCK) + N_EXPERTS
    n_rows = n_blocks * MOE_BLOCK
    row_token = jnp.full((n_rows,), n_tok, jnp.int32).at[dest_sorted].set((order // TOP_K).astype(jnp.int32))
    block_expert = jnp.minimum(
        jnp.searchsorted(pend, jnp.arange(n_blocks) * MOE_BLOCK, side='right'), N_EXPERTS - 1)
    t_pad = jnp.concatenate([t, jnp.zeros((1, d), t.dtype)], axis=0)
    xb = t_pad[row_token].reshape(n_blocks, MOE_BLOCK, d)

    def expert_block(args):
        xblk, e = args
        return swiglu(xblk, w_gate[e], w_up[e], w_down[e])

    yb = lax.map(expert_block, (xb, block_expert)).reshape(n_rows, d)
    dest = jnp.zeros((n_assign,), jnp.int32).at[order].set(dest_sorted)
    y_sel = yb[dest].reshape(n_tok, TOP_K, d)
    return jnp.einsum('tk,tkd->td', top_w, y_sel)


def setup_inputs(seed: int = 0) -> dict:
    key = jax.random.key(seed)
    ks = iter(jax.random.split(key, 32))
    D = D_MODEL

    def nrm(shape, scale):
        return scale * jax.random.normal(next(ks), shape, jnp.float32)

    def gain(shape):
        return 1.0 + nrm(shape, 0.02)

    return {
        'x': nrm((BATCH, SEQ, D), 1.0),
        'c': nrm((BATCH, D), 1.0),
        'ctx': nrm((BATCH, CTX_LEN, D), 1.0),
        'c_ctx': nrm((D,), 1.0),
        'w_mod': nrm((DEPTH, D, N_MOD * D), 0.5 * D ** -0.5),
        'b_mod': nrm((DEPTH, N_MOD * D), 0.02),
        'pre_mix_norm': gain((DEPTH, D)),
        'post_mix_norm': gain((DEPTH, D)),
        'pre_ffn_norm': gain((DEPTH, D)),
        'post_ffn_norm': gain((DEPTH, D)),
        'w_in': nrm((DEPTH, D, PROJ_WIDTH), D ** -0.5),
        'q_norm': gain((DEPTH, HEAD_DIM)),
        'k_norm': gain((DEPTH, HEAD_DIM)),
        'hg_norm': gain((DEPTH, HGRN_DV)),
        'hg_lb_logits': nrm((DEPTH, HGRN_WIDTH), 1.0),
        'w_att_branch': nrm((DEPTH, ATT_WIDTH, D), ATT_WIDTH ** -0.5),
        'w_hg_branch': nrm((DEPTH, HGRN_VWIDTH, D), HGRN_VWIDTH ** -0.5),
        'w_out': nrm((DEPTH, D, D), D ** -0.5),
        'ffn_w_gate': nrm((N_DENSE_LAYERS, D, D_FF_DENSE), D ** -0.5),
        'ffn_w_up': nrm((N_DENSE_LAYERS, D, D_FF_DENSE), D ** -0.5),
        'ffn_w_down': nrm((N_DENSE_LAYERS, D_FF_DENSE, D), D_FF_DENSE ** -0.5),
        'moe_router': nrm((N_MOE_LAYERS, D, N_EXPERTS), D ** -0.5),
        'moe_w_gate': nrm((N_MOE_LAYERS, N_EXPERTS, D, D_FF_EXPERT), D ** -0.5),
        'moe_w_up': nrm((N_MOE_LAYERS, N_EXPERTS, D, D_FF_EXPERT), D ** -0.5),
        'moe_w_down': nrm((N_MOE_LAYERS, N_EXPERTS, D_FF_EXPERT, D), D_FF_EXPERT ** -0.5),
    }


def reference(x, c, ctx, c_ctx, w_mod, b_mod, pre_mix_norm, post_mix_norm, pre_ffn_norm,
              post_ffn_norm, w_in, q_norm, k_norm, hg_norm, hg_lb_logits, w_att_branch,
              w_hg_branch, w_out, ffn_w_gate, ffn_w_up, ffn_w_down, moe_router, moe_w_gate,
              moe_w_up, moe_w_down):
    b, n, d = x.shape
    l_ctx = ctx.shape[1]
    cos, sin = axial_rope_tables(n)
    lower_bounds = hgrn_lower_bounds(hg_lb_logits)
    zero_state = jnp.zeros((b, HGRN_HEADS, HGRN_EXPAND, HGRN_DV), jnp.float32)
    x_lat, x_ctx = x, ctx

    for layer in range(DEPTH):
        last = layer == DEPTH - 1
        mod_lat = (jax.nn.silu(c) @ w_mod[layer] + b_mod[layer])[:, None, :]
        mod_ctx = (jax.nn.silu(c_ctx) @ w_mod[layer] + b_mod[layer])[None, None, :]
        sh1, sc1, g1, sh2, sc2, g2 = jnp.split(mod_lat, N_MOD, axis=-1)
        csh1, csc1, cg1, csh2, csc2, cg2 = jnp.split(mod_ctx, N_MOD, axis=-1)
        lb = lower_bounds[layer]

        h_ctx = modulate(rms_norm(x_ctx, pre_mix_norm[layer]), csh1, csc1)
        h_lat = modulate(rms_norm(x_lat, pre_mix_norm[layer]), sh1, sc1)
        (aq_c, ak_c, av_c, hq_c, hff_c, hfb_c, hi_c, hg_c, ga_c, gh_c) = jnp.split(
            h_ctx @ w_in[layer], PROJ_SPLITS, axis=-1)
        (aq_l, ak_l, av_l, hq_l, hff_l, hfb_l, hi_l, hg_l, ga_l, gh_l) = jnp.split(
            h_lat @ w_in[layer], PROJ_SPLITS, axis=-1)

        k_ctx = rms_norm(split_heads(ak_c, ATT_KV_HEADS), k_norm[layer])
        v_ctx = split_heads(av_c, ATT_KV_HEADS)
        hq_ctx = split_heads(jax.nn.silu(hq_c), HGRN_HEADS)
        hv_ctx = split_heads(hi_c, HGRN_HEADS)
        kf_c, lf_c = hgrn_forget(hff_c, lb)
        kb_c, lb_c = hgrn_forget(hfb_c, lb)
        o_cf, state_f = hgrn2_chunk_scan(hq_ctx, kf_c, hv_ctx, lf_c, zero_state)
        o_cb, state_b = hgrn2_chunk_scan(jnp.flip(hq_ctx, 1), jnp.flip(kb_c, 1), jnp.flip(hv_ctx, 1),
                                         jnp.flip(lb_c, 1), zero_state)

        q_lat = apply_rope(rms_norm(split_heads(aq_l, ATT_HEADS), q_norm[layer]), cos, sin)
        k_lat = apply_rope(rms_norm(split_heads(ak_l, ATT_KV_HEADS), k_norm[layer]), cos, sin)
        v_lat = split_heads(av_l, ATT_KV_HEADS)
        att_lat = latent_attention(q_lat, k_lat, v_lat, k_ctx, v_ctx)
        hq_lat = split_heads(jax.nn.silu(hq_l), HGRN_HEADS)
        hv_lat = split_heads(hi_l, HGRN_HEADS)
        kf_l, lf_l = hgrn_forget(hff_l, lb)
        kb_l, lb_l = hgrn_forget(hfb_l, lb)
        o_lf, _ = hgrn2_chunk_scan(hq_lat, kf_l, hv_lat, lf_l, state_f)
        o_lb, _ = hgrn2_chunk_scan(jnp.flip(hq_lat, 1), jnp.flip(kb_l, 1), jnp.flip(hv_lat, 1),
                                   jnp.flip(lb_l, 1), state_b)
        hg_lat = hgrn_readout(o_lf + jnp.flip(o_lb, 1), hg_l, hg_norm[layer])
        mix_lat = merge_branches(att_lat, hg_lat, ga_l, gh_l, w_att_branch[layer],
                                 w_hg_branch[layer], w_out[layer])
        x_lat = x_lat + g1 * rms_norm(mix_lat, post_mix_norm[layer])

        if not last:
            q_ctx = rms_norm(split_heads(aq_c, ATT_HEADS), q_norm[layer])
            att_ctx = context_attention(q_ctx, k_ctx, v_ctx)
            hg_ctx = hgrn_readout(o_cf + jnp.flip(o_cb, 1), hg_c, hg_norm[layer])
            mix_ctx = merge_branches(att_ctx, hg_ctx, ga_c, gh_c, w_att_branch[layer],
                                     w_hg_branch[layer], w_out[layer])
            x_ctx = x_ctx + cg1 * rms_norm(mix_ctx, post_mix_norm[layer])

        f_lat = modulate(rms_norm(x_lat, pre_ffn_norm[layer]), sh2, sc2).reshape(b * n, d)
        if not last:
            f_ctx = modulate(rms_norm(x_ctx, pre_ffn_norm[layer]), csh2, csc2).reshape(b * l_ctx, d)
            tokens = jnp.concatenate([f_lat, f_ctx], axis=0)
        else:
            tokens = f_lat
        idx = layer // 2
        if layer % 2 == 0:
            y = swiglu(tokens, ffn_w_gate[idx], ffn_w_up[idx], ffn_w_down[idx])
        else:
            y = moe_swiglu(tokens, moe_router[idx], moe_w_gate[idx], moe_w_up[idx], moe_w_down[idx])
        y = rms_norm(y, post_ffn_norm[layer])
        x_lat = x_lat + g2 * y[:b * n].reshape(b, n, d)
        if not last:
            x_ctx = x_ctx + cg2 * y[b * n:].reshape(b, l_ctx, d)

    return x_lat
```

```python
import functools

import jax
import jax.numpy as jnp
from jax import lax
from jax.experimental import pallas as pl
from jax.experimental.pallas import tpu as pltpu

F32 = jnp.float32
BF16 = jnp.bfloat16

D_MODEL = 1024
NORM_EPS = 1e-6
N_MOD = 6
GRID_W = 64
ROPE_THETA = 10000.0

HEAD_DIM = 128
ATT_HEADS = 8
ATT_KV_HEADS = 2
ATT_GROUP = ATT_HEADS // ATT_KV_HEADS
KV_WIDTH = ATT_KV_HEADS * HEAD_DIM
HGRN_HEADS = 8

N_EXPERTS = 8
TOP_K = 2
MOE_BLOCK = 256

SUBLANES = 8
LANES = 128
VMEM_LIMIT_BYTES = 56 * 1024 * 1024

TOKEN_TILE = 256
HGRN_CHUNK = 64
HGRN_STEP = 256
ATT_Q_TILE = 128
ATT_K_TILE = 512
ROW_DMA_WINDOW = 32
ROW_DMA_STEP = 1024

_C_AQ = 0
_C_AK = _C_AQ + ATT_HEADS * HEAD_DIM
_C_AV = _C_AK + KV_WIDTH
_C_HQ = _C_AV + KV_WIDTH
_C_HFF = _C_HQ + D_MODEL
_C_HFB = _C_HFF + D_MODEL
_C_HI = _C_HFB + D_MODEL
_C_HG = _C_HI + D_MODEL
_C_GA = _C_HG + D_MODEL
_C_GH = _C_GA + D_MODEL
PROJ_WIDTH = _C_GH + D_MODEL


def _params(*sem):
    return pltpu.CompilerParams(dimension_semantics=sem, vmem_limit_bytes=VMEM_LIMIT_BYTES)


def _resident(shape, index_map):
    return pl.BlockSpec(shape, index_map, pipeline_mode=pl.Buffered(1))


def _rms(t, gain):
    return t * lax.rsqrt(jnp.mean(t * t, axis=-1, keepdims=True) + NORM_EPS) * gain


def _silu(t):
    return t * jax.nn.sigmoid(t)


def _dot(a, b):
    return jnp.dot(a, b, preferred_element_type=F32)


def _dot_nt(a, b):
    return lax.dot_general(a, b, (((1,), (1,)), ((), ())), preferred_element_type=F32)


def _dot_tn(a, b):
    return lax.dot_general(a, b, (((0,), (0,)), ((), ())), preferred_element_type=F32)


def _mod_kernel(c_ref, w_ref, b_ref, o_ref):
    a = _silu(c_ref[...])
    o_ref[...] = jnp.dot(a, w_ref[...], preferred_element_type=F32,
                         precision=lax.Precision.HIGHEST) + b_ref[...]


def _mod_vectors(c_rows, w_mod, b_mod):
    depth, d, width = w_mod.shape
    rows = c_rows.shape[0]
    tn = 1536
    return pl.pallas_call(
        _mod_kernel,
        grid=(depth, width // tn),
        in_specs=[
            pl.BlockSpec((rows, d), lambda l, j: (0, 0)),
            pl.BlockSpec((None, d, tn), lambda l, j: (l, 0, j)),
            pl.BlockSpec((None, 1, tn), lambda l, j: (l, 0, j)),
        ],
        out_specs=pl.BlockSpec((None, rows, tn), lambda l, j: (l, 0, j)),
        out_shape=jax.ShapeDtypeStruct((depth, rows, width), F32),
        compiler_params=_params("arbitrary", "arbitrary"),
        name="mod_vectors",
    )(c_rows, w_mod, b_mod.reshape(depth, 1, width))


def _inproj_kernel(x_ref, sh_ref, sc_ref, gain_ref, w_ref, cos_ref, sin_ref, qn_ref, kn_ref, lbl_ref,
                   aq_ref, ak_ref, av_ref, hq_ref, kf_ref, lf_ref, kb_ref, lb_ref, hi_ref, hg_ref,
                   ga_ref, gh_ref, *, layer):
    h = _rms(x_ref[...], gain_ref[...])
    hb = (h * (1.0 + sc_ref[...]) + sh_ref[...]).astype(BF16)
    cos = cos_ref[...]
    sin = sin_ref[...]

    def proj(c0, width):
        return _dot(hb, w_ref[:, c0:c0 + width])

    def head(t, i):
        return t[:, i * HEAD_DIM:(i + 1) * HEAD_DIM]

    def norm_rope(t, gain):
        r = _rms(t, gain)
        return r * cos + pltpu.roll(r, HEAD_DIM // 2, 1) * sin

    qn = qn_ref[...]
    for c in range(2):
        t = proj(_C_AQ + c * 512, 512)
        for i in range(4):
            hh = c * 4 + i
            aq_ref[:, hh * HEAD_DIM:(hh + 1) * HEAD_DIM] = (
                norm_rope(head(t, i), qn) * (HEAD_DIM ** -0.5)).astype(BF16)
    t = proj(_C_AK, 2 * KV_WIDTH)
    kn = kn_ref[...]
    for i in range(ATT_KV_HEADS):
        ak_ref[:, i * HEAD_DIM:(i + 1) * HEAD_DIM] = norm_rope(head(t, i), kn).astype(BF16)
    av_ref[...] = t[:, KV_WIDTH:].astype(BF16)

    for c in range(2):
        t = proj(_C_HQ + c * 512, 512)
        for i in range(4):
            hq_ref[c * 4 + i] = _silu(head(t, i)).astype(BF16)

    lg = lbl_ref[...]
    depth = lg.shape[0]
    mx = lg[0:1]
    for j in range(1, depth):
        mx = jnp.maximum(mx, lg[j:j + 1])
    es = [jnp.exp(lg[j:j + 1] - mx) for j in range(depth)]
    tot = es[0]
    for j in range(1, depth):
        tot = tot + es[j]
    low = jnp.zeros_like(tot)
    for j in range(1, layer + 1):
        low = low + es[j] / tot

    for c0, k_ref, l_ref in ((_C_HFF, kf_ref, lf_ref), (_C_HFB, kb_ref, lb_ref)):
        for c in range(2):
            t = proj(c0 + c * 512, 512)
            for i in range(4):
                hh = c * 4 + i
                lo = low[:, hh * HEAD_DIM:(hh + 1) * HEAD_DIM]
                f = lo + (1.0 - lo) * jax.nn.sigmoid(head(t, i))
                k_ref[hh] = (1.0 - f).astype(BF16)
                l_ref[hh] = jnp.log(f)

    for c in range(2):
        t = proj(_C_HI + c * 512, 512)
        for i in range(4):
            hi_ref[c * 4 + i] = head(t, i).astype(BF16)
    for c in range(2):
        t = proj(_C_HG + c * 512, 512)
        for i in range(4):
            hg_ref[c * 4 + i] = _silu(head(t, i)).astype(BF16)

    for c in range(2):
        ga_ref[:, c * 512:(c + 1) * 512] = jax.nn.sigmoid(proj(_C_GA + c * 512, 512)).astype(BF16)
    for c in range(2):
        gh_ref[:, c * 512:(c + 1) * 512] = jax.nn.sigmoid(proj(_C_GH + c * 512, 512)).astype(BF16)


def _in_projection(x, mod, gain, w_in, cos_tab, sin_tab, q_norm, k_norm, lb_logits, *, layer, n_lat,
                   seq):
    t_rows, d = x.shape
    tm = TOKEN_TILE
    n_lat_tiles = n_lat // tm
    per_seq = seq // tm

    def mod_row(i):
        return jnp.where(i < n_lat_tiles, i // per_seq, mod.shape[0] - 1 - 3)

    def pos_blk(i):
        return jnp.where(i < n_lat_tiles, i % per_seq, per_seq)

    tok = lambda w: pl.BlockSpec((tm, w), lambda i: (i, 0))
    hm = pl.BlockSpec((HGRN_HEADS, tm, HEAD_DIM), lambda i: (0, i, 0))
    hm_shape = lambda dt: jax.ShapeDtypeStruct((HGRN_HEADS, t_rows, HEAD_DIM), dt)
    tm_shape = lambda w: jax.ShapeDtypeStruct((t_rows, w), BF16)
    return pl.pallas_call(
        functools.partial(_inproj_kernel, layer=layer),
        grid=(t_rows // tm,),
        in_specs=[
            tok(d),
            pl.BlockSpec((None, 1, d), lambda i: (mod_row(i), 0, 0)),
            pl.BlockSpec((None, 1, d), lambda i: (mod_row(i), 0, 1)),
            pl.BlockSpec((1, d), lambda i: (0, 0)),
            _resident((d, PROJ_WIDTH), lambda i: (0, 0)),
            pl.BlockSpec((tm, HEAD_DIM), lambda i: (pos_blk(i), 0)),
            pl.BlockSpec((tm, HEAD_DIM), lambda i: (pos_blk(i), 0)),
            pl.BlockSpec((1, HEAD_DIM), lambda i: (0, 0)),
            pl.BlockSpec((1, HEAD_DIM), lambda i: (0, 0)),
            pl.BlockSpec(lb_logits.shape, lambda i: (0, 0)),
        ],
        out_specs=[tok(d), tok(KV_WIDTH), tok(KV_WIDTH), hm, hm, hm, hm, hm, hm, hm, tok(d), tok(d)],
        out_shape=[tm_shape(d), tm_shape(KV_WIDTH), tm_shape(KV_WIDTH), hm_shape(BF16), hm_shape(BF16),
                   hm_shape(F32), hm_shape(BF16), hm_shape(F32), hm_shape(BF16), hm_shape(BF16),
                   tm_shape(d), tm_shape(d)],
        compiler_params=_params("arbitrary"),
        name=f"in_projection_l{layer}",
    )(x, mod, mod, gain, w_in, cos_tab, sin_tab, q_norm, k_norm, lb_logits)


def _hgrn_constants(c):
    t = lax.broadcasted_iota(jnp.int32, (c, c), 0)
    s = lax.broadcasted_iota(jnp.int32, (c, c), 1)
    tri = ((t >= s).astype(BF16), (t <= s).astype(BF16))
    same8 = (t // SUBLANES) == (s // SUBLANES)
    diag = (same8 & (t >= s), same8 & (t <= s))
    levels = []
    blk = 2 * SUBLANES
    while blk <= c:
        half = blk // 2
        same = (t // blk) == (s // blk)
        t_hi = (t % blk) >= half
        s_hi = (s % blk) >= half
        levels.append((blk, (same & t_hi & ~s_hi, same & ~t_hi & s_hi)))
        blk *= 2
    r = lax.broadcasted_iota(jnp.int32, (SUBLANES * LANES, c), 0)
    j = lax.broadcasted_iota(jnp.int32, (SUBLANES * LANES, c), 1)
    sel = ((j % SUBLANES) == (r // LANES)).astype(BF16)
    return tri, diag, levels, sel


def _hgrn_chunk(q, k, v, g, st_t, consts, rev):
    tri, diag, levels, sel = consts
    c = q.shape[0]
    qf = q.astype(F32)
    kf = k.astype(F32)

    g1 = g.astype(BF16)
    r1 = g - g1.astype(F32)
    g2 = r1.astype(BF16)
    g3 = (r1 - g2.astype(F32)).astype(BF16)
    cum = _dot(tri[rev], g1) + _dot(tri[rev], g2) + _dot(tri[rev], g3)
    tail = cum[0:1] if rev else cum[c - 1:c]

    out = _dot_nt((qf * jnp.exp(cum)).astype(BF16), st_t.astype(BF16))
    k_end = (kf * jnp.exp(tail - cum)).astype(BF16)
    st_new = st_t * jnp.exp(tail) + _dot_tn(v, k_end)

    a = jnp.zeros((c, c), F32)
    for blk, masks in levels:
        half = blk // 2
        rows = []
        for a0 in range(0, c, blk):
            r = a0 + half if rev else a0 + half - 1
            rows.append(jnp.broadcast_to(cum[r:r + 1, :], (blk, LANES)))
        mid = rows[0] if len(rows) == 1 else jnp.concatenate(rows, axis=0)
        q_l = (qf * jnp.exp(jnp.minimum(cum - mid, 0.0))).astype(BF16)
        k_l = (kf * jnp.exp(jnp.minimum(mid - cum, 0.0))).astype(BF16)
        a = a + jnp.where(masks[rev], _dot_nt(q_l, k_l), 0.0)

    nb = c // SUBLANES
    q3 = qf.reshape(nb, SUBLANES, LANES)
    k3 = kf.reshape(nb, SUBLANES, LANES)
    c3 = cum.reshape(nb, SUBLANES, LANES)
    parts = []
    for s in range(SUBLANES):
        e = jnp.exp(jnp.minimum(c3 - c3[:, s:s + 1, :], 0.0))
        parts.append((q3 * e * k3[:, s:s + 1, :]).reshape(c, LANES).astype(BF16))
    a = a + jnp.where(diag[rev], _dot(jnp.concatenate(parts, axis=1), sel), 0.0)

    out = out + _dot(a.astype(BF16), v)
    return out, st_new


def _hgrn_kernel(qf_ref, vf_ref, kf_ref, lf_ref, qb_ref, vb_ref, kb_ref, lb_ref, of_ref, ob_ref,
                 st_ref, *, chunk):
    @pl.when(pl.program_id(1) == 0)
    def _():
        st_ref[...] = jnp.zeros(st_ref.shape, F32)

    consts = _hgrn_constants(chunk)
    n_chunks = qf_ref.shape[1] // chunk

    def head_body(h, carry):
        st_f = st_ref[0, h]
        st_b = st_ref[1, h]
        for j in range(n_chunks):
            rows = pl.ds(j * chunk, chunk)
            o, st_f = _hgrn_chunk(qf_ref[h, rows, :], kf_ref[h, rows, :], vf_ref[h, rows, :],
                                  lf_ref[h, rows, :], st_f, consts, 0)
            of_ref[h, rows, :] = o
            rows = pl.ds((n_chunks - 1 - j) * chunk, chunk)
            o, st_b = _hgrn_chunk(qb_ref[h, rows, :], kb_ref[h, rows, :], vb_ref[h, rows, :],
                                  lb_ref[h, rows, :], st_b, consts, 1)
            ob_ref[h, rows, :] = o
        st_ref[0, h] = st_f
        st_ref[1, h] = st_b
        return carry

    lax.fori_loop(0, qf_ref.shape[0], head_body, 0)


def _hgrn_scan(hq, hi, kf, lf, kb, lb, *, batch, seq, ctx_len):
    heads, t_rows, dh = hq.shape
    ts = HGRN_STEP
    n_ctx = ctx_len // ts
    n_seq = seq // ts
    ctx0 = batch * n_seq

    def fwd(b, s):
        return jnp.where(s < n_ctx, ctx0 + b * n_ctx + s, b * n_seq + s - n_ctx)

    def bwd(b, s):
        return jnp.where(s < n_ctx, ctx0 + b * n_ctx + n_ctx - 1 - s, b * n_seq + n_seq - 1 - (s - n_ctx))

    spec_f = pl.BlockSpec((heads, ts, dh), lambda b, s: (0, fwd(b, s), 0))
    spec_b = pl.BlockSpec((heads, ts, dh), lambda b, s: (0, bwd(b, s), 0))
    out = jax.ShapeDtypeStruct((heads, t_rows, dh), F32)
    return pl.pallas_call(
        functools.partial(_hgrn_kernel, chunk=HGRN_CHUNK),
        grid=(batch, n_ctx + n_seq),
        in_specs=[spec_f, spec_f, spec_f, spec_f, spec_b, spec_b, spec_b, spec_b],
        out_specs=[spec_f, spec_b],
        out_shape=[out, out],
        scratch_shapes=[pltpu.VMEM((2, heads, dh, dh), F32)],
        compiler_params=_params("arbitrary", "arbitrary"),
        name="hgrn_scan",
    )(hq, hi, kf, lf, hq, hi, kb, lb)


def _attn_kernel(q_ref, kc_ref, vc_ref, kl_ref, vl_ref, o_ref, *, n_lat_q_tiles, k_tile):
    tq = q_ref.shape[0]
    q = jnp.concatenate([q_ref[:, i * HEAD_DIM:(i + 1) * HEAD_DIM] for i in range(ATT_GROUP)], axis=0)
    rows = q.shape[0]

    def update(carry, k, v):
        m, l, acc = carry
        s = _dot_nt(q, k)
        m_new = jnp.maximum(m, jnp.max(s, axis=-1, keepdims=True))
        alpha = jnp.exp(m - m_new)
        p = jnp.exp(s - m_new)
        l = alpha * l + jnp.sum(p, axis=-1, keepdims=True)
        acc = alpha * acc + _dot(p.astype(BF16), v)
        return m_new, l, acc

    carry = (jnp.full((rows, 1), -1e30, F32), jnp.zeros((rows, 1), F32), jnp.zeros((rows, HEAD_DIM), F32))
    carry = update(carry, kc_ref[...], vc_ref[...])
    n_chunks = jnp.where(pl.program_id(2) < n_lat_q_tiles, kl_ref.shape[0] // k_tile, 0)

    def body(j, c):
        r = pl.ds(pl.multiple_of(j * k_tile, k_tile), k_tile)
        return update(c, kl_ref[r, :], vl_ref[r, :])

    _, l, acc = lax.fori_loop(0, n_chunks, body, carry)
    out = acc / l
    for i in range(ATT_GROUP):
        o_ref[:, i * HEAD_DIM:(i + 1) * HEAD_DIM] = out[i * tq:(i + 1) * tq].astype(BF16)


def _attention(aq, ak, av, *, batch, seq, ctx_len, with_ctx_queries):
    t_rows = aq.shape[0]
    tq = ATT_Q_TILE
    nq_lat = seq // tq
    nq_ctx = ctx_len // tq
    ctx_q0 = batch * nq_lat
    ctx_k0 = batch * seq // ctx_len
    gw = ATT_GROUP * HEAD_DIM

    def q_blk(b, i):
        return jnp.where(i < nq_lat, b * nq_lat + i, ctx_q0 + b * nq_ctx + i - nq_lat)

    q_spec = pl.BlockSpec((tq, gw), lambda b, g, i: (q_blk(b, i), g))
    kc_spec = pl.BlockSpec((ctx_len, HEAD_DIM), lambda b, g, i: (ctx_k0 + b, g))
    kl_spec = pl.BlockSpec((seq, HEAD_DIM), lambda b, g, i: (b, g))
    return pl.pallas_call(
        functools.partial(_attn_kernel, n_lat_q_tiles=nq_lat, k_tile=ATT_K_TILE),
        grid=(batch, ATT_KV_HEADS, nq_lat + (nq_ctx if with_ctx_queries else 0)),
        in_specs=[q_spec, kc_spec, kc_spec, kl_spec, kl_spec],
        out_specs=q_spec,
        out_shape=jax.ShapeDtypeStruct((t_rows if with_ctx_queries else batch * seq, ATT_HEADS * HEAD_DIM),
                                       BF16),
        compiler_params=_params("arbitrary", "arbitrary", "arbitrary"),
        name="gqa_attention",
    )(aq, ak, av, ak, av)


def _merge_kernel(x_ref, att_ref, of_ref, ob_ref, hg_ref, ga_ref, gh_ref, g1_ref, hgn_ref, post_ref,
                  wa_ref, wh_ref, wo_ref, o_ref, hg_scr):
    hgn = hgn_ref[...]
    for i in range(HGRN_HEADS):
        o = _rms(of_ref[i] + ob_ref[i], hgn)
        hg_scr[:, i * HEAD_DIM:(i + 1) * HEAD_DIM] = (o * hg_ref[i].astype(F32)).astype(BF16)
    y = (ga_ref[...].astype(F32) * _dot(att_ref[...], wa_ref[...])
         + gh_ref[...].astype(F32) * _dot(hg_scr[...], wh_ref[...]))
    z = _rms(_dot(y.astype(BF16), wo_ref[...]), post_ref[...])
    o_ref[...] = x_ref[...] + g1_ref[...] * z


def _merge(x, att, o_f, o_b, hg, ga, gh, mod, hg_norm, post_norm, w_att, w_hg, w_out, *, rows, n_lat,
           seq):
    d = x.shape[1]
    tm = TOKEN_TILE
    n_lat_tiles = n_lat // tm
    per_seq = seq // tm

    def mod_row(i):
        return jnp.where(i < n_lat_tiles, i // per_seq, mod.shape[0] - 1 - 3)

    tok = pl.BlockSpec((tm, d), lambda i: (i, 0))
    hm = pl.BlockSpec((HGRN_HEADS, tm, HEAD_DIM), lambda i: (0, i, 0))
    vec = lambda w: pl.BlockSpec((1, w), lambda i: (0, 0))
    wgt = _resident((d, d), lambda i: (0, 0))
    return pl.pallas_call(
        _merge_kernel,
        grid=(rows // tm,),
        in_specs=[tok, tok, hm, hm, hm, tok, tok,
                  pl.BlockSpec((None, 1, d), lambda i: (mod_row(i), 0, 2)),
                  vec(HEAD_DIM), vec(d), wgt, wgt, wgt],
        out_specs=tok,
        out_shape=jax.ShapeDtypeStruct((rows, d), F32),
        scratch_shapes=[pltpu.VMEM((tm, d), BF16)],
        compiler_params=_params("arbitrary"),
        name="branch_merge",
    )(x, att, o_f, o_b, hg, ga, gh, mod, hg_norm, post_norm, w_att, w_hg, w_out)


def _ffn_kernel(x_ref, sh_ref, sc_ref, g2_ref, pre_ref, post_ref, wg_ref, wu_ref, wd_ref, o_ref, a_scr,
                *, ff_tile):
    x = x_ref[...]
    f = (_rms(x, pre_ref[...]) * (1.0 + sc_ref[...]) + sh_ref[...]).astype(BF16)
    for c in range(0, wg_ref.shape[1], ff_tile):
        g = _dot(f, wg_ref[:, c:c + ff_tile])
        u = _dot(f, wu_ref[:, c:c + ff_tile])
        a_scr[:, c:c + ff_tile] = (_silu(g) * u).astype(BF16)
    y = _rms(_dot(a_scr[...], wd_ref[...]), post_ref[...])
    o_ref[...] = x + g2_ref[...] * y


def _dense_ffn(x, mod, pre_norm, post_norm, w_gate, w_up, w_down, *, n_lat, seq):
    rows, d = x.shape
    d_ff = w_gate.shape[1]
    tm = TOKEN_TILE
    n_lat_tiles = n_lat // tm
    per_seq = seq // tm

    def mod_row(i):
        return jnp.where(i < n_lat_tiles, i // per_seq, mod.shape[0] - 1 - 3)

    tok = pl.BlockSpec((tm, d), lambda i: (i, 0))
    vec = pl.BlockSpec((1, d), lambda i: (0, 0))
    mod_spec = lambda col: pl.BlockSpec((None, 1, d), lambda i: (mod_row(i), 0, col))
    return pl.pallas_call(
        functools.partial(_ffn_kernel, ff_tile=256),
        grid=(rows // tm,),
        in_specs=[tok, mod_spec(3), mod_spec(4), mod_spec(5), vec, vec,
                  _resident((d, d_ff), lambda i: (0, 0)), _resident((d, d_ff), lambda i: (0, 0)),
                  _resident((d_ff, d), lambda i: (0, 0))],
        out_specs=tok,
        out_shape=jax.ShapeDtypeStruct((rows, d), F32),
        scratch_shapes=[pltpu.VMEM((tm, d_ff), BF16)],
        compiler_params=_params("arbitrary"),
        name="dense_swiglu",
    )(x, mod, mod, mod, pre_norm, post_norm, w_gate, w_up, w_down)


def _router_kernel(x_ref, sh_ref, sc_ref, pre_ref, wr_ref, f_ref, route_ref):
    f = _rms(x_ref[...], pre_ref[...]) * (1.0 + sc_ref[...]) + sh_ref[...]
    f_ref[...] = f
    logits = jnp.dot(f, wr_ref[...], preferred_element_type=F32, precision=lax.Precision.HIGHEST)
    lane = lax.broadcasted_iota(jnp.int32, logits.shape, 1)
    neg = jnp.float32(-jnp.inf)
    logits = jnp.where(lane < N_EXPERTS, logits, neg)
    m1 = jnp.max(logits, axis=-1, keepdims=True)
    i1 = jnp.min(jnp.where(logits == m1, lane, LANES), axis=-1, keepdims=True)
    rest = jnp.where(lane == i1, neg, logits)
    m2 = jnp.max(rest, axis=-1, keepdims=True)
    i2 = jnp.min(jnp.where(rest == m2, lane, LANES), axis=-1, keepdims=True)
    w1 = 1.0 / (1.0 + jnp.exp(m2 - m1))
    w2 = 1.0 - w1
    route = jnp.where(lane == 0, i1.astype(F32),
                      jnp.where(lane == 1, i2.astype(F32),
                                jnp.where(lane == 2, w1, jnp.where(lane == 3, w2, 0.0))))
    route_ref[...] = route


def _router(x, mod, pre_norm, w_router_padded, *, seq):
    rows, d = x.shape
    tm = TOKEN_TILE
    per_seq = seq // tm
    tok = pl.BlockSpec((tm, d), lambda i: (i, 0))
    mod_spec = lambda col: pl.BlockSpec((None, 1, d), lambda i: (i // per_seq, 0, col))
    return pl.pallas_call(
        _router_kernel,
        grid=(rows // tm,),
        in_specs=[tok, mod_spec(3), mod_spec(4), pl.BlockSpec((1, d), lambda i: (0, 0)),
                  pl.BlockSpec((d, LANES), lambda i: (0, 0))],
        out_specs=[tok, pl.BlockSpec((tm, LANES), lambda i: (i, 0))],
        out_shape=[jax.ShapeDtypeStruct((rows, d), F32), jax.ShapeDtypeStruct((rows, LANES), F32)],
        compiler_params=_params("arbitrary"),
        name="moe_router",
    )(x, mod, mod, pre_norm, w_router_padded)


def _row_copy(src_hbm, src_row, dst_ref, dst_row, sem):
    return pltpu.make_async_copy(src_hbm.at[pl.ds(src_row, 1)], dst_ref.at[pl.ds(dst_row, 1)], sem)


def _scatter_kernel(dest_ref, f_hbm, xs_in_hbm, xs_hbm, sem):
    del xs_in_hbm
    n = dest_ref.shape[0]
    base = pl.program_id(0) * n

    def copy(a):
        return _row_copy(f_hbm, (base + a) // TOP_K, xs_hbm, dest_ref[a], sem)

    def issue(a, carry):
        copy(a).start()

        @pl.when(a >= ROW_DMA_WINDOW)
        def _():
            copy(a - ROW_DMA_WINDOW).wait()
        return carry

    lax.fori_loop(0, n, issue, 0)

    def drain(a, carry):
        copy(a).wait()
        return carry

    lax.fori_loop(n - ROW_DMA_WINDOW, n, drain, 0)


def _scatter_rows(f, dest, n_rows):
    n_assign = dest.shape[0]
    d = f.shape[1]
    return pl.pallas_call(
        _scatter_kernel,
        grid=(n_assign // ROW_DMA_STEP,),
        in_specs=[pl.BlockSpec((ROW_DMA_STEP,), lambda i: (i,), memory_space=pltpu.SMEM),
                  pl.BlockSpec(memory_space=pl.ANY), pl.BlockSpec(memory_space=pl.ANY)],
        out_specs=pl.BlockSpec(memory_space=pl.ANY),
        out_shape=jax.ShapeDtypeStruct((n_rows, d), F32),
        scratch_shapes=[pltpu.SemaphoreType.DMA(())],
        input_output_aliases={2: 0},
        compiler_params=pltpu.CompilerParams(dimension_semantics=("arbitrary",), has_side_effects=True),
        name="moe_scatter_rows",
    )(dest, f, jnp.zeros((n_rows, d), F32))


def _expert_kernel(be_ref, nb_ref, x_ref, wg_ref, wu_ref, wd_ref, o_ref, a_scr, *, ff_tile):
    del be_ref

    @pl.when(pl.program_id(0) < nb_ref[0])
    def _():
        xb = x_ref[...].astype(BF16)
        for c in range(0, wg_ref.shape[1], ff_tile):
            g = _dot(xb, wg_ref[:, c:c + ff_tile])
            u = _dot(xb, wu_ref[:, c:c + ff_tile])
            a_scr[:, c:c + ff_tile] = (_silu(g) * u).astype(BF16)
        o_ref[...] = _dot(a_scr[...], wd_ref[...])

    @pl.when(pl.program_id(0) >= nb_ref[0])
    def _():
        o_ref[...] = jnp.zeros(o_ref.shape, F32)


def _expert_ffn(xs, block_expert, n_used, w_gate, w_up, w_down):
    n_rows, d = xs.shape
    d_ff = w_gate.shape[2]
    n_blocks = n_rows // MOE_BLOCK
    blk = pl.BlockSpec((MOE_BLOCK, d), lambda i, be, nb: (i, 0))
    grid_spec = pltpu.PrefetchScalarGridSpec(
        num_scalar_prefetch=2,
        grid=(n_blocks,),
        in_specs=[blk,
                  pl.BlockSpec((None, d, d_ff), lambda i, be, nb: (be[i], 0, 0), pipeline_mode=pl.Buffered(1)),
                  pl.BlockSpec((None, d, d_ff), lambda i, be, nb: (be[i], 0, 0), pipeline_mode=pl.Buffered(1)),
                  pl.BlockSpec((None, d_ff, d), lambda i, be, nb: (be[i], 0, 0), pipeline_mode=pl.Buffered(1))],
        out_specs=blk,
        scratch_shapes=[pltpu.VMEM((MOE_BLOCK, d_ff), BF16)],
    )
    return pl.pallas_call(
        functools.partial(_expert_kernel, ff_tile=512),
        grid_spec=grid_spec,
        out_shape=jax.ShapeDtypeStruct((n_rows, d), F32),
        compiler_params=_params("arbitrary"),
        name="moe_expert_swiglu",
    )(block_expert, n_used, xs, w_gate, w_up, w_down)


def _combine_kernel(dest_ref, ys_hbm, x_ref, route_ref, g2_ref, post_ref, o_ref, buf, sem):
    n = dest_ref.shape[0]

    def copy(a):
        return _row_copy(ys_hbm, dest_ref[a], buf.at[a % TOP_K], a // TOP_K, sem)

    def issue(a, carry):
        copy(a).start()

        @pl.when(a >= ROW_DMA_WINDOW)
        def _():
            copy(a - ROW_DMA_WINDOW).wait()
        return carry

    lax.fori_loop(0, n, issue, 0)

    def drain(a, carry):
        copy(a).wait()
        return carry

    lax.fori_loop(n - ROW_DMA_WINDOW, n, drain, 0)

    route = route_ref[...]
    y = route[:, 2:3] * buf[0] + route[:, 3:4] * buf[1]
    o_ref[...] = x_ref[...] + g2_ref[...] * _rms(y, post_ref[...])


def _combine(ys, dest, x, route, mod, post_norm, *, seq):
    rows, d = x.shape
    tm = ROW_DMA_STEP // TOP_K
    per_seq = seq // tm
    tok = pl.BlockSpec((tm, d), lambda i: (i, 0))
    return pl.pallas_call(
        _combine_kernel,
        grid=(rows // tm,),
        in_specs=[pl.BlockSpec((ROW_DMA_STEP,), lambda i: (i,), memory_space=pltpu.SMEM),
                  pl.BlockSpec(memory_space=pl.ANY), tok,
                  pl.BlockSpec((tm, LANES), lambda i: (i, 0)),
                  pl.BlockSpec((None, 1, d), lambda i: (i // per_seq, 0, 5)),
                  pl.BlockSpec((1, d), lambda i: (0, 0))],
        out_specs=tok,
        out_shape=jax.ShapeDtypeStruct((rows, d), F32),
        scratch_shapes=[pltpu.VMEM((TOP_K, tm, d), F32), pltpu.SemaphoreType.DMA(())],
        compiler_params=_params("arbitrary"),
        name="moe_combine",
    )(dest, ys, x, route, mod, post_norm)


def _moe_ffn(x, mod, pre_norm, post_norm, w_router, w_gate, w_up, w_down, *, seq):
    n_tok, d = x.shape
    n_assign = n_tok * TOP_K
    n_blocks = -(-n_assign // MOE_BLOCK) + N_EXPERTS
    n_rows = n_blocks * MOE_BLOCK
    wr = jnp.pad(w_router, ((0, 0), (0, LANES - N_EXPERTS)))
    f, route = _router(x, mod, pre_norm, wr, seq=seq)

    flat_e = route[:, :TOP_K].astype(jnp.int32).reshape(-1)
    onehot = (flat_e[:, None] == jnp.arange(N_EXPERTS, dtype=jnp.int32)[None, :]).astype(jnp.int32)
    csum = jnp.cumsum(onehot, axis=0)
    counts = csum[-1]
    rank = jnp.sum((csum - onehot) * onehot, axis=1)
    padded = (counts + MOE_BLOCK - 1) // MOE_BLOCK * MOE_BLOCK
    pend = jnp.cumsum(padded)
    dest = ((pend - padded)[flat_e] + rank).astype(jnp.int32)
    block_expert = jnp.minimum(
        jnp.searchsorted(pend, jnp.arange(n_blocks, dtype=jnp.int32) * MOE_BLOCK, side='right'),
        N_EXPERTS - 1).astype(jnp.int32)
    n_used = (pend[-1:] // MOE_BLOCK).astype(jnp.int32)

    xs = _scatter_rows(f, dest, n_rows)
    ys = _expert_ffn(xs, block_expert, n_used, w_gate, w_up, w_down)
    return _combine(ys, dest, x, route, mod, post_norm, seq=seq)


def _rope_tables(seq, pad_rows):
    pos = jnp.arange(seq)
    nfreq = HEAD_DIM // 4
    inv_freq = ROPE_THETA ** (-jnp.arange(nfreq, dtype=F32) / nfreq)
    ang = jnp.concatenate([(pos // GRID_W).astype(F32)[:, None] * inv_freq,
                           (pos % GRID_W).astype(F32)[:, None] * inv_freq], axis=-1)
    ang = jnp.concatenate([ang, ang], axis=-1)
    sign = jnp.where(jnp.arange(HEAD_DIM) < HEAD_DIM // 2, -1.0, 1.0).astype(F32)
    cos = jnp.concatenate([jnp.cos(ang), jnp.ones((pad_rows, HEAD_DIM), F32)], axis=0)
    sin = jnp.concatenate([jnp.sin(ang) * sign, jnp.zeros((pad_rows, HEAD_DIM), F32)], axis=0)
    return cos, sin


def kernel(x, c, ctx, c_ctx, w_mod, b_mod, pre_mix_norm, post_mix_norm, pre_ffn_norm, post_ffn_norm, w_in, q_norm, k_norm, hg_norm, hg_lb_logits, w_att_branch, w_hg_branch, w_out, ffn_w_gate, ffn_w_up, ffn_w_down, moe_router, moe_w_gate, moe_w_up, moe_w_down):
    batch, seq, d = x.shape
    ctx_len = ctx.shape[1]
    depth = w_mod.shape[0]
    n_lat = batch * seq
    assert d == D_MODEL and w_in.shape[2] == PROJ_WIDTH
    assert seq % HGRN_STEP == 0 and ctx_len % HGRN_STEP == 0 and n_lat % ROW_DMA_STEP == 0
    assert seq % ctx_len == 0 and ctx_len % TOKEN_TILE == 0 and seq % ATT_K_TILE == 0

    c_rows = jnp.concatenate([c, c_ctx[None, :], jnp.zeros((3, d), F32)], axis=0)
    mod_all = _mod_vectors(c_rows, w_mod, b_mod)
    cos_tab, sin_tab = _rope_tables(seq, TOKEN_TILE)
    row = lambda v: v.reshape(1, -1)

    xt = jnp.concatenate([x.reshape(n_lat, d), ctx.reshape(batch * ctx_len, d)], axis=0)
    for layer in range(depth):
        last = layer == depth - 1
        mod = mod_all[layer].reshape(batch + 4, 1, N_MOD * d)
        (aq, ak, av, hq, kf, lf, kb, lb, hi, hg, ga, gh) = _in_projection(
            xt, mod, row(pre_mix_norm[layer]), w_in[layer].astype(BF16), cos_tab, sin_tab,
            row(q_norm[layer]), row(k_norm[layer]), hg_lb_logits, layer=layer, n_lat=n_lat, seq=seq)
        o_f, o_b = _hgrn_scan(hq, hi, kf, lf, kb, lb, batch=batch, seq=seq, ctx_len=ctx_len)
        att = _attention(aq, ak, av, batch=batch, seq=seq, ctx_len=ctx_len, with_ctx_queries=not last)
        rows = n_lat if last else xt.shape[0]
        xt = _merge(xt, att, o_f, o_b, hg, ga, gh, mod, row(hg_norm[layer]), row(post_mix_norm[layer]),
                    w_att_branch[layer].astype(BF16), w_hg_branch[layer].astype(BF16),
                    w_out[layer].astype(BF16), rows=rows, n_lat=n_lat, seq=seq)
        idx = layer // 2
        if layer % 2 == 0:
            xt = _dense_ffn(xt, mod, row(pre_ffn_norm[layer]), row(post_ffn_norm[layer]),
                            ffn_w_gate[idx].astype(BF16), ffn_w_up[idx].astype(BF16),
                            ffn_w_down[idx].astype(BF16), n_lat=n_lat, seq=seq)
        else:
            assert last, "the expert mixer is implemented for latent tokens only"
            xt = _moe_ffn(xt, mod, row(pre_ffn_norm[layer]), row(post_ffn_norm[layer]), moe_router[idx],
                          moe_w_gate[idx].astype(BF16), moe_w_up[idx].astype(BF16),
                          moe_w_down[idx].astype(BF16), seq=seq)
    return xt[:n_lat].reshape(batch, seq, d)
```

```python
import functools

import jax
import jax.numpy as jnp
from jax import lax
from jax.experimental import pallas as pl
from jax.experimental.pallas import tpu as pltpu

F32 = jnp.float32
BF16 = jnp.bfloat16

D_MODEL = 1024
NORM_EPS = 1e-6
N_MOD = 6
GRID_W = 64
ROPE_THETA = 10000.0

HEAD_DIM = 128
ATT_HEADS = 8
ATT_KV_HEADS = 2
ATT_GROUP = ATT_HEADS // ATT_KV_HEADS
KV_WIDTH = ATT_KV_HEADS * HEAD_DIM
HGRN_HEADS = 8

N_EXPERTS = 8
TOP_K = 2
MOE_BLOCK = 256

SUBLANES = 8
LANES = 128
VMEM_LIMIT_BYTES = 56 * 1024 * 1024

TOKEN_TILE = 256
HGRN_CHUNK = 64
HGRN_STEP = 256
ATT_Q_TILE = 128
ATT_K_TILE = 512

_C_AQ = 0
_C_AK = _C_AQ + ATT_HEADS * HEAD_DIM
_C_AV = _C_AK + KV_WIDTH
_C_HQ = _C_AV + KV_WIDTH
_C_HFF = _C_HQ + D_MODEL
_C_HFB = _C_HFF + D_MODEL
_C_HI = _C_HFB + D_MODEL
_C_HG = _C_HI + D_MODEL
_C_GA = _C_HG + D_MODEL
_C_GH = _C_GA + D_MODEL
PROJ_WIDTH = _C_GH + D_MODEL


def _params(*sem):
    return pltpu.CompilerParams(dimension_semantics=sem, vmem_limit_bytes=VMEM_LIMIT_BYTES)


def _resident(shape, index_map):
    return pl.BlockSpec(shape, index_map, pipeline_mode=pl.Buffered(1))


def _rms(t, gain):
    return t * lax.rsqrt(jnp.mean(t * t, axis=-1, keepdims=True) + NORM_EPS) * gain


def _silu(t):
    return t * jax.nn.sigmoid(t)


def _dot(a, b):
    return jnp.dot(a, b, preferred_element_type=F32)


def _dot_nt(a, b):
    return lax.dot_general(a, b, (((1,), (1,)), ((), ())), preferred_element_type=F32)


def _dot_tn(a, b):
    return lax.dot_general(a, b, (((0,), (0,)), ((), ())), preferred_element_type=F32)


def _mod_kernel(c_ref, w_ref, b_ref, o_ref):
    a = _silu(c_ref[...])
    o_ref[...] = jnp.dot(a, w_ref[...], preferred_element_type=F32,
                         precision=lax.Precision.HIGHEST) + b_ref[...]


def _mod_vectors(c_rows, w_mod, b_mod):
    depth, d, width = w_mod.shape
    rows = c_rows.shape[0]
    tn = 1536
    return pl.pallas_call(
        _mod_kernel,
        grid=(depth, width // tn),
        in_specs=[
            pl.BlockSpec((rows, d), lambda l, j: (0, 0)),
            pl.BlockSpec((None, d, tn), lambda l, j: (l, 0, j)),
            pl.BlockSpec((None, 1, tn), lambda l, j: (l, 0, j)),
        ],
        out_specs=pl.BlockSpec((None, rows, tn), lambda l, j: (l, 0, j)),
        out_shape=jax.ShapeDtypeStruct((depth, rows, width), F32),
        compiler_params=_params("arbitrary", "arbitrary"),
        name="mod_vectors",
    )(c_rows, w_mod, b_mod.reshape(depth, 1, width))


def _inproj_kernel(x_ref, sh_ref, sc_ref, gain_ref, w_ref, cos_ref, sin_ref, qn_ref, kn_ref, lbl_ref,
                   aq_ref, ak_ref, av_ref, hq_ref, kf_ref, lf_ref, kb_ref, lb_ref, hi_ref, hg_ref,
                   ga_ref, gh_ref, *, layer):
    h = _rms(x_ref[...], gain_ref[...])
    hb = (h * (1.0 + sc_ref[...]) + sh_ref[...]).astype(BF16)
    cos = cos_ref[...]
    sin = sin_ref[...]

    def proj(c0, width):
        return _dot(hb, w_ref[:, c0:c0 + width])

    def head(t, i):
        return t[:, i * HEAD_DIM:(i + 1) * HEAD_DIM]

    def norm_rope(t, gain):
        r = _rms(t, gain)
        return r * cos + pltpu.roll(r, HEAD_DIM // 2, 1) * sin

    qn = qn_ref[...]
    for c in range(2):
        t = proj(_C_AQ + c * 512, 512)
        for i in range(4):
            hh = c * 4 + i
            aq_ref[:, hh * HEAD_DIM:(hh + 1) * HEAD_DIM] = (
                norm_rope(head(t, i), qn) * (HEAD_DIM ** -0.5)).astype(BF16)
    t = proj(_C_AK, 2 * KV_WIDTH)
    kn = kn_ref[...]
    for i in range(ATT_KV_HEADS):
        ak_ref[:, i * HEAD_DIM:(i + 1) * HEAD_DIM] = norm_rope(head(t, i), kn).astype(BF16)
    av_ref[...] = t[:, KV_WIDTH:].astype(BF16)

    for c in range(2):
        t = proj(_C_HQ + c * 512, 512)
        for i in range(4):
            hq_ref[c * 4 + i] = _silu(head(t, i)).astype(BF16)

    lg = lbl_ref[...]
    depth = lg.shape[0]
    mx = lg[0:1]
    for j in range(1, depth):
        mx = jnp.maximum(mx, lg[j:j + 1])
    es = [jnp.exp(lg[j:j + 1] - mx) for j in range(depth)]
    tot = es[0]
    for j in range(1, depth):
        tot = tot + es[j]
    low = jnp.zeros_like(tot)
    for j in range(1, layer + 1):
        low = low + es[j] / tot

    for c0, k_ref, l_ref in ((_C_HFF, kf_ref, lf_ref), (_C_HFB, kb_ref, lb_ref)):
        for c in range(2):
            t = proj(c0 + c * 512, 512)
            for i in range(4):
                hh = c * 4 + i
                lo = low[:, hh * HEAD_DIM:(hh + 1) * HEAD_DIM]
                f = lo + (1.0 - lo) * jax.nn.sigmoid(head(t, i))
                k_ref[hh] = (1.0 - f).astype(BF16)
                l_ref[hh] = jnp.log(f)

    for c in range(2):
        t = proj(_C_HI + c * 512, 512)
        for i in range(4):
            hi_ref[c * 4 + i] = head(t, i).astype(BF16)
    for c in range(2):
        t = proj(_C_HG + c * 512, 512)
        for i in range(4):
            hg_ref[c * 4 + i] = _silu(head(t, i)).astype(BF16)

    for c in range(2):
        ga_ref[:, c * 512:(c + 1) * 512] = jax.nn.sigmoid(proj(_C_GA + c * 512, 512)).astype(BF16)
    for c in range(2):
        gh_ref[:, c * 512:(c + 1) * 512] = jax.nn.sigmoid(proj(_C_GH + c * 512, 512)).astype(BF16)


def _in_projection(x, mod, gain, w_in, cos_tab, sin_tab, q_norm, k_norm, lb_logits, *, layer, n_lat,
                   seq):
    t_rows, d = x.shape
    tm = TOKEN_TILE
    n_lat_tiles = n_lat // tm
    per_seq = seq // tm

    def mod_row(i):
        return jnp.where(i < n_lat_tiles, i // per_seq, mod.shape[0] - 1 - 3)

    def pos_blk(i):
        return jnp.where(i < n_lat_tiles, i % per_seq, per_seq)

    tok = lambda w: pl.BlockSpec((tm, w), lambda i: (i, 0))
    hm = pl.BlockSpec((HGRN_HEADS, tm, HEAD_DIM), lambda i: (0, i, 0))
    hm_shape = lambda dt: jax.ShapeDtypeStruct((HGRN_HEADS, t_rows, HEAD_DIM), dt)
    tm_shape = lambda w: jax.ShapeDtypeStruct((t_rows, w), BF16)
    return pl.pallas_call(
        functools.partial(_inproj_kernel, layer=layer),
        grid=(t_rows // tm,),
        in_specs=[
            tok(d),
            pl.BlockSpec((None, 1, d), lambda i: (mod_row(i), 0, 0)),
            pl.BlockSpec((None, 1, d), lambda i: (mod_row(i), 0, 1)),
            pl.BlockSpec((1, d), lambda i: (0, 0)),
            _resident((d, PROJ_WIDTH), lambda i: (0, 0)),
            pl.BlockSpec((tm, HEAD_DIM), lambda i: (pos_blk(i), 0)),
            pl.BlockSpec((tm, HEAD_DIM), lambda i: (pos_blk(i), 0)),
            pl.BlockSpec((1, HEAD_DIM), lambda i: (0, 0)),
            pl.BlockSpec((1, HEAD_DIM), lambda i: (0, 0)),
            pl.BlockSpec(lb_logits.shape, lambda i: (0, 0)),
        ],
        out_specs=[tok(d), tok(KV_WIDTH), tok(KV_WIDTH), hm, hm, hm, hm, hm, hm, hm, tok(d), tok(d)],
        out_shape=[tm_shape(d), tm_shape(KV_WIDTH), tm_shape(KV_WIDTH), hm_shape(BF16), hm_shape(BF16),
                   hm_shape(F32), hm_shape(BF16), hm_shape(F32), hm_shape(BF16), hm_shape(BF16),
                   tm_shape(d), tm_shape(d)],
        compiler_params=_params("arbitrary"),
        name=f"in_projection_l{layer}",
    )(x, mod, mod, gain, w_in, cos_tab, sin_tab, q_norm, k_norm, lb_logits)


def _hgrn_constants(c):
    t = lax.broadcasted_iota(jnp.int32, (c, c), 0)
    s = lax.broadcasted_iota(jnp.int32, (c, c), 1)
    tri = ((t >= s).astype(BF16), (t <= s).astype(BF16))
    same8 = (t // SUBLANES) == (s // SUBLANES)
    diag = (same8 & (t >= s), same8 & (t <= s))
    levels = []
    blk = 2 * SUBLANES
    while blk <= c:
        half = blk // 2
        same = (t // blk) == (s // blk)
        t_hi = (t % blk) >= half
        s_hi = (s % blk) >= half
        levels.append((blk, (same & t_hi & ~s_hi, same & ~t_hi & s_hi)))
        blk *= 2
    r = lax.broadcasted_iota(jnp.int32, (SUBLANES * LANES, c), 0)
    j = lax.broadcasted_iota(jnp.int32, (SUBLANES * LANES, c), 1)
    sel = ((j % SUBLANES) == (r // LANES)).astype(BF16)
    return tri, diag, levels, sel


def _hgrn_chunk(q, k, v, g, st_t, consts, rev):
    tri, diag, levels, sel = consts
    c = q.shape[0]
    qf = q.astype(F32)
    kf = k.astype(F32)

    g1 = g.astype(BF16)
    r1 = g - g1.astype(F32)
    g2 = r1.astype(BF16)
    g3 = (r1 - g2.astype(F32)).astype(BF16)
    cum = _dot(tri[rev], g1) + _dot(tri[rev], g2) + _dot(tri[rev], g3)
    tail = cum[0:1] if rev else cum[c - 1:c]

    out = _dot_nt((qf * jnp.exp(cum)).astype(BF16), st_t.astype(BF16))
    k_end = (kf * jnp.exp(tail - cum)).astype(BF16)
    st_new = st_t * jnp.exp(tail) + _dot_tn(v, k_end)

    a = jnp.zeros((c, c), F32)
    for blk, masks in levels:
        half = blk // 2
        rows = []
        for a0 in range(0, c, blk):
            r = a0 + half if rev else a0 + half - 1
            rows.append(jnp.broadcast_to(cum[r:r + 1, :], (blk, LANES)))
        mid = rows[0] if len(rows) == 1 else jnp.concatenate(rows, axis=0)
        q_l = (qf * jnp.exp(jnp.minimum(cum - mid, 0.0))).astype(BF16)
        k_l = (kf * jnp.exp(jnp.minimum(mid - cum, 0.0))).astype(BF16)
        a = a + jnp.where(masks[rev], _dot_nt(q_l, k_l), 0.0)

    nb = c // SUBLANES
    q3 = qf.reshape(nb, SUBLANES, LANES)
    k3 = kf.reshape(nb, SUBLANES, LANES)
    c3 = cum.reshape(nb, SUBLANES, LANES)
    parts = []
    for s in range(SUBLANES):
        e = jnp.exp(jnp.minimum(c3 - c3[:, s:s + 1, :], 0.0))
        parts.append((q3 * e * k3[:, s:s + 1, :]).reshape(c, LANES).astype(BF16))
    a = a + jnp.where(diag[rev], _dot(jnp.concatenate(parts, axis=1), sel), 0.0)

    out = out + _dot(a.astype(BF16), v)
    return out, st_new


def _hgrn_kernel(qf_ref, vf_ref, kf_ref, lf_ref, qb_ref, vb_ref, kb_ref, lb_ref, of_ref, ob_ref,
                 st_ref, *, chunk):
    @pl.when(pl.program_id(1) == 0)
    def _():
        st_ref[...] = jnp.zeros(st_ref.shape, F32)

    consts = _hgrn_constants(chunk)
    n_chunks = qf_ref.shape[1] // chunk

    def head_body(h, carry):
        st_f = st_ref[0, h]
        st_b = st_ref[1, h]
        for j in range(n_chunks):
            rows = pl.ds(j * chunk, chunk)
            o, st_f = _hgrn_chunk(qf_ref[h, rows, :], kf_ref[h, rows, :], vf_ref[h, rows, :],
                                  lf_ref[h, rows, :], st_f, consts, 0)
            of_ref[h, rows, :] = o
            rows = pl.ds((n_chunks - 1 - j) * chunk, chunk)
            o, st_b = _hgrn_chunk(qb_ref[h, rows, :], kb_ref[h, rows, :], vb_ref[h, rows, :],
                                  lb_ref[h, rows, :], st_b, consts, 1)
            ob_ref[h, rows, :] = o
        st_ref[0, h] = st_f
        st_ref[1, h] = st_b
        return carry

    lax.fori_loop(0, qf_ref.shape[0], head_body, 0)


def _hgrn_scan(hq, hi, kf, lf, kb, lb, *, batch, seq, ctx_len):
    heads, t_rows, dh = hq.shape
    ts = HGRN_STEP
    n_ctx = ctx_len // ts
    n_seq = seq // ts
    ctx0 = batch * n_seq

    def fwd(b, s):
        return jnp.where(s < n_ctx, ctx0 + b * n_ctx + s, b * n_seq + s - n_ctx)

    def bwd(b, s):
        return jnp.where(s < n_ctx, ctx0 + b * n_ctx + n_ctx - 1 - s, b * n_seq + n_seq - 1 - (s - n_ctx))

    spec_f = pl.BlockSpec((heads, ts, dh), lambda b, s: (0, fwd(b, s), 0))
    spec_b = pl.BlockSpec((heads, ts, dh), lambda b, s: (0, bwd(b, s), 0))
    out = jax.ShapeDtypeStruct((heads, t_rows, dh), F32)
    return pl.pallas_call(
        functools.partial(_hgrn_kernel, chunk=HGRN_CHUNK),
        grid=(batch, n_ctx + n_seq),
        in_specs=[spec_f, spec_f, spec_f, spec_f, spec_b, spec_b, spec_b, spec_b],
        out_specs=[spec_f, spec_b],
        out_shape=[out, out],
        scratch_shapes=[pltpu.VMEM((2, heads, dh, dh), F32)],
        compiler_params=_params("arbitrary", "arbitrary"),
        name="hgrn_scan",
    )(hq, hi, kf, lf, hq, hi, kb, lb)


def _attn_kernel(q_ref, kc_ref, vc_ref, kl_ref, vl_ref, o_ref, *, n_lat_q_tiles, k_tile):
    tq = q_ref.shape[0]
    q = jnp.concatenate([q_ref[:, i * HEAD_DIM:(i + 1) * HEAD_DIM] for i in range(ATT_GROUP)], axis=0)
    rows = q.shape[0]

    def update(carry, k, v):
        m, l, acc = carry
        s = _dot_nt(q, k)
        m_new = jnp.maximum(m, jnp.max(s, axis=-1, keepdims=True))
        alpha = jnp.exp(m - m_new)
        p = jnp.exp(s - m_new)
        l = alpha * l + jnp.sum(p, axis=-1, keepdims=True)
        acc = alpha * acc + _dot(p.astype(BF16), v)
        return m_new, l, acc

    carry = (jnp.full((rows, 1), -1e30, F32), jnp.zeros((rows, 1), F32), jnp.zeros((rows, HEAD_DIM), F32))
    carry = update(carry, kc_ref[...], vc_ref[...])
    n_chunks = jnp.where(pl.program_id(2) < n_lat_q_tiles, kl_ref.shape[0] // k_tile, 0)

    def body(j, c):
        r = pl.ds(pl.multiple_of(j * k_tile, k_tile), k_tile)
        return update(c, kl_ref[r, :], vl_ref[r, :])

    _, l, acc = lax.fori_loop(0, n_chunks, body, carry)
    out = acc / l
    for i in range(ATT_GROUP):
        o_ref[:, i * HEAD_DIM:(i + 1) * HEAD_DIM] = out[i * tq:(i + 1) * tq].astype(BF16)


def _attention(aq, ak, av, *, batch, seq, ctx_len, with_ctx_queries):
    t_rows = aq.shape[0]
    tq = ATT_Q_TILE
    nq_lat = seq // tq
    nq_ctx = ctx_len // tq
    ctx_q0 = batch * nq_lat
    ctx_k0 = batch * seq // ctx_len
    gw = ATT_GROUP * HEAD_DIM

    def q_blk(b, i):
        return jnp.where(i < nq_lat, b * nq_lat + i, ctx_q0 + b * nq_ctx + i - nq_lat)

    q_spec = pl.BlockSpec((tq, gw), lambda b, g, i: (q_blk(b, i), g))
    kc_spec = pl.BlockSpec((ctx_len, HEAD_DIM), lambda b, g, i: (ctx_k0 + b, g))
    kl_spec = pl.BlockSpec((seq, HEAD_DIM), lambda b, g, i: (b, g))
    return pl.pallas_call(
        functools.partial(_attn_kernel, n_lat_q_tiles=nq_lat, k_tile=ATT_K_TILE),
        grid=(batch, ATT_KV_HEADS, nq_lat + (nq_ctx if with_ctx_queries else 0)),
        in_specs=[q_spec, kc_spec, kc_spec, kl_spec, kl_spec],
        out_specs=q_spec,
        out_shape=jax.ShapeDtypeStruct((t_rows if with_ctx_queries else batch * seq, ATT_HEADS * HEAD_DIM),
                                       BF16),
        compiler_params=_params("arbitrary", "arbitrary", "arbitrary"),
        name="gqa_attention",
    )(aq, ak, av, ak, av)


def _merge_kernel(x_ref, att_ref, of_ref, ob_ref, hg_ref, ga_ref, gh_ref, g1_ref, hgn_ref, post_ref,
                  wa_ref, wh_ref, wo_ref, o_ref, hg_scr):
    hgn = hgn_ref[...]
    for i in range(HGRN_HEADS):
        o = _rms(of_ref[i] + ob_ref[i], hgn)
        hg_scr[:, i * HEAD_DIM:(i + 1) * HEAD_DIM] = (o * hg_ref[i].astype(F32)).astype(BF16)
    y = (ga_ref[...].astype(F32) * _dot(att_ref[...], wa_ref[...])
         + gh_ref[...].astype(F32) * _dot(hg_scr[...], wh_ref[...]))
    z = _rms(_dot(y.astype(BF16), wo_ref[...]), post_ref[...])
    o_ref[...] = x_ref[...] + g1_ref[...] * z


def _merge(x, att, o_f, o_b, hg, ga, gh, mod, hg_norm, post_norm, w_att, w_hg, w_out, *, rows, n_lat,
           seq):
    d = x.shape[1]
    tm = TOKEN_TILE
    n_lat_tiles = n_lat // tm
    per_seq = seq // tm

    def mod_row(i):
        return jnp.where(i < n_lat_tiles, i // per_seq, mod.shape[0] - 1 - 3)

    tok = pl.BlockSpec((tm, d), lambda i: (i, 0))
    hm = pl.BlockSpec((HGRN_HEADS, tm, HEAD_DIM), lambda i: (0, i, 0))
    vec = lambda w: pl.BlockSpec((1, w), lambda i: (0, 0))
    wgt = _resident((d, d), lambda i: (0, 0))
    return pl.pallas_call(
        _merge_kernel,
        grid=(rows // tm,),
        in_specs=[tok, tok, hm, hm, hm, tok, tok,
                  pl.BlockSpec((None, 1, d), lambda i: (mod_row(i), 0, 2)),
                  vec(HEAD_DIM), vec(d), wgt, wgt, wgt],
        out_specs=tok,
        out_shape=jax.ShapeDtypeStruct((rows, d), F32),
        scratch_shapes=[pltpu.VMEM((tm, d), BF16)],
        compiler_params=_params("arbitrary"),
        name="branch_merge",
    )(x, att, o_f, o_b, hg, ga, gh, mod, hg_norm, post_norm, w_att, w_hg, w_out)


def _ffn_kernel(x_ref, sh_ref, sc_ref, g2_ref, pre_ref, post_ref, wg_ref, wu_ref, wd_ref, o_ref, a_scr,
                *, ff_tile):
    x = x_ref[...]
    f = (_rms(x, pre_ref[...]) * (1.0 + sc_ref[...]) + sh_ref[...]).astype(BF16)
    for c in range(0, wg_ref.shape[1], ff_tile):
        g = _dot(f, wg_ref[:, c:c + ff_tile])
        u = _dot(f, wu_ref[:, c:c + ff_tile])
        a_scr[:, c:c + ff_tile] = (_silu(g) * u).astype(BF16)
    y = _rms(_dot(a_scr[...], wd_ref[...]), post_ref[...])
    o_ref[...] = x + g2_ref[...] * y


def _dense_ffn(x, mod, pre_norm, post_norm, w_gate, w_up, w_down, *, n_lat, seq):
    rows, d = x.shape
    d_ff = w_gate.shape[1]
    tm = TOKEN_TILE
    n_lat_tiles = n_lat // tm
    per_seq = seq // tm

    def mod_row(i):
        return jnp.where(i < n_lat_tiles, i // per_seq, mod.shape[0] - 1 - 3)

    tok = pl.BlockSpec((tm, d), lambda i: (i, 0))
    vec = pl.BlockSpec((1, d), lambda i: (0, 0))
    mod_spec = lambda col: pl.BlockSpec((None, 1, d), lambda i: (mod_row(i), 0, col))
    return pl.pallas_call(
        functools.partial(_ffn_kernel, ff_tile=256),
        grid=(rows // tm,),
        in_specs=[tok, mod_spec(3), mod_spec(4), mod_spec(5), vec, vec,
                  _resident((d, d_ff), lambda i: (0, 0)), _resident((d, d_ff), lambda i: (0, 0)),
                  _resident((d_ff, d), lambda i: (0, 0))],
        out_specs=tok,
        out_shape=jax.ShapeDtypeStruct((rows, d), F32),
        scratch_shapes=[pltpu.VMEM((tm, d_ff), BF16)],
        compiler_params=_params("arbitrary"),
        name="dense_swiglu",
    )(x, mod, mod, mod, pre_norm, post_norm, w_gate, w_up, w_down)


def _router_kernel(x_ref, sh_ref, sc_ref, pre_ref, wr_ref, f_ref, route_ref):
    f = _rms(x_ref[...], pre_ref[...]) * (1.0 + sc_ref[...]) + sh_ref[...]
    f_ref[...] = f
    logits = jnp.dot(f, wr_ref[...], preferred_element_type=F32, precision=lax.Precision.HIGHEST)
    lane = lax.broadcasted_iota(jnp.int32, logits.shape, 1)
    neg = jnp.float32(-jnp.inf)
    logits = jnp.where(lane < N_EXPERTS, logits, neg)
    m1 = jnp.max(logits, axis=-1, keepdims=True)
    i1 = jnp.min(jnp.where(logits == m1, lane, LANES), axis=-1, keepdims=True)
    rest = jnp.where(lane == i1, neg, logits)
    m2 = jnp.max(rest, axis=-1, keepdims=True)
    i2 = jnp.min(jnp.where(rest == m2, lane, LANES), axis=-1, keepdims=True)
    w1 = 1.0 / (1.0 + jnp.exp(m2 - m1))
    w2 = 1.0 - w1
    route = jnp.where(lane == 0, i1.astype(F32),
                      jnp.where(lane == 1, i2.astype(F32),
                                jnp.where(lane == 2, w1, jnp.where(lane == 3, w2, 0.0))))
    route_ref[...] = route


def _router(x, mod, pre_norm, w_router_padded, *, seq):
    rows, d = x.shape
    tm = TOKEN_TILE
    per_seq = seq // tm
    tok = pl.BlockSpec((tm, d), lambda i: (i, 0))
    mod_spec = lambda col: pl.BlockSpec((None, 1, d), lambda i: (i // per_seq, 0, col))
    return pl.pallas_call(
        _router_kernel,
        grid=(rows // tm,),
        in_specs=[tok, mod_spec(3), mod_spec(4), pl.BlockSpec((1, d), lambda i: (0, 0)),
                  pl.BlockSpec((d, LANES), lambda i: (0, 0))],
        out_specs=[tok, pl.BlockSpec((tm, LANES), lambda i: (i, 0))],
        out_shape=[jax.ShapeDtypeStruct((rows, d), F32), jax.ShapeDtypeStruct((rows, LANES), F32)],
        compiler_params=_params("arbitrary"),
        name="moe_router",
    )(x, mod, mod, pre_norm, w_router_padded)


def _expert_kernel(be_ref, tok_ref, tok_next_ref, dst_ref, f_hbm, wg_ref, wu_ref, wd_ref, ysel_hbm,
                   xbuf, ybuf, a_scr, gsem, ssem, *, ff_tile):
    del be_ref
    i = pl.program_id(0)
    last = pl.num_programs(0) - 1
    slot = i % 2
    rows = xbuf.shape[1]

    def gather(idx_ref, s):
        def body(r, carry):
            pltpu.make_async_copy(f_hbm.at[pl.ds(idx_ref[r], 1)], xbuf.at[s, pl.ds(r, 1)],
                                  gsem.at[s]).start()
            return carry
        lax.fori_loop(0, rows, body, 0)

    def wait_gather(s):
        pltpu.make_async_copy(f_hbm.at[pl.ds(0, rows)], xbuf.at[s], gsem.at[s]).wait()

    def wait_scatter(s):
        pltpu.make_async_copy(ybuf.at[s], ysel_hbm.at[pl.ds(0, rows)], ssem.at[s]).wait()

    @pl.when(i == 0)
    def _():
        gather(tok_ref, 0)

    @pl.when(i < last)
    def _():
        gather(tok_next_ref, 1 - slot)

    wait_gather(slot)

    @pl.when(i >= 2)
    def _():
        wait_scatter(slot)

    xb = xbuf[slot].astype(BF16)
    for c in range(0, wg_ref.shape[1], ff_tile):
        g = _dot(xb, wg_ref[:, c:c + ff_tile])
        u = _dot(xb, wu_ref[:, c:c + ff_tile])
        a_scr[:, c:c + ff_tile] = (_silu(g) * u).astype(BF16)
    ybuf[slot] = _dot(a_scr[...], wd_ref[...])

    def scatter(r, carry):
        pltpu.make_async_copy(ybuf.at[slot, pl.ds(r, 1)], ysel_hbm.at[pl.ds(dst_ref[r], 1)],
                              ssem.at[slot]).start()
        return carry
    lax.fori_loop(0, rows, scatter, 0)

    @pl.when(i == last)
    def _():
        wait_scatter(slot)

        @pl.when(i >= 1)
        def _():
            wait_scatter(1 - slot)


def _expert_ffn(f, row_tok, row_dst, block_expert, w_gate, w_up, w_down):
    n_rows = row_tok.shape[0]
    d = f.shape[1]
    d_ff = w_gate.shape[2]
    n_blocks = n_rows // MOE_BLOCK
    idx = lambda fn: pl.BlockSpec((MOE_BLOCK,), fn, memory_space=pltpu.SMEM)
    wgt = lambda shape: pl.BlockSpec((None,) + shape, lambda i, be: (be[i], 0, 0),
                                     pipeline_mode=pl.Buffered(1))
    grid_spec = pltpu.PrefetchScalarGridSpec(
        num_scalar_prefetch=1,
        grid=(n_blocks,),
        in_specs=[idx(lambda i, be: (i,)),
                  idx(lambda i, be: (jnp.minimum(i + 1, n_blocks - 1),)),
                  idx(lambda i, be: (i,)),
                  pl.BlockSpec(memory_space=pl.ANY),
                  wgt((d, d_ff)), wgt((d, d_ff)), wgt((d_ff, d))],
        out_specs=pl.BlockSpec(memory_space=pl.ANY),
        scratch_shapes=[pltpu.VMEM((2, MOE_BLOCK, d), F32), pltpu.VMEM((2, MOE_BLOCK, d), F32),
                        pltpu.VMEM((MOE_BLOCK, d_ff), BF16),
                        pltpu.SemaphoreType.DMA((2,)), pltpu.SemaphoreType.DMA((2,))],
    )
    return pl.pallas_call(
        functools.partial(_expert_kernel, ff_tile=512),
        grid_spec=grid_spec,
        out_shape=jax.ShapeDtypeStruct((n_rows, d), F32),
        compiler_params=_params("arbitrary"),
        name="moe_expert_swiglu",
    )(block_expert, row_tok, row_tok, row_dst, f, w_gate, w_up, w_down)


def _combine_kernel(y_ref, x_ref, route_ref, g2_ref, post_ref, o_ref):
    d = x_ref.shape[1]
    route = route_ref[...]
    y = route[:, 2:3] * y_ref[:, :d] + route[:, 3:4] * y_ref[:, d:]
    o_ref[...] = x_ref[...] + g2_ref[...] * _rms(y, post_ref[...])


def _combine(y_pairs, x, route, mod, post_norm, *, seq):
    rows, d = x.shape
    tm = TOKEN_TILE
    per_seq = seq // tm
    tok = pl.BlockSpec((tm, d), lambda i: (i, 0))
    return pl.pallas_call(
        _combine_kernel,
        grid=(rows // tm,),
        in_specs=[pl.BlockSpec((tm, TOP_K * d), lambda i: (i, 0)), tok,
                  pl.BlockSpec((tm, LANES), lambda i: (i, 0)),
                  pl.BlockSpec((None, 1, d), lambda i: (i // per_seq, 0, 5)),
                  pl.BlockSpec((1, d), lambda i: (0, 0))],
        out_specs=tok,
        out_shape=jax.ShapeDtypeStruct((rows, d), F32),
        compiler_params=_params("arbitrary"),
        name="moe_combine",
    )(y_pairs, x, route, mod, post_norm)


def _moe_ffn(x, mod, pre_norm, post_norm, w_router, w_gate, w_up, w_down, *, seq):
    n_tok, d = x.shape
    n_assign = n_tok * TOP_K
    n_blocks = -(-n_assign // MOE_BLOCK) + N_EXPERTS
    n_rows = n_blocks * MOE_BLOCK
    wr = jnp.pad(w_router, ((0, 0), (0, LANES - N_EXPERTS)))
    f, route = _router(x, mod, pre_norm, wr, seq=seq)

    flat_e = route[:, :TOP_K].astype(jnp.int32).reshape(-1)
    onehot = (flat_e[:, None] == jnp.arange(N_EXPERTS, dtype=jnp.int32)[None, :]).astype(jnp.int32)
    csum = jnp.cumsum(onehot, axis=0)
    counts = csum[-1]
    rank = jnp.sum((csum - onehot) * onehot, axis=1)
    padded = (counts + MOE_BLOCK - 1) // MOE_BLOCK * MOE_BLOCK
    pend = jnp.cumsum(padded)
    dest = ((pend - padded)[flat_e] + rank).astype(jnp.int32)
    block_expert = jnp.minimum(
        jnp.searchsorted(pend, jnp.arange(n_blocks, dtype=jnp.int32) * MOE_BLOCK, side='right'),
        N_EXPERTS - 1).astype(jnp.int32)
    row_assign = jnp.full((n_rows,), -1, jnp.int32).at[dest].set(jnp.arange(n_assign, dtype=jnp.int32))
    is_pad = row_assign < 0
    row_tok = jnp.where(is_pad, 0, row_assign // TOP_K)
    row_dst = jnp.where(is_pad, n_assign - 1 + jnp.cumsum(is_pad.astype(jnp.int32)), row_assign)

    y_sel = _expert_ffn(f, row_tok, row_dst, block_expert, w_gate, w_up, w_down)
    return _combine(y_sel.reshape(n_rows // TOP_K, TOP_K * d), x, route, mod, post_norm, seq=seq)


def _rope_tables(seq, pad_rows):
    pos = jnp.arange(seq)
    nfreq = HEAD_DIM // 4
    inv_freq = ROPE_THETA ** (-jnp.arange(nfreq, dtype=F32) / nfreq)
    ang = jnp.concatenate([(pos // GRID_W).astype(F32)[:, None] * inv_freq,
                           (pos % GRID_W).astype(F32)[:, None] * inv_freq], axis=-1)
    ang = jnp.concatenate([ang, ang], axis=-1)
    sign = jnp.where(jnp.arange(HEAD_DIM) < HEAD_DIM // 2, -1.0, 1.0).astype(F32)
    cos = jnp.concatenate([jnp.cos(ang), jnp.ones((pad_rows, HEAD_DIM), F32)], axis=0)
    sin = jnp.concatenate([jnp.sin(ang) * sign, jnp.zeros((pad_rows, HEAD_DIM), F32)], axis=0)
    return cos, sin


def kernel(x, c, ctx, c_ctx, w_mod, b_mod, pre_mix_norm, post_mix_norm, pre_ffn_norm, post_ffn_norm, w_in, q_norm, k_norm, hg_norm, hg_lb_logits, w_att_branch, w_hg_branch, w_out, ffn_w_gate, ffn_w_up, ffn_w_down, moe_router, moe_w_gate, moe_w_up, moe_w_down):
    batch, seq, d = x.shape
    ctx_len = ctx.shape[1]
    depth = w_mod.shape[0]
    n_lat = batch * seq
    assert d == D_MODEL and w_in.shape[2] == PROJ_WIDTH
    assert seq % HGRN_STEP == 0 and ctx_len % HGRN_STEP == 0 and n_lat % TOKEN_TILE == 0
    assert seq % ctx_len == 0 and ctx_len % TOKEN_TILE == 0 and seq % ATT_K_TILE == 0

    c_rows = jnp.concatenate([c, c_ctx[None, :], jnp.zeros((3, d), F32)], axis=0)
    mod_all = _mod_vectors(c_rows, w_mod, b_mod)
    cos_tab, sin_tab = _rope_tables(seq, TOKEN_TILE)
    row = lambda v: v.reshape(1, -1)

    xt = jnp.concatenate([x.reshape(n_lat, d), ctx.reshape(batch * ctx_len, d)], axis=0)
    for layer in range(depth):
        last = layer == depth - 1
        mod = mod_all[layer].reshape(batch + 4, 1, N_MOD * d)
        (aq, ak, av, hq, kf, lf, kb, lb, hi, hg, ga, gh) = _in_projection(
            xt, mod, row(pre_mix_norm[layer]), w_in[layer].astype(BF16), cos_tab, sin_tab,
            row(q_norm[layer]), row(k_norm[layer]), hg_lb_logits, layer=layer, n_lat=n_lat, seq=seq)
        o_f, o_b = _hgrn_scan(hq, hi, kf, lf, kb, lb, batch=batch, seq=seq, ctx_len=ctx_len)
        att = _attention(aq, ak, av, batch=batch, seq=seq, ctx_len=ctx_len, with_ctx_queries=not last)
        rows = n_lat if last else xt.shape[0]
        xt = _merge(xt, att, o_f, o_b, hg, ga, gh, mod, row(hg_norm[layer]), row(post_mix_norm[layer]),
                    w_att_branch[layer].astype(BF16), w_hg_branch[layer].astype(BF16),
                    w_out[layer].astype(BF16), rows=rows, n_lat=n_lat, seq=seq)
        idx = layer // 2
        if layer % 2 == 0:
            xt = _dense_ffn(xt, mod, row(pre_ffn_norm[layer]), row(post_ffn_norm[layer]),
                            ffn_w_gate[idx].astype(BF16), ffn_w_up[idx].astype(BF16),
                            ffn_w_down[idx].astype(BF16), n_lat=n_lat, seq=seq)
        else:
            assert last, "the expert mixer is implemented for latent tokens only"
            xt = _moe_ffn(xt, mod, row(pre_ffn_norm[layer]), row(post_ffn_norm[layer]), moe_router[idx],
                          moe_w_gate[idx].astype(BF16), moe_w_up[idx].astype(BF16),
                          moe_w_down[idx].astype(BF16), seq=seq)
    return xt[:n_lat].reshape(batch, seq, d)
```

```python
import functools

import jax
import jax.numpy as jnp
from jax import lax
from jax.experimental import pallas as pl
from jax.experimental.pallas import tpu as pltpu

F32 = jnp.float32
BF16 = jnp.bfloat16

D_MODEL = 1024
NORM_EPS = 1e-6
LOG2_E = 1.4426950408889634
N_MOD = 6
GRID_W = 64
ROPE_THETA = 10000.0

HEAD_DIM = 128
ATT_HEADS = 8
ATT_KV_HEADS = 2
ATT_GROUP = ATT_HEADS // ATT_KV_HEADS
KV_WIDTH = ATT_KV_HEADS * HEAD_DIM
HGRN_HEADS = 8

N_EXPERTS = 8
TOP_K = 2
MOE_BLOCK = 256

SUBLANES = 8
LANES = 128
VMEM_LIMIT_BYTES = 56 * 1024 * 1024

TOKEN_TILE = 256
HGRN_CHUNK = 64
HGRN_STEP = 256
ATT_Q_TILE = 128
ATT_K_TILE = 512

_C_AQ = 0
_C_AK = _C_AQ + ATT_HEADS * HEAD_DIM
_C_AV = _C_AK + KV_WIDTH
_C_HQ = _C_AV + KV_WIDTH
_C_HFF = _C_HQ + D_MODEL
_C_HFB = _C_HFF + D_MODEL
_C_HI = _C_HFB + D_MODEL
_C_HG = _C_HI + D_MODEL
_C_GA = _C_HG + D_MODEL
_C_GH = _C_GA + D_MODEL
PROJ_WIDTH = _C_GH + D_MODEL


def _params(*sem):
    return pltpu.CompilerParams(dimension_semantics=sem, vmem_limit_bytes=VMEM_LIMIT_BYTES)


def _resident(shape, index_map):
    return pl.BlockSpec(shape, index_map, pipeline_mode=pl.Buffered(1))


def _rms(t, gain):
    return t * lax.rsqrt(jnp.mean(t * t, axis=-1, keepdims=True) + NORM_EPS) * gain


def _silu(t):
    return t * jax.nn.sigmoid(t)


def _dot(a, b):
    return jnp.dot(a, b, preferred_element_type=F32)


def _dot_nt(a, b):
    return lax.dot_general(a, b, (((1,), (1,)), ((), ())), preferred_element_type=F32)


def _dot_tn(a, b):
    return lax.dot_general(a, b, (((0,), (0,)), ((), ())), preferred_element_type=F32)


def _mod_kernel(c_ref, w_ref, b_ref, o_ref):
    a = _silu(c_ref[...])
    o_ref[...] = jnp.dot(a, w_ref[...], preferred_element_type=F32,
                         precision=lax.Precision.HIGHEST) + b_ref[...]


def _mod_vectors(c_rows, w_mod, b_mod):
    depth, d, width = w_mod.shape
    rows = c_rows.shape[0]
    tn = 1536
    return pl.pallas_call(
        _mod_kernel,
        grid=(depth, width // tn),
        in_specs=[
            pl.BlockSpec((rows, d), lambda l, j: (0, 0)),
            pl.BlockSpec((None, d, tn), lambda l, j: (l, 0, j)),
            pl.BlockSpec((None, 1, tn), lambda l, j: (l, 0, j)),
        ],
        out_specs=pl.BlockSpec((None, rows, tn), lambda l, j: (l, 0, j)),
        out_shape=jax.ShapeDtypeStruct((depth, rows, width), F32),
        compiler_params=_params("arbitrary", "arbitrary"),
        name="mod_vectors",
    )(c_rows, w_mod, b_mod.reshape(depth, 1, width))


def _inproj_kernel(x_ref, sh_ref, sc_ref, gain_ref, w_ref, cos_ref, sin_ref, qn_ref, kn_ref, lbl_ref,
                   aq_ref, ak_ref, av_ref, hq_ref, kf_ref, lf_ref, kb_ref, lb_ref, hi_ref, hg_ref,
                   ga_ref, gh_ref, *, layer):
    h = _rms(x_ref[...], gain_ref[...])
    hb = (h * (1.0 + sc_ref[...]) + sh_ref[...]).astype(BF16)
    cos = cos_ref[...]
    sin = sin_ref[...]

    def proj(c0, width):
        return _dot(hb, w_ref[:, c0:c0 + width])

    def head(t, i):
        return t[:, i * HEAD_DIM:(i + 1) * HEAD_DIM]

    def norm_rope(t, gain):
        r = _rms(t, gain)
        return r * cos + pltpu.roll(r, HEAD_DIM // 2, 1) * sin

    qn = qn_ref[...]
    for c in range(2):
        t = proj(_C_AQ + c * 512, 512)
        for i in range(4):
            hh = c * 4 + i
            aq_ref[:, hh * HEAD_DIM:(hh + 1) * HEAD_DIM] = (
                norm_rope(head(t, i), qn) * (HEAD_DIM ** -0.5 * LOG2_E)).astype(BF16)
    t = proj(_C_AK, 2 * KV_WIDTH)
    kn = kn_ref[...]
    for i in range(ATT_KV_HEADS):
        ak_ref[:, i * HEAD_DIM:(i + 1) * HEAD_DIM] = norm_rope(head(t, i), kn).astype(BF16)
    av_ref[...] = t[:, KV_WIDTH:].astype(BF16)

    for c in range(2):
        t = proj(_C_HQ + c * 512, 512)
        for i in range(4):
            hq_ref[c * 4 + i] = _silu(head(t, i)).astype(BF16)

    lg = lbl_ref[...]
    depth = lg.shape[0]
    mx = lg[0:1]
    for j in range(1, depth):
        mx = jnp.maximum(mx, lg[j:j + 1])
    es = [jnp.exp(lg[j:j + 1] - mx) for j in range(depth)]
    tot = es[0]
    for j in range(1, depth):
        tot = tot + es[j]
    low = jnp.zeros_like(tot)
    for j in range(1, layer + 1):
        low = low + es[j] / tot

    for c0, k_ref, l_ref in ((_C_HFF, kf_ref, lf_ref), (_C_HFB, kb_ref, lb_ref)):
        for c in range(2):
            t = proj(c0 + c * 512, 512)
            for i in range(4):
                hh = c * 4 + i
                lo = low[:, hh * HEAD_DIM:(hh + 1) * HEAD_DIM]
                f = lo + (1.0 - lo) * jax.nn.sigmoid(head(t, i))
                k_ref[hh] = (1.0 - f).astype(BF16)
                l_ref[hh] = jnp.log(f) * LOG2_E

    for c in range(2):
        t = proj(_C_HI + c * 512, 512)
        for i in range(4):
            hi_ref[c * 4 + i] = head(t, i).astype(BF16)
    for c in range(2):
        t = proj(_C_HG + c * 512, 512)
        for i in range(4):
            hg_ref[c * 4 + i] = _silu(head(t, i)).astype(BF16)

    for c in range(2):
        ga_ref[:, c * 512:(c + 1) * 512] = jax.nn.sigmoid(proj(_C_GA + c * 512, 512)).astype(BF16)
    for c in range(2):
        gh_ref[:, c * 512:(c + 1) * 512] = jax.nn.sigmoid(proj(_C_GH + c * 512, 512)).astype(BF16)


def _in_projection(x, mod, gain, w_in, cos_tab, sin_tab, q_norm, k_norm, lb_logits, *, layer, n_lat,
                   seq):
    t_rows, d = x.shape
    tm = TOKEN_TILE
    n_lat_tiles = n_lat // tm
    per_seq = seq // tm

    def mod_row(i):
        return jnp.where(i < n_lat_tiles, i // per_seq, mod.shape[0] - 1 - 3)

    def pos_blk(i):
        return jnp.where(i < n_lat_tiles, i % per_seq, per_seq)

    tok = lambda w: pl.BlockSpec((tm, w), lambda i: (i, 0))
    hm = pl.BlockSpec((HGRN_HEADS, tm, HEAD_DIM), lambda i: (0, i, 0))
    hm_shape = lambda dt: jax.ShapeDtypeStruct((HGRN_HEADS, t_rows, HEAD_DIM), dt)
    tm_shape = lambda w: jax.ShapeDtypeStruct((t_rows, w), BF16)
    return pl.pallas_call(
        functools.partial(_inproj_kernel, layer=layer),
        grid=(t_rows // tm,),
        in_specs=[
            tok(d),
            pl.BlockSpec((None, 1, d), lambda i: (mod_row(i), 0, 0)),
            pl.BlockSpec((None, 1, d), lambda i: (mod_row(i), 0, 1)),
            pl.BlockSpec((1, d), lambda i: (0, 0)),
            _resident((d, PROJ_WIDTH), lambda i: (0, 0)),
            pl.BlockSpec((tm, HEAD_DIM), lambda i: (pos_blk(i), 0)),
            pl.BlockSpec((tm, HEAD_DIM), lambda i: (pos_blk(i), 0)),
            pl.BlockSpec((1, HEAD_DIM), lambda i: (0, 0)),
            pl.BlockSpec((1, HEAD_DIM), lambda i: (0, 0)),
            pl.BlockSpec(lb_logits.shape, lambda i: (0, 0)),
        ],
        out_specs=[tok(d), tok(KV_WIDTH), tok(KV_WIDTH), hm, hm, hm, hm, hm, hm, hm, tok(d), tok(d)],
        out_shape=[tm_shape(d), tm_shape(KV_WIDTH), tm_shape(KV_WIDTH), hm_shape(BF16), hm_shape(BF16),
                   hm_shape(F32), hm_shape(BF16), hm_shape(F32), hm_shape(BF16), hm_shape(BF16),
                   tm_shape(d), tm_shape(d)],
        compiler_params=_params("arbitrary"),
        name=f"in_projection_l{layer}",
    )(x, mod, mod, gain, w_in, cos_tab, sin_tab, q_norm, k_norm, lb_logits)


def _hgrn_constants(c):
    t = lax.broadcasted_iota(jnp.int32, (c, c), 0)
    s = lax.broadcasted_iota(jnp.int32, (c, c), 1)
    tri = ((t >= s).astype(BF16), (t <= s).astype(BF16))
    same8 = (t // SUBLANES) == (s // SUBLANES)
    diag = (same8 & (t >= s), same8 & (t <= s))
    levels = []
    blk = 2 * SUBLANES
    while blk <= c:
        half = blk // 2
        same = (t // blk) == (s // blk)
        t_hi = (t % blk) >= half
        s_hi = (s % blk) >= half
        levels.append((blk, (same & t_hi & ~s_hi, same & ~t_hi & s_hi)))
        blk *= 2
    r = lax.broadcasted_iota(jnp.int32, (SUBLANES * LANES, c), 0)
    j = lax.broadcasted_iota(jnp.int32, (SUBLANES * LANES, c), 1)
    sel = ((j % SUBLANES) == (r // LANES)).astype(BF16)
    return tri, diag, levels, sel


def _hgrn_chunk(q, k, v, g, st_t, consts, rev):
    tri, diag, levels, sel = consts
    c = q.shape[0]
    qf = q.astype(F32)
    kf = k.astype(F32)

    g1 = g.astype(BF16)
    r1 = g - g1.astype(F32)
    g2 = r1.astype(BF16)
    g3 = (r1 - g2.astype(F32)).astype(BF16)
    cum = _dot(tri[rev], g1) + _dot(tri[rev], g2) + _dot(tri[rev], g3)
    tail = cum[0:1] if rev else cum[c - 1:c]

    out = _dot_nt((qf * jnp.exp2(cum)).astype(BF16), st_t.astype(BF16))
    k_end = (kf * jnp.exp2(tail - cum)).astype(BF16)
    st_new = st_t * jnp.exp2(tail) + _dot_tn(v, k_end)

    a = jnp.zeros((c, c), F32)
    for blk, masks in levels:
        half = blk // 2
        rows = []
        for a0 in range(0, c, blk):
            r = a0 + half if rev else a0 + half - 1
            rows.append(jnp.broadcast_to(cum[r:r + 1, :], (blk, LANES)))
        mid = rows[0] if len(rows) == 1 else jnp.concatenate(rows, axis=0)
        e = jnp.exp2(-jnp.abs(cum - mid))
        a = a + jnp.where(masks[rev], _dot_nt((qf * e).astype(BF16), (kf * e).astype(BF16)), 0.0)

    nb = c // SUBLANES
    q3 = qf.reshape(nb, SUBLANES, LANES)
    k3 = kf.reshape(nb, SUBLANES, LANES)
    c3 = cum.reshape(nb, SUBLANES, LANES)
    parts = []
    for s in range(SUBLANES):
        e = jnp.exp2(jnp.minimum(c3 - c3[:, s:s + 1, :], 0.0))
        parts.append((q3 * e * k3[:, s:s + 1, :]).reshape(c, LANES).astype(BF16))
    a = a + jnp.where(diag[rev], _dot(jnp.concatenate(parts, axis=1), sel), 0.0)

    out = out + _dot(a.astype(BF16), v)
    return out, st_new


def _hgrn_kernel(qf_ref, vf_ref, kf_ref, lf_ref, qb_ref, vb_ref, kb_ref, lb_ref, of_ref, ob_ref,
                 st_ref, *, chunk):
    @pl.when(pl.program_id(1) == 0)
    def _():
        st_ref[...] = jnp.zeros(st_ref.shape, F32)

    consts = _hgrn_constants(chunk)
    n_chunks = qf_ref.shape[1] // chunk

    def head_body(h, carry):
        st_f = st_ref[0, h]
        st_b = st_ref[1, h]
        for j in range(n_chunks):
            rows = pl.ds(j * chunk, chunk)
            o, st_f = _hgrn_chunk(qf_ref[h, rows, :], kf_ref[h, rows, :], vf_ref[h, rows, :],
                                  lf_ref[h, rows, :], st_f, consts, 0)
            of_ref[h, rows, :] = o
            rows = pl.ds((n_chunks - 1 - j) * chunk, chunk)
            o, st_b = _hgrn_chunk(qb_ref[h, rows, :], kb_ref[h, rows, :], vb_ref[h, rows, :],
                                  lb_ref[h, rows, :], st_b, consts, 1)
            ob_ref[h, rows, :] = o
        st_ref[0, h] = st_f
        st_ref[1, h] = st_b
        return carry

    lax.fori_loop(0, qf_ref.shape[0], head_body, 0)


def _hgrn_scan(hq, hi, kf, lf, kb, lb, *, batch, seq, ctx_len):
    heads, t_rows, dh = hq.shape
    ts = HGRN_STEP
    n_ctx = ctx_len // ts
    n_seq = seq // ts
    ctx0 = batch * n_seq

    def fwd(b, s):
        return jnp.where(s < n_ctx, ctx0 + b * n_ctx + s, b * n_seq + s - n_ctx)

    def bwd(b, s):
        return jnp.where(s < n_ctx, ctx0 + b * n_ctx + n_ctx - 1 - s, b * n_seq + n_seq - 1 - (s - n_ctx))

    spec_f = pl.BlockSpec((heads, ts, dh), lambda b, s: (0, fwd(b, s), 0))
    spec_b = pl.BlockSpec((heads, ts, dh), lambda b, s: (0, bwd(b, s), 0))
    out = jax.ShapeDtypeStruct((heads, t_rows, dh), F32)
    return pl.pallas_call(
        functools.partial(_hgrn_kernel, chunk=HGRN_CHUNK),
        grid=(batch, n_ctx + n_seq),
        in_specs=[spec_f, spec_f, spec_f, spec_f, spec_b, spec_b, spec_b, spec_b],
        out_specs=[spec_f, spec_b],
        out_shape=[out, out],
        scratch_shapes=[pltpu.VMEM((2, heads, dh, dh), F32)],
        compiler_params=_params("arbitrary", "arbitrary"),
        name="hgrn_scan",
    )(hq, hi, kf, lf, hq, hi, kb, lb)


def _attn_kernel(q_ref, kc_ref, vc_ref, kl_ref, vl_ref, o_ref, *, n_lat_q_tiles, k_tile):
    tq = q_ref.shape[0]
    q = jnp.concatenate([q_ref[:, i * HEAD_DIM:(i + 1) * HEAD_DIM] for i in range(ATT_GROUP)], axis=0)
    rows = q.shape[0]

    def update(carry, k, v):
        m, acc = carry
        ones_col = (lax.broadcasted_iota(jnp.int32, v.shape, 1) == 0).astype(BF16)
        s = _dot_nt(q, k)
        m_new = jnp.maximum(m, jnp.max(s, axis=-1, keepdims=True))
        p = jnp.exp2((s - m_new).astype(BF16))
        acc = jnp.exp2(m - m_new) * acc + _dot(p, jnp.concatenate([v, ones_col], axis=1))
        return m_new, acc

    def finish(carry):
        acc = carry[1]
        out = acc[:, :HEAD_DIM] / acc[:, HEAD_DIM:HEAD_DIM + 1]
        for i in range(ATT_GROUP):
            o_ref[:, i * HEAD_DIM:(i + 1) * HEAD_DIM] = out[i * tq:(i + 1) * tq].astype(BF16)

    def ctx_keys():
        init = (jnp.full((rows, 1), -1e30, F32), jnp.zeros((rows, 2 * HEAD_DIM), F32))
        return update(init, kc_ref[...], vc_ref[...])

    is_latent = pl.program_id(2) < n_lat_q_tiles

    @pl.when(is_latent)
    def _():
        carry = ctx_keys()
        for j in range(kl_ref.shape[0] // k_tile):
            carry = update(carry, kl_ref[j * k_tile:(j + 1) * k_tile, :], vl_ref[j * k_tile:(j + 1) * k_tile, :])
        finish(carry)

    @pl.when(jnp.logical_not(is_latent))
    def _():
        finish(ctx_keys())


def _attention(aq, ak, av, *, batch, seq, ctx_len, with_ctx_queries):
    t_rows = aq.shape[0]
    tq = ATT_Q_TILE
    nq_lat = seq // tq
    nq_ctx = ctx_len // tq
    ctx_q0 = batch * nq_lat
    ctx_k0 = batch * seq // ctx_len
    gw = ATT_GROUP * HEAD_DIM

    def q_blk(b, i):
        return jnp.where(i < nq_lat, b * nq_lat + i, ctx_q0 + b * nq_ctx + i - nq_lat)

    q_spec = pl.BlockSpec((tq, gw), lambda b, g, i: (q_blk(b, i), g))
    kc_spec = pl.BlockSpec((ctx_len, HEAD_DIM), lambda b, g, i: (ctx_k0 + b, g))
    kl_spec = pl.BlockSpec((seq, HEAD_DIM), lambda b, g, i: (b, g))
    return pl.pallas_call(
        functools.partial(_attn_kernel, n_lat_q_tiles=nq_lat, k_tile=ATT_K_TILE),
        grid=(batch, ATT_KV_HEADS, nq_lat + (nq_ctx if with_ctx_queries else 0)),
        in_specs=[q_spec, kc_spec, kc_spec, kl_spec, kl_spec],
        out_specs=q_spec,
        out_shape=jax.ShapeDtypeStruct((t_rows if with_ctx_queries else batch * seq, ATT_HEADS * HEAD_DIM),
                                       BF16),
        compiler_params=_params("arbitrary", "arbitrary", "arbitrary"),
        name="gqa_attention",
    )(aq, ak, av, ak, av)


def _merge_kernel(x_ref, att_ref, of_ref, ob_ref, hg_ref, ga_ref, gh_ref, g1_ref, hgn_ref, post_ref,
                  wa_ref, wh_ref, wo_ref, o_ref, hg_scr):
    hgn = hgn_ref[...]
    for i in range(HGRN_HEADS):
        o = _rms(of_ref[i] + ob_ref[i], hgn)
        hg_scr[:, i * HEAD_DIM:(i + 1) * HEAD_DIM] = (o * hg_ref[i].astype(F32)).astype(BF16)
    y = (ga_ref[...].astype(F32) * _dot(att_ref[...], wa_ref[...])
         + gh_ref[...].astype(F32) * _dot(hg_scr[...], wh_ref[...]))
    z = _rms(_dot(y.astype(BF16), wo_ref[...]), post_ref[...])
    o_ref[...] = x_ref[...] + g1_ref[...] * z


def _merge(x, att, o_f, o_b, hg, ga, gh, mod, hg_norm, post_norm, w_att, w_hg, w_out, *, rows, n_lat,
           seq):
    d = x.shape[1]
    tm = TOKEN_TILE
    n_lat_tiles = n_lat // tm
    per_seq = seq // tm

    def mod_row(i):
        return jnp.where(i < n_lat_tiles, i // per_seq, mod.shape[0] - 1 - 3)

    tok = pl.BlockSpec((tm, d), lambda i: (i, 0))
    hm = pl.BlockSpec((HGRN_HEADS, tm, HEAD_DIM), lambda i: (0, i, 0))
    vec = lambda w: pl.BlockSpec((1, w), lambda i: (0, 0))
    wgt = _resident((d, d), lambda i: (0, 0))
    return pl.pallas_call(
        _merge_kernel,
        grid=(rows // tm,),
        in_specs=[tok, tok, hm, hm, hm, tok, tok,
                  pl.BlockSpec((None, 1, d), lambda i: (mod_row(i), 0, 2)),
                  vec(HEAD_DIM), vec(d), wgt, wgt, wgt],
        out_specs=tok,
        out_shape=jax.ShapeDtypeStruct((rows, d), F32),
        scratch_shapes=[pltpu.VMEM((tm, d), BF16)],
        compiler_params=_params("arbitrary"),
        name="branch_merge",
    )(x, att, o_f, o_b, hg, ga, gh, mod, hg_norm, post_norm, w_att, w_hg, w_out)


def _ffn_kernel(x_ref, sh_ref, sc_ref, g2_ref, pre_ref, post_ref, wg_ref, wu_ref, wd_ref, o_ref, a_scr,
                *, ff_tile):
    x = x_ref[...]
    f = (_rms(x, pre_ref[...]) * (1.0 + sc_ref[...]) + sh_ref[...]).astype(BF16)
    for c in range(0, wg_ref.shape[1], ff_tile):
        g = _dot(f, wg_ref[:, c:c + ff_tile])
        u = _dot(f, wu_ref[:, c:c + ff_tile])
        a_scr[:, c:c + ff_tile] = (_silu(g) * u).astype(BF16)
    y = _rms(_dot(a_scr[...], wd_ref[...]), post_ref[...])
    o_ref[...] = x + g2_ref[...] * y


def _dense_ffn(x, mod, pre_norm, post_norm, w_gate, w_up, w_down, *, n_lat, seq):
    rows, d = x.shape
    d_ff = w_gate.shape[1]
    tm = TOKEN_TILE
    n_lat_tiles = n_lat // tm
    per_seq = seq // tm

    def mod_row(i):
        return jnp.where(i < n_lat_tiles, i // per_seq, mod.shape[0] - 1 - 3)

    tok = pl.BlockSpec((tm, d), lambda i: (i, 0))
    vec = pl.BlockSpec((1, d), lambda i: (0, 0))
    mod_spec = lambda col: pl.BlockSpec((None, 1, d), lambda i: (mod_row(i), 0, col))
    return pl.pallas_call(
        functools.partial(_ffn_kernel, ff_tile=256),
        grid=(rows // tm,),
        in_specs=[tok, mod_spec(3), mod_spec(4), mod_spec(5), vec, vec,
                  _resident((d, d_ff), lambda i: (0, 0)), _resident((d, d_ff), lambda i: (0, 0)),
                  _resident((d_ff, d), lambda i: (0, 0))],
        out_specs=tok,
        out_shape=jax.ShapeDtypeStruct((rows, d), F32),
        scratch_shapes=[pltpu.VMEM((tm, d_ff), BF16)],
        compiler_params=_params("arbitrary"),
        name="dense_swiglu",
    )(x, mod, mod, mod, pre_norm, post_norm, w_gate, w_up, w_down)


def _router_kernel(x_ref, sh_ref, sc_ref, pre_ref, wr_ref, f_ref, route_ref):
    f = _rms(x_ref[...], pre_ref[...]) * (1.0 + sc_ref[...]) + sh_ref[...]
    f_ref[...] = f
    logits = jnp.dot(f, wr_ref[...], preferred_element_type=F32, precision=lax.Precision.HIGHEST)
    lane = lax.broadcasted_iota(jnp.int32, logits.shape, 1)
    neg = jnp.float32(-jnp.inf)
    logits = jnp.where(lane < N_EXPERTS, logits, neg)
    m1 = jnp.max(logits, axis=-1, keepdims=True)
    i1 = jnp.min(jnp.where(logits == m1, lane, LANES), axis=-1, keepdims=True)
    rest = jnp.where(lane == i1, neg, logits)
    m2 = jnp.max(rest, axis=-1, keepdims=True)
    i2 = jnp.min(jnp.where(rest == m2, lane, LANES), axis=-1, keepdims=True)
    w1 = 1.0 / (1.0 + jnp.exp(m2 - m1))
    w2 = 1.0 - w1
    route = jnp.where(lane == 0, i1.astype(F32),
                      jnp.where(lane == 1, i2.astype(F32),
                                jnp.where(lane == 2, w1, jnp.where(lane == 3, w2, 0.0))))
    route_ref[...] = route


def _router(x, mod, pre_norm, w_router_padded, *, seq):
    rows, d = x.shape
    tm = TOKEN_TILE
    per_seq = seq // tm
    tok = pl.BlockSpec((tm, d), lambda i: (i, 0))
    mod_spec = lambda col: pl.BlockSpec((None, 1, d), lambda i: (i // per_seq, 0, col))
    return pl.pallas_call(
        _router_kernel,
        grid=(rows // tm,),
        in_specs=[tok, mod_spec(3), mod_spec(4), pl.BlockSpec((1, d), lambda i: (0, 0)),
                  pl.BlockSpec((d, LANES), lambda i: (0, 0))],
        out_specs=[tok, pl.BlockSpec((tm, LANES), lambda i: (i, 0))],
        out_shape=[jax.ShapeDtypeStruct((rows, d), F32), jax.ShapeDtypeStruct((rows, LANES), F32)],
        compiler_params=_params("arbitrary"),
        name="moe_router",
    )(x, mod, mod, pre_norm, w_router_padded)


def _expert_kernel(be_ref, tok_ref, dst_ref, f_hbm, wg_ref, wu_ref, wd_ref, ysel_hbm,
                   xbuf, ybuf, a_scr, gsem, ssem, *, ff_tile, n_blocks, spare_row0):
    del be_ref
    s = pl.program_id(0)
    rows = xbuf.shape[1]

    def gather_row(r, slot):
        pltpu.make_async_copy(f_hbm.at[pl.ds(tok_ref[r], 1)], xbuf.at[slot, pl.ds(r, 1)],
                              gsem.at[slot]).start()

    def scatter_row(r, slot, dst_row):
        pltpu.make_async_copy(ybuf.at[slot, pl.ds(r, 1)], ysel_hbm.at[pl.ds(dst_row, 1)],
                              ssem.at[slot]).start()

    def wait_gather(slot):
        pltpu.make_async_copy(f_hbm.at[pl.ds(0, rows)], xbuf.at[slot], gsem.at[slot]).wait()

    def wait_scatter(slot):
        pltpu.make_async_copy(ybuf.at[slot], ysel_hbm.at[pl.ds(0, rows)], ssem.at[slot]).wait()

    @pl.when(s == 0)
    def _():
        ybuf[...] = jnp.zeros(ybuf.shape, F32)

        def body(r, carry):
            gather_row(r, 0)
            scatter_row(r, 0, spare_row0 + r)
            return carry
        lax.fori_loop(0, rows, body, 0)

    @pl.when((s >= 1) & (s <= n_blocks))
    def _():
        slot = (s - 1) % 2
        wait_gather(slot)
        wait_scatter(slot)
        xb = xbuf[slot].astype(BF16)
        chunks = list(range(0, wg_ref.shape[1], ff_tile))
        per = -(-rows // (len(chunks) + 1))

        def move_rows(j):
            for r in range(j * per, min((j + 1) * per, rows)):
                gather_row(r, 1 - slot)
                scatter_row(r, 1 - slot, dst_ref[r])

        for j, c in enumerate(chunks):
            move_rows(j)
            g = _dot(xb, wg_ref[:, c:c + ff_tile])
            u = _dot(xb, wu_ref[:, c:c + ff_tile])
            a_scr[:, c:c + ff_tile] = (_silu(g) * u).astype(BF16)
        move_rows(len(chunks))
        ybuf[slot] = _dot(a_scr[...], wd_ref[...])

    @pl.when(s == n_blocks + 1)
    def _():
        slot = (n_blocks - 1) % 2

        def body(r, carry):
            scatter_row(r, slot, dst_ref[r])
            return carry
        lax.fori_loop(0, rows, body, 0)
        wait_scatter(1 - slot)
        wait_scatter(slot)
        wait_gather(1 - slot)


def _expert_ffn(f, row_tok, row_dst_ext, block_expert, w_gate, w_up, w_down, *, n_out_rows, spare_row0):
    n_rows = row_tok.shape[0]
    d = f.shape[1]
    d_ff = w_gate.shape[2]
    n_blocks = n_rows // MOE_BLOCK
    idx = lambda fn: pl.BlockSpec((MOE_BLOCK,), fn, memory_space=pltpu.SMEM)
    wgt = lambda shape: pl.BlockSpec((None,) + shape,
                                     lambda s, be: (be[jnp.clip(s - 1, 0, n_blocks - 1)], 0, 0),
                                     pipeline_mode=pl.Buffered(1))
    grid_spec = pltpu.PrefetchScalarGridSpec(
        num_scalar_prefetch=1,
        grid=(n_blocks + 2,),
        in_specs=[idx(lambda s, be: (jnp.minimum(s, n_blocks - 1),)),
                  idx(lambda s, be: (jnp.maximum(s - 1, 0),)),
                  pl.BlockSpec(memory_space=pl.ANY),
                  wgt((d, d_ff)), wgt((d, d_ff)), wgt((d_ff, d))],
        out_specs=pl.BlockSpec(memory_space=pl.ANY),
        scratch_shapes=[pltpu.VMEM((2, MOE_BLOCK, d), F32), pltpu.VMEM((2, MOE_BLOCK, d), F32),
                        pltpu.VMEM((MOE_BLOCK, d_ff), BF16),
                        pltpu.SemaphoreType.DMA((2,)), pltpu.SemaphoreType.DMA((2,))],
    )
    return pl.pallas_call(
        functools.partial(_expert_kernel, ff_tile=512, n_blocks=n_blocks, spare_row0=spare_row0),
        grid_spec=grid_spec,
        out_shape=jax.ShapeDtypeStruct((n_out_rows, d), F32),
        compiler_params=_params("arbitrary"),
        name="moe_expert_swiglu",
    )(block_expert, row_tok, row_dst_ext, f, w_gate, w_up, w_down)


def _combine_kernel(y0_ref, y1_ref, x_ref, route_ref, g2_ref, post_ref, o_ref):
    route = route_ref[...]
    y = route[:, 2:3] * y0_ref[...] + route[:, 3:4] * y1_ref[...]
    o_ref[...] = x_ref[...] + g2_ref[...] * _rms(y, post_ref[...])


def _combine(y_sel, x, route, mod, post_norm, *, seq, choice_stride):
    rows, d = x.shape
    tm = TOKEN_TILE
    per_seq = seq // tm
    tok = pl.BlockSpec((tm, d), lambda i: (i, 0))
    return pl.pallas_call(
        _combine_kernel,
        grid=(rows // tm,),
        in_specs=[tok, pl.BlockSpec((tm, d), lambda i: (choice_stride // tm + i, 0)), tok,
                  pl.BlockSpec((tm, LANES), lambda i: (i, 0)),
                  pl.BlockSpec((None, 1, d), lambda i: (i // per_seq, 0, 5)),
                  pl.BlockSpec((1, d), lambda i: (0, 0))],
        out_specs=tok,
        out_shape=jax.ShapeDtypeStruct((rows, d), F32),
        compiler_params=_params("arbitrary"),
        name="moe_combine",
    )(y_sel, y_sel, x, route, mod, post_norm)


def _moe_ffn(x, mod, pre_norm, post_norm, w_router, w_gate, w_up, w_down, *, seq):
    n_tok, d = x.shape
    n_assign = n_tok * TOP_K
    n_blocks = -(-n_assign // MOE_BLOCK) + N_EXPERTS
    n_rows = n_blocks * MOE_BLOCK
    n_pad = n_rows - n_assign
    assert TOP_K == 2 and n_pad % (TOP_K * TOKEN_TILE) == 0
    wr = jnp.pad(w_router, ((0, 0), (0, LANES - N_EXPERTS)))
    f, route = _router(x, mod, pre_norm, wr, seq=seq)

    flat_e = route[:, :TOP_K].astype(jnp.int32).reshape(-1)
    onehot = (flat_e[:, None] == jnp.arange(N_EXPERTS, dtype=jnp.int32)[None, :]).astype(jnp.int32)
    csum = jnp.cumsum(onehot, axis=0)
    counts = csum[-1]
    rank = jnp.sum((csum - onehot) * onehot, axis=1)
    padded = (counts + MOE_BLOCK - 1) // MOE_BLOCK * MOE_BLOCK
    pend = jnp.cumsum(padded)
    dest = ((pend - padded)[flat_e] + rank).astype(jnp.int32)
    block_expert = jnp.minimum(
        jnp.searchsorted(pend, jnp.arange(n_blocks, dtype=jnp.int32) * MOE_BLOCK, side='right'),
        N_EXPERTS - 1).astype(jnp.int32)
    choice_stride = n_tok + n_pad // TOP_K
    row_assign = jnp.full((n_rows,), -1, jnp.int32).at[dest].set(jnp.arange(n_assign, dtype=jnp.int32))
    is_pad = row_assign < 0
    pad_id = jnp.cumsum(is_pad.astype(jnp.int32)) - 1
    row_tok = jnp.where(is_pad, 0, row_assign // TOP_K)
    row_dst = jnp.where(is_pad, (pad_id % TOP_K) * choice_stride + n_tok + pad_id // TOP_K,
                        (row_assign % TOP_K) * choice_stride + row_assign // TOP_K)
    row_dst_ext = jnp.concatenate([n_rows + jnp.arange(MOE_BLOCK, dtype=jnp.int32), row_dst])

    y_sel = _expert_ffn(f, row_tok, row_dst_ext, block_expert, w_gate, w_up, w_down,
                        n_out_rows=n_rows + 2 * MOE_BLOCK, spare_row0=n_rows + MOE_BLOCK)
    return _combine(y_sel, x, route, mod, post_norm, seq=seq, choice_stride=choice_stride)


def _rope_tables(seq, pad_rows):
    pos = jnp.arange(seq)
    nfreq = HEAD_DIM // 4
    inv_freq = ROPE_THETA ** (-jnp.arange(nfreq, dtype=F32) / nfreq)
    ang = jnp.concatenate([(pos // GRID_W).astype(F32)[:, None] * inv_freq,
                           (pos % GRID_W).astype(F32)[:, None] * inv_freq], axis=-1)
    ang = jnp.concatenate([ang, ang], axis=-1)
    sign = jnp.where(jnp.arange(HEAD_DIM) < HEAD_DIM // 2, -1.0, 1.0).astype(F32)
    cos = jnp.concatenate([jnp.cos(ang), jnp.ones((pad_rows, HEAD_DIM), F32)], axis=0)
    sin = jnp.concatenate([jnp.sin(ang) * sign, jnp.zeros((pad_rows, HEAD_DIM), F32)], axis=0)
    return cos, sin


def kernel(x, c, ctx, c_ctx, w_mod, b_mod, pre_mix_norm, post_mix_norm, pre_ffn_norm, post_ffn_norm, w_in, q_norm, k_norm, hg_norm, hg_lb_logits, w_att_branch, w_hg_branch, w_out, ffn_w_gate, ffn_w_up, ffn_w_down, moe_router, moe_w_gate, moe_w_up, moe_w_down):
    batch, seq, d = x.shape
    ctx_len = ctx.shape[1]
    depth = w_mod.shape[0]
    n_lat = batch * seq
    assert d == D_MODEL and w_in.shape[2] == PROJ_WIDTH
    assert seq % HGRN_STEP == 0 and ctx_len % HGRN_STEP == 0 and n_lat % TOKEN_TILE == 0
    assert seq % ctx_len == 0 and ctx_len % TOKEN_TILE == 0 and seq % ATT_K_TILE == 0

    c_rows = jnp.concatenate([c, c_ctx[None, :], jnp.zeros((3, d), F32)], axis=0)
    mod_all = _mod_vectors(c_rows, w_mod, b_mod)
    cos_tab, sin_tab = _rope_tables(seq, TOKEN_TILE)
    row = lambda v: v.reshape(1, -1)

    xt = jnp.concatenate([x.reshape(n_lat, d), ctx.reshape(batch * ctx_len, d)], axis=0)
    for layer in range(depth):
        last = layer == depth - 1
        mod = mod_all[layer].reshape(batch + 4, 1, N_MOD * d)
        (aq, ak, av, hq, kf, lf, kb, lb, hi, hg, ga, gh) = _in_projection(
            xt, mod, row(pre_mix_norm[layer]), w_in[layer].astype(BF16), cos_tab, sin_tab,
            row(q_norm[layer]), row(k_norm[layer]), hg_lb_logits, layer=layer, n_lat=n_lat, seq=seq)
        o_f, o_b = _hgrn_scan(hq, hi, kf, lf, kb, lb, batch=batch, seq=seq, ctx_len=ctx_len)
        att = _attention(aq, ak, av, batch=batch, seq=seq, ctx_len=ctx_len, with_ctx_queries=not last)
        rows = n_lat if last else xt.shape[0]
        xt = _merge(xt, att, o_f, o_b, hg, ga, gh, mod, row(hg_norm[layer]), row(post_mix_norm[layer]),
                    w_att_branch[layer].astype(BF16), w_hg_branch[layer].astype(BF16),
                    w_out[layer].astype(BF16), rows=rows, n_lat=n_lat, seq=seq)
        idx = layer // 2
        if layer % 2 == 0:
            xt = _dense_ffn(xt, mod, row(pre_ffn_norm[layer]), row(post_ffn_norm[layer]),
                            ffn_w_gate[idx].astype(BF16), ffn_w_up[idx].astype(BF16),
                            ffn_w_down[idx].astype(BF16), n_lat=n_lat, seq=seq)
        else:
            assert last, "the expert mixer is implemented for latent tokens only"
            xt = _moe_ffn(xt, mod, row(pre_ffn_norm[layer]), row(post_ffn_norm[layer]), moe_router[idx],
                          moe_w_gate[idx].astype(BF16), moe_w_up[idx].astype(BF16),
                          moe_w_down[idx].astype(BF16), seq=seq)
    return xt[:n_lat].reshape(batch, seq, d)
```

```python
import functools

import jax
import jax.numpy as jnp
from jax import lax
from jax.experimental import pallas as pl
from jax.experimental.pallas import tpu as pltpu

F32 = jnp.float32
BF16 = jnp.bfloat16

D_MODEL = 1024
NORM_EPS = 1e-6
LOG2_E = 1.4426950408889634
N_MOD = 6
GRID_W = 64
ROPE_THETA = 10000.0

HEAD_DIM = 128
ATT_HEADS = 8
ATT_KV_HEADS = 2
ATT_GROUP = ATT_HEADS // ATT_KV_HEADS
KV_WIDTH = ATT_KV_HEADS * HEAD_DIM
HGRN_HEADS = 8

N_EXPERTS = 8
TOP_K = 2
MOE_BLOCK = 256

SUBLANES = 8
LANES = 128
VMEM_LIMIT_BYTES = 56 * 1024 * 1024

TOKEN_TILE = 256
HGRN_CHUNK = 128
HGRN_STEP = 256
ATT_Q_TILE = 128
ATT_K_TILE = 512

_C_AQ = 0
_C_AK = _C_AQ + ATT_HEADS * HEAD_DIM
_C_AV = _C_AK + KV_WIDTH
_C_HQ = _C_AV + KV_WIDTH
_C_HFF = _C_HQ + D_MODEL
_C_HFB = _C_HFF + D_MODEL
_C_HI = _C_HFB + D_MODEL
_C_HG = _C_HI + D_MODEL
_C_GA = _C_HG + D_MODEL
_C_GH = _C_GA + D_MODEL
PROJ_WIDTH = _C_GH + D_MODEL


def _params(*sem):
    return pltpu.CompilerParams(dimension_semantics=sem, vmem_limit_bytes=VMEM_LIMIT_BYTES)


def _resident(shape, index_map):
    return pl.BlockSpec(shape, index_map, pipeline_mode=pl.Buffered(1))


def _rms(t, gain):
    return t * lax.rsqrt(jnp.mean(t * t, axis=-1, keepdims=True) + NORM_EPS) * gain


def _silu(t):
    return t * jax.nn.sigmoid(t)


def _dot(a, b):
    return jnp.dot(a, b, preferred_element_type=F32)


def _dot_nt(a, b):
    return lax.dot_general(a, b, (((1,), (1,)), ((), ())), preferred_element_type=F32)


def _dot_tn(a, b):
    return lax.dot_general(a, b, (((0,), (0,)), ((), ())), preferred_element_type=F32)


def _mod_kernel(c_ref, w_ref, b_ref, o_ref):
    a = _silu(c_ref[...])
    o_ref[...] = jnp.dot(a, w_ref[...], preferred_element_type=F32,
                         precision=lax.Precision.HIGHEST) + b_ref[...]


def _mod_vectors(c_rows, w_mod, b_mod):
    depth, d, width = w_mod.shape
    rows = c_rows.shape[0]
    tn = 1536
    return pl.pallas_call(
        _mod_kernel,
        grid=(depth, width // tn),
        in_specs=[
            pl.BlockSpec((rows, d), lambda l, j: (0, 0)),
            pl.BlockSpec((None, d, tn), lambda l, j: (l, 0, j)),
            pl.BlockSpec((None, 1, tn), lambda l, j: (l, 0, j)),
        ],
        out_specs=pl.BlockSpec((None, rows, tn), lambda l, j: (l, 0, j)),
        out_shape=jax.ShapeDtypeStruct((depth, rows, width), F32),
        compiler_params=_params("arbitrary", "arbitrary"),
        name="mod_vectors",
    )(c_rows, w_mod, b_mod.reshape(depth, 1, width))


def _inproj_kernel(x_ref, sh_ref, sc_ref, gain_ref, w_ref, cos_ref, sin_ref, qn_ref, kn_ref, lbl_ref,
                   aq_ref, ak_ref, av_ref, hq_ref, kf_ref, lf_ref, kb_ref, lb_ref, hi_ref, hg_ref,
                   ga_ref, gh_ref, *, layer):
    h = _rms(x_ref[...], gain_ref[...])
    hb = (h * (1.0 + sc_ref[...]) + sh_ref[...]).astype(BF16)
    cos = cos_ref[...]
    sin = sin_ref[...]

    def proj(c0, width):
        return _dot(hb, w_ref[:, c0:c0 + width])

    def head(t, i):
        return t[:, i * HEAD_DIM:(i + 1) * HEAD_DIM]

    def norm_rope(t, gain):
        r = _rms(t, gain)
        return r * cos + pltpu.roll(r, HEAD_DIM // 2, 1) * sin

    qn = qn_ref[...]
    for c in range(2):
        t = proj(_C_AQ + c * 512, 512)
        for i in range(4):
            hh = c * 4 + i
            aq_ref[:, hh * HEAD_DIM:(hh + 1) * HEAD_DIM] = (
                norm_rope(head(t, i), qn) * (HEAD_DIM ** -0.5 * LOG2_E)).astype(BF16)
    t = proj(_C_AK, 2 * KV_WIDTH)
    kn = kn_ref[...]
    for i in range(ATT_KV_HEADS):
        ak_ref[:, i * HEAD_DIM:(i + 1) * HEAD_DIM] = norm_rope(head(t, i), kn).astype(BF16)
    av_ref[...] = t[:, KV_WIDTH:].astype(BF16)

    for c in range(2):
        t = proj(_C_HQ + c * 512, 512)
        for i in range(4):
            hq_ref[c * 4 + i] = _silu(head(t, i)).astype(BF16)

    lg = lbl_ref[...]
    depth = lg.shape[0]
    mx = lg[0:1]
    for j in range(1, depth):
        mx = jnp.maximum(mx, lg[j:j + 1])
    es = [jnp.exp(lg[j:j + 1] - mx) for j in range(depth)]
    tot = es[0]
    for j in range(1, depth):
        tot = tot + es[j]
    low = jnp.zeros_like(tot)
    for j in range(1, layer + 1):
        low = low + es[j] / tot

    for c0, k_ref, l_ref in ((_C_HFF, kf_ref, lf_ref), (_C_HFB, kb_ref, lb_ref)):
        for c in range(2):
            t = proj(c0 + c * 512, 512)
            for i in range(4):
                hh = c * 4 + i
                lo = low[:, hh * HEAD_DIM:(hh + 1) * HEAD_DIM]
                f = lo + (1.0 - lo) * jax.nn.sigmoid(head(t, i))
                k_ref[hh] = (1.0 - f).astype(BF16)
                g = jnp.log(f) * LOG2_E
                g1 = g.astype(BF16)
                r1 = g - g1.astype(F32)
                g2 = r1.astype(BF16)
                g3 = (r1 - g2.astype(F32)).astype(BF16)
                l_ref[hh] = jnp.concatenate([g1, g2, g3], axis=1)

    for c in range(2):
        t = proj(_C_HI + c * 512, 512)
        for i in range(4):
            hi_ref[c * 4 + i] = head(t, i).astype(BF16)
    for c in range(2):
        t = proj(_C_HG + c * 512, 512)
        for i in range(4):
            hg_ref[c * 4 + i] = _silu(head(t, i)).astype(BF16)

    for c in range(2):
        ga_ref[:, c * 512:(c + 1) * 512] = jax.nn.sigmoid(proj(_C_GA + c * 512, 512)).astype(BF16)
    for c in range(2):
        gh_ref[:, c * 512:(c + 1) * 512] = jax.nn.sigmoid(proj(_C_GH + c * 512, 512)).astype(BF16)


def _in_projection(x, mod, gain, w_in, cos_tab, sin_tab, q_norm, k_norm, lb_logits, *, layer, n_lat,
                   seq):
    t_rows, d = x.shape
    tm = TOKEN_TILE
    n_lat_tiles = n_lat // tm
    per_seq = seq // tm

    def mod_row(i):
        return jnp.where(i < n_lat_tiles, i // per_seq, mod.shape[0] - 1 - 3)

    def pos_blk(i):
        return jnp.where(i < n_lat_tiles, i % per_seq, per_seq)

    tok = lambda w: pl.BlockSpec((tm, w), lambda i: (i, 0))
    hm = pl.BlockSpec((HGRN_HEADS, tm, HEAD_DIM), lambda i: (0, i, 0))
    hm3 = pl.BlockSpec((HGRN_HEADS, tm, 3 * HEAD_DIM), lambda i: (0, i, 0))
    hm_shape = lambda terms: jax.ShapeDtypeStruct((HGRN_HEADS, t_rows, terms * HEAD_DIM), BF16)
    tm_shape = lambda w: jax.ShapeDtypeStruct((t_rows, w), BF16)
    return pl.pallas_call(
        functools.partial(_inproj_kernel, layer=layer),
        grid=(t_rows // tm,),
        in_specs=[
            tok(d),
            pl.BlockSpec((None, 1, d), lambda i: (mod_row(i), 0, 0)),
            pl.BlockSpec((None, 1, d), lambda i: (mod_row(i), 0, 1)),
            pl.BlockSpec((1, d), lambda i: (0, 0)),
            _resident((d, PROJ_WIDTH), lambda i: (0, 0)),
            pl.BlockSpec((tm, HEAD_DIM), lambda i: (pos_blk(i), 0)),
            pl.BlockSpec((tm, HEAD_DIM), lambda i: (pos_blk(i), 0)),
            pl.BlockSpec((1, HEAD_DIM), lambda i: (0, 0)),
            pl.BlockSpec((1, HEAD_DIM), lambda i: (0, 0)),
            pl.BlockSpec(lb_logits.shape, lambda i: (0, 0)),
        ],
        out_specs=[tok(d), tok(KV_WIDTH), tok(KV_WIDTH), hm, hm, hm3, hm, hm3, hm, hm, tok(d), tok(d)],
        out_shape=[tm_shape(d), tm_shape(KV_WIDTH), tm_shape(KV_WIDTH), hm_shape(1), hm_shape(1),
                   hm_shape(3), hm_shape(1), hm_shape(3), hm_shape(1), hm_shape(1),
                   tm_shape(d), tm_shape(d)],
        compiler_params=_params("arbitrary"),
        name=f"in_projection_l{layer}",
    )(x, mod, mod, gain, w_in, cos_tab, sin_tab, q_norm, k_norm, lb_logits)


def _hgrn_constants(n, c):
    t = lax.broadcasted_iota(jnp.int32, (n, n), 0)
    s = lax.broadcasted_iota(jnp.int32, (n, n), 1)
    tri = ((t >= s).astype(BF16), (t <= s).astype(BF16))
    t = lax.broadcasted_iota(jnp.int32, (c, c), 0)
    s = lax.broadcasted_iota(jnp.int32, (c, c), 1)
    levels = []
    blk = 2
    while blk <= c:
        half = blk // 2
        same = (t // blk) == (s // blk)
        t_hi = (t % blk) >= half
        s_hi = (s % blk) >= half
        levels.append((blk, (same & t_hi & ~s_hi, same & ~t_hi & s_hi)))
        blk *= 2
    return tri, t == s, levels


def _boundary_decay(cum, blk, rev):
    n = cum.shape[0]
    half = blk // 2
    off = half if rev else half - 1
    if blk >= 2 * SUBLANES:
        rows = [jnp.broadcast_to(cum[a0 + off:a0 + off + 1, :], (blk, LANES)) for a0 in range(0, n, blk)]
        mid = rows[0] if len(rows) == 1 else jnp.concatenate(rows, axis=0)
    else:
        c3 = cum.reshape(n // SUBLANES, SUBLANES, LANES)
        sub = lax.broadcasted_iota(jnp.int32, c3.shape, 1)
        mid = jnp.broadcast_to(c3[:, off:off + 1, :], c3.shape)
        for a0 in range(blk, SUBLANES, blk):
            mid = jnp.where(sub >= a0, jnp.broadcast_to(c3[:, a0 + off:a0 + off + 1, :], c3.shape), mid)
        mid = mid.reshape(n, LANES)
    x = lax.bitcast_convert_type(cum - mid, jnp.uint32) | jnp.uint32(0x80000000)
    return jnp.exp2(lax.bitcast_convert_type(x, F32))


def _hgrn_block(q, k, v, g3, st_t, consts, chunk, rev):
    tri, eye, levels = consts
    n = q.shape[0]
    c3x = _dot(tri[rev], g3)
    cum = c3x[:, :LANES] + c3x[:, LANES:2 * LANES] + c3x[:, 2 * LANES:]
    tail = cum[0:1] if rev else cum[n - 1:n]

    qf = q.astype(F32)
    kf = k.astype(F32)
    scaled = lambda t, e: (t * e).astype(BF16)
    out = _dot_nt(scaled(qf, jnp.exp2(cum)), st_t.astype(BF16))
    k_end = scaled(kf, jnp.exp2(tail - cum))
    st_new = st_t * jnp.exp2(tail) + _dot_tn(v, k_end)

    blk = n
    far = []
    while blk > chunk:
        half = blk // 2
        e = _boundary_decay(cum, blk, rev)
        qe = scaled(qf, e)
        ke = scaled(kf, e)
        for a0 in range(0, n, blk):
            qr = a0 if rev else a0 + half
            kr = a0 + half if rev else a0
            sc = _dot_nt(qe[qr:qr + half], ke[kr:kr + half])
            far.append((qr, half, _dot(sc.astype(BF16), v[kr:kr + half])))
        blk //= 2

    sides = [(q, k)]
    for blk, _ in levels:
        if blk == 2:
            sides.append((scaled(qf, jnp.exp2(g3[:, :LANES].astype(F32))), k))
        else:
            e = _boundary_decay(cum, blk, rev)
            sides.append((scaled(qf, e), scaled(kf, e)))
    masks = [eye] + [m[rev] for _, m in levels]
    outs = []
    for c0 in range(0, n, chunk):
        a = jnp.zeros((chunk, chunk), F32)
        rows = slice(c0, c0 + chunk)
        zero = jnp.zeros((chunk, LANES), BF16)
        for i in range(0, len(sides) - 1, 2):
            (qa, ka), (qb, kb) = sides[i], sides[i + 1]
            keys = jnp.concatenate([jnp.concatenate([ka[rows], zero], axis=1),
                                    jnp.concatenate([zero, kb[rows]], axis=1)], axis=0)
            sc = _dot_nt(jnp.concatenate([qa[rows], qb[rows]], axis=1), keys)
            a = jnp.where(masks[i], sc[:, :chunk], a)
            a = jnp.where(masks[i + 1], sc[:, chunk:], a)
        if len(sides) % 2:
            a = jnp.where(masks[-1], _dot_nt(sides[-1][0][rows], sides[-1][1][rows]), a)
        o = out[c0:c0 + chunk] + _dot(a.astype(BF16), v[c0:c0 + chunk])
        for r0, rows, val in far:
            if r0 <= c0 < r0 + rows:
                o = o + val[c0 - r0:c0 - r0 + chunk]
        outs.append(o)
    return (outs[0] if len(outs) == 1 else jnp.concatenate(outs, axis=0)), st_new


def _hgrn_kernel(qf_ref, vf_ref, kf_ref, lf_ref, qb_ref, vb_ref, kb_ref, lb_ref, of_ref, ob_ref,
                 st_ref, *, chunk):
    @pl.when(pl.program_id(1) == 0)
    def _():
        st_ref[...] = jnp.zeros(st_ref.shape, F32)

    consts = _hgrn_constants(qf_ref.shape[1], chunk)

    def head_body(h, carry):
        o, st = _hgrn_block(qf_ref[h], kf_ref[h], vf_ref[h], lf_ref[h], st_ref[0, h], consts, chunk, 0)
        of_ref[h] = o
        st_ref[0, h] = st
        o, st = _hgrn_block(qb_ref[h], kb_ref[h], vb_ref[h], lb_ref[h], st_ref[1, h], consts, chunk, 1)
        ob_ref[h] = o
        st_ref[1, h] = st
        return carry

    lax.fori_loop(0, qf_ref.shape[0], head_body, 0, unroll=2)


def _hgrn_scan(hq, hi, kf, lf, kb, lb, *, batch, seq, ctx_len):
    heads, t_rows, dh = hq.shape
    ts = HGRN_STEP
    n_ctx = ctx_len // ts
    n_seq = seq // ts
    ctx0 = batch * n_seq

    def fwd(b, s):
        return jnp.where(s < n_ctx, ctx0 + b * n_ctx + s, b * n_seq + s - n_ctx)

    def bwd(b, s):
        return jnp.where(s < n_ctx, ctx0 + b * n_ctx + n_ctx - 1 - s, b * n_seq + n_seq - 1 - (s - n_ctx))

    spec_f = pl.BlockSpec((heads, ts, dh), lambda b, s: (0, fwd(b, s), 0))
    spec_b = pl.BlockSpec((heads, ts, dh), lambda b, s: (0, bwd(b, s), 0))
    log_f = pl.BlockSpec((heads, ts, lf.shape[2]), lambda b, s: (0, fwd(b, s), 0))
    log_b = pl.BlockSpec((heads, ts, lb.shape[2]), lambda b, s: (0, bwd(b, s), 0))
    out = jax.ShapeDtypeStruct((heads, t_rows, dh), F32)
    return pl.pallas_call(
        functools.partial(_hgrn_kernel, chunk=HGRN_CHUNK),
        grid=(batch, n_ctx + n_seq),
        in_specs=[spec_f, spec_f, spec_f, log_f, spec_b, spec_b, spec_b, log_b],
        out_specs=[spec_f, spec_b],
        out_shape=[out, out],
        scratch_shapes=[pltpu.VMEM((2, heads, dh, dh), F32)],
        compiler_params=_params("arbitrary", "arbitrary"),
        name="hgrn_scan",
    )(hq, hi, kf, lf, hq, hi, kb, lb)


def _attn_kernel(q_ref, kc_ref, vc_ref, kl_ref, vl_ref, o_ref, *, n_lat_q_tiles, k_tile):
    tq = q_ref.shape[0]
    q = jnp.concatenate([q_ref[:, i * HEAD_DIM:(i + 1) * HEAD_DIM] for i in range(ATT_GROUP)], axis=0)
    rows = q.shape[0]

    def update(carry, k, v):
        m, acc = carry
        ones_col = (lax.broadcasted_iota(jnp.int32, v.shape, 1) == 0).astype(BF16)
        s = _dot_nt(q, k)
        m_new = jnp.maximum(m, jnp.max(s, axis=-1, keepdims=True))
        p = jnp.exp2((s - m_new).astype(BF16))
        acc = jnp.exp2(m - m_new) * acc + _dot(p, jnp.concatenate([v, ones_col], axis=1))
        return m_new, acc

    def finish(carry):
        acc = carry[1]
        out = acc[:, :HEAD_DIM] / acc[:, HEAD_DIM:HEAD_DIM + 1]
        for i in range(ATT_GROUP):
            o_ref[:, i * HEAD_DIM:(i + 1) * HEAD_DIM] = out[i * tq:(i + 1) * tq].astype(BF16)

    def ctx_keys():
        init = (jnp.full((rows, 1), -1e30, F32), jnp.zeros((rows, 2 * HEAD_DIM), F32))
        return update(init, kc_ref[...], vc_ref[...])

    is_latent = pl.program_id(2) < n_lat_q_tiles

    @pl.when(is_latent)
    def _():
        carry = ctx_keys()
        for j in range(kl_ref.shape[0] // k_tile):
            carry = update(carry, kl_ref[j * k_tile:(j + 1) * k_tile, :], vl_ref[j * k_tile:(j + 1) * k_tile, :])
        finish(carry)

    @pl.when(jnp.logical_not(is_latent))
    def _():
        finish(ctx_keys())


def _attention(aq, ak, av, *, batch, seq, ctx_len, with_ctx_queries):
    t_rows = aq.shape[0]
    tq = ATT_Q_TILE
    nq_lat = seq // tq
    nq_ctx = ctx_len // tq
    ctx_q0 = batch * nq_lat
    ctx_k0 = batch * seq // ctx_len
    gw = ATT_GROUP * HEAD_DIM

    def q_blk(b, i):
        return jnp.where(i < nq_lat, b * nq_lat + i, ctx_q0 + b * nq_ctx + i - nq_lat)

    q_spec = pl.BlockSpec((tq, gw), lambda b, g, i: (q_blk(b, i), g))
    kc_spec = pl.BlockSpec((ctx_len, HEAD_DIM), lambda b, g, i: (ctx_k0 + b, g))
    kl_spec = pl.BlockSpec((seq, HEAD_DIM), lambda b, g, i: (b, g))
    return pl.pallas_call(
        functools.partial(_attn_kernel, n_lat_q_tiles=nq_lat, k_tile=ATT_K_TILE),
        grid=(batch, ATT_KV_HEADS, nq_lat + (nq_ctx if with_ctx_queries else 0)),
        in_specs=[q_spec, kc_spec, kc_spec, kl_spec, kl_spec],
        out_specs=q_spec,
        out_shape=jax.ShapeDtypeStruct((t_rows if with_ctx_queries else batch * seq, ATT_HEADS * HEAD_DIM),
                                       BF16),
        compiler_params=_params("arbitrary", "arbitrary", "arbitrary"),
        name="gqa_attention",
    )(aq, ak, av, ak, av)


def _merge_kernel(x_ref, att_ref, of_ref, ob_ref, hg_ref, ga_ref, gh_ref, g1_ref, hgn_ref, post_ref,
                  wa_ref, wh_ref, wo_ref, o_ref, hg_scr):
    hgn = hgn_ref[...]
    for i in range(HGRN_HEADS):
        o = _rms(of_ref[i] + ob_ref[i], hgn)
        hg_scr[:, i * HEAD_DIM:(i + 1) * HEAD_DIM] = (o * hg_ref[i].astype(F32)).astype(BF16)
    y = (ga_ref[...].astype(F32) * _dot(att_ref[...], wa_ref[...])
         + gh_ref[...].astype(F32) * _dot(hg_scr[...], wh_ref[...]))
    z = _rms(_dot(y.astype(BF16), wo_ref[...]), post_ref[...])
    o_ref[...] = x_ref[...] + g1_ref[...] * z


def _merge(x, att, o_f, o_b, hg, ga, gh, mod, hg_norm, post_norm, w_att, w_hg, w_out, *, rows, n_lat,
           seq):
    d = x.shape[1]
    tm = TOKEN_TILE
    n_lat_tiles = n_lat // tm
    per_seq = seq // tm

    def mod_row(i):
        return jnp.where(i < n_lat_tiles, i // per_seq, mod.shape[0] - 1 - 3)

    tok = pl.BlockSpec((tm, d), lambda i: (i, 0))
    hm = pl.BlockSpec((HGRN_HEADS, tm, HEAD_DIM), lambda i: (0, i, 0))
    vec = lambda w: pl.BlockSpec((1, w), lambda i: (0, 0))
    wgt = _resident((d, d), lambda i: (0, 0))
    return pl.pallas_call(
        _merge_kernel,
        grid=(rows // tm,),
        in_specs=[tok, tok, hm, hm, hm, tok, tok,
                  pl.BlockSpec((None, 1, d), lambda i: (mod_row(i), 0, 2)),
                  vec(HEAD_DIM), vec(d), wgt, wgt, wgt],
        out_specs=tok,
        out_shape=jax.ShapeDtypeStruct((rows, d), F32),
        scratch_shapes=[pltpu.VMEM((tm, d), BF16)],
        compiler_params=_params("arbitrary"),
        name="branch_merge",
    )(x, att, o_f, o_b, hg, ga, gh, mod, hg_norm, post_norm, w_att, w_hg, w_out)


def _ffn_kernel(x_ref, sh_ref, sc_ref, g2_ref, pre_ref, post_ref, wg_ref, wu_ref, wd_ref, o_ref, a_scr,
                *, ff_tile):
    x = x_ref[...]
    f = (_rms(x, pre_ref[...]) * (1.0 + sc_ref[...]) + sh_ref[...]).astype(BF16)
    for c in range(0, wg_ref.shape[1], ff_tile):
        g = _dot(f, wg_ref[:, c:c + ff_tile])
        u = _dot(f, wu_ref[:, c:c + ff_tile])
        a_scr[:, c:c + ff_tile] = (_silu(g) * u).astype(BF16)
    y = _rms(_dot(a_scr[...], wd_ref[...]), post_ref[...])
    o_ref[...] = x + g2_ref[...] * y


def _dense_ffn(x, mod, pre_norm, post_norm, w_gate, w_up, w_down, *, n_lat, seq):
    rows, d = x.shape
    d_ff = w_gate.shape[1]
    tm = TOKEN_TILE
    n_lat_tiles = n_lat // tm
    per_seq = seq // tm

    def mod_row(i):
        return jnp.where(i < n_lat_tiles, i // per_seq, mod.shape[0] - 1 - 3)

    tok = pl.BlockSpec((tm, d), lambda i: (i, 0))
    vec = pl.BlockSpec((1, d), lambda i: (0, 0))
    mod_spec = lambda col: pl.BlockSpec((None, 1, d), lambda i: (mod_row(i), 0, col))
    return pl.pallas_call(
        functools.partial(_ffn_kernel, ff_tile=256),
        grid=(rows // tm,),
        in_specs=[tok, mod_spec(3), mod_spec(4), mod_spec(5), vec, vec,
                  _resident((d, d_ff), lambda i: (0, 0)), _resident((d, d_ff), lambda i: (0, 0)),
                  _resident((d_ff, d), lambda i: (0, 0))],
        out_specs=tok,
        out_shape=jax.ShapeDtypeStruct((rows, d), F32),
        scratch_shapes=[pltpu.VMEM((tm, d_ff), BF16)],
        compiler_params=_params("arbitrary"),
        name="dense_swiglu",
    )(x, mod, mod, mod, pre_norm, post_norm, w_gate, w_up, w_down)


def _router_kernel(x_ref, sh_ref, sc_ref, pre_ref, wr_ref, f_ref, route_ref):
    f = _rms(x_ref[...], pre_ref[...]) * (1.0 + sc_ref[...]) + sh_ref[...]
    f_ref[...] = f
    logits = jnp.dot(f, wr_ref[...], preferred_element_type=F32, precision=lax.Precision.HIGHEST)
    lane = lax.broadcasted_iota(jnp.int32, logits.shape, 1)
    neg = jnp.float32(-jnp.inf)
    logits = jnp.where(lane < N_EXPERTS, logits, neg)
    m1 = jnp.max(logits, axis=-1, keepdims=True)
    i1 = jnp.min(jnp.where(logits == m1, lane, LANES), axis=-1, keepdims=True)
    rest = jnp.where(lane == i1, neg, logits)
    m2 = jnp.max(rest, axis=-1, keepdims=True)
    i2 = jnp.min(jnp.where(rest == m2, lane, LANES), axis=-1, keepdims=True)
    w1 = 1.0 / (1.0 + jnp.exp(m2 - m1))
    w2 = 1.0 - w1
    route = jnp.where(lane == 0, i1.astype(F32),
                      jnp.where(lane == 1, i2.astype(F32),
                                jnp.where(lane == 2, w1, jnp.where(lane == 3, w2, 0.0))))
    route_ref[...] = route


def _router(x, mod, pre_norm, w_router_padded, *, seq):
    rows, d = x.shape
    tm = TOKEN_TILE
    per_seq = seq // tm
    tok = pl.BlockSpec((tm, d), lambda i: (i, 0))
    mod_spec = lambda col: pl.BlockSpec((None, 1, d), lambda i: (i // per_seq, 0, col))
    return pl.pallas_call(
        _router_kernel,
        grid=(rows // tm,),
        in_specs=[tok, mod_spec(3), mod_spec(4), pl.BlockSpec((1, d), lambda i: (0, 0)),
                  pl.BlockSpec((d, LANES), lambda i: (0, 0))],
        out_specs=[tok, pl.BlockSpec((tm, LANES), lambda i: (i, 0))],
        out_shape=[jax.ShapeDtypeStruct((rows, d), F32), jax.ShapeDtypeStruct((rows, LANES), F32)],
        compiler_params=_params("arbitrary"),
        name="moe_router",
    )(x, mod, mod, pre_norm, w_router_padded)


def _expert_kernel(be_ref, tok_ref, dst_ref, f_hbm, wg_ref, wu_ref, wd_ref, ysel_hbm,
                   xbuf, ybuf, a_scr, gsem, ssem, *, ff_tile, n_blocks, spare_row0):
    del be_ref
    s = pl.program_id(0)
    rows = xbuf.shape[1]

    def gather_row(r, slot):
        pltpu.make_async_copy(f_hbm.at[pl.ds(tok_ref[r], 1)], xbuf.at[slot, pl.ds(r, 1)],
                              gsem.at[slot]).start()

    def scatter_row(r, slot, dst_row):
        pltpu.make_async_copy(ybuf.at[slot, pl.ds(r, 1)], ysel_hbm.at[pl.ds(dst_row, 1)],
                              ssem.at[slot]).start()

    def wait_gather(slot):
        pltpu.make_async_copy(f_hbm.at[pl.ds(0, rows)], xbuf.at[slot], gsem.at[slot]).wait()

    def wait_scatter(slot):
        pltpu.make_async_copy(ybuf.at[slot], ysel_hbm.at[pl.ds(0, rows)], ssem.at[slot]).wait()

    @pl.when(s == 0)
    def _():
        ybuf[...] = jnp.zeros(ybuf.shape, F32)

        def body(r, carry):
            gather_row(r, 0)
            scatter_row(r, 0, spare_row0 + r)
            return carry
        lax.fori_loop(0, rows, body, 0)

    @pl.when((s >= 1) & (s <= n_blocks))
    def _():
        slot = (s - 1) % 2
        wait_gather(slot)
        wait_scatter(slot)
        xb = xbuf[slot].astype(BF16)
        chunks = list(range(0, wg_ref.shape[1], ff_tile))
        per = -(-rows // (len(chunks) + 1))

        def move_rows(j):
            for r in range(j * per, min((j + 1) * per, rows)):
                gather_row(r, 1 - slot)
                scatter_row(r, 1 - slot, dst_ref[r])

        for j, c in enumerate(chunks):
            move_rows(j)
            g = _dot(xb, wg_ref[:, c:c + ff_tile])
            u = _dot(xb, wu_ref[:, c:c + ff_tile])
            a_scr[:, c:c + ff_tile] = (_silu(g) * u).astype(BF16)
        move_rows(len(chunks))
        ybuf[slot] = _dot(a_scr[...], wd_ref[...])

    @pl.when(s == n_blocks + 1)
    def _():
        slot = (n_blocks - 1) % 2

        def body(r, carry):
            scatter_row(r, slot, dst_ref[r])
            return carry
        lax.fori_loop(0, rows, body, 0)
        wait_scatter(1 - slot)
        wait_scatter(slot)
        wait_gather(1 - slot)


def _expert_ffn(f, row_tok, row_dst_ext, block_expert, w_gate, w_up, w_down, *, n_out_rows, spare_row0):
    n_rows = row_tok.shape[0]
    d = f.shape[1]
    d_ff = w_gate.shape[2]
    n_blocks = n_rows // MOE_BLOCK
    idx = lambda fn: pl.BlockSpec((MOE_BLOCK,), fn, memory_space=pltpu.SMEM)
    wgt = lambda shape: pl.BlockSpec((None,) + shape,
                                     lambda s, be: (be[jnp.clip(s - 1, 0, n_blocks - 1)], 0, 0))
    grid_spec = pltpu.PrefetchScalarGridSpec(
        num_scalar_prefetch=1,
        grid=(n_blocks + 2,),
        in_specs=[idx(lambda s, be: (jnp.minimum(s, n_blocks - 1),)),
                  idx(lambda s, be: (jnp.maximum(s - 1, 0),)),
                  pl.BlockSpec(memory_space=pl.ANY),
                  wgt((d, d_ff)), wgt((d, d_ff)), wgt((d_ff, d))],
        out_specs=pl.BlockSpec(memory_space=pl.ANY),
        scratch_shapes=[pltpu.VMEM((2, MOE_BLOCK, d), F32), pltpu.VMEM((2, MOE_BLOCK, d), F32),
                        pltpu.VMEM((MOE_BLOCK, d_ff), BF16),
                        pltpu.SemaphoreType.DMA((2,)), pltpu.SemaphoreType.DMA((2,))],
    )
    return pl.pallas_call(
        functools.partial(_expert_kernel, ff_tile=512, n_blocks=n_blocks, spare_row0=spare_row0),
        grid_spec=grid_spec,
        out_shape=jax.ShapeDtypeStruct((n_out_rows, d), F32),
        compiler_params=_params("arbitrary"),
        name="moe_expert_swiglu",
    )(block_expert, row_tok, row_dst_ext, f, w_gate, w_up, w_down)


def _combine_kernel(y0_ref, y1_ref, x_ref, route_ref, g2_ref, post_ref, o_ref):
    route = route_ref[...]
    y = route[:, 2:3] * y0_ref[...] + route[:, 3:4] * y1_ref[...]
    o_ref[...] = x_ref[...] + g2_ref[...] * _rms(y, post_ref[...])


def _combine(y_sel, x, route, mod, post_norm, *, seq, choice_stride):
    rows, d = x.shape
    tm = TOKEN_TILE
    per_seq = seq // tm
    tok = pl.BlockSpec((tm, d), lambda i: (i, 0))
    return pl.pallas_call(
        _combine_kernel,
        grid=(rows // tm,),
        in_specs=[tok, pl.BlockSpec((tm, d), lambda i: (choice_stride // tm + i, 0)), tok,
                  pl.BlockSpec((tm, LANES), lambda i: (i, 0)),
                  pl.BlockSpec((None, 1, d), lambda i: (i // per_seq, 0, 5)),
                  pl.BlockSpec((1, d), lambda i: (0, 0))],
        out_specs=tok,
        out_shape=jax.ShapeDtypeStruct((rows, d), F32),
        compiler_params=_params("arbitrary"),
        name="moe_combine",
    )(y_sel, y_sel, x, route, mod, post_norm)


def _moe_ffn(x, mod, pre_norm, post_norm, w_router, w_gate, w_up, w_down, *, seq):
    n_tok, d = x.shape
    n_assign = n_tok * TOP_K
    n_blocks = -(-n_assign // MOE_BLOCK) + N_EXPERTS
    n_rows = n_blocks * MOE_BLOCK
    n_pad = n_rows - n_assign
    assert TOP_K == 2 and n_pad % (TOP_K * TOKEN_TILE) == 0
    wr = jnp.pad(w_router, ((0, 0), (0, LANES - N_EXPERTS)))
    f, route = _router(x, mod, pre_norm, wr, seq=seq)

    flat_e = route[:, :TOP_K].astype(jnp.int32).reshape(-1)
    onehot = (flat_e[:, None] == jnp.arange(N_EXPERTS, dtype=jnp.int32)[None, :]).astype(jnp.int32)
    csum = jnp.cumsum(onehot, axis=0)
    counts = csum[-1]
    rank = jnp.sum((csum - onehot) * onehot, axis=1)
    padded = (counts + MOE_BLOCK - 1) // MOE_BLOCK * MOE_BLOCK
    pend = jnp.cumsum(padded)
    dest = ((pend - padded)[flat_e] + rank).astype(jnp.int32)
    block_expert = jnp.minimum(
        jnp.searchsorted(pend, jnp.arange(n_blocks, dtype=jnp.int32) * MOE_BLOCK, side='right'),
        N_EXPERTS - 1).astype(jnp.int32)
    choice_stride = n_tok + n_pad // TOP_K
    row_assign = jnp.full((n_rows,), -1, jnp.int32).at[dest].set(jnp.arange(n_assign, dtype=jnp.int32))
    is_pad = row_assign < 0
    pad_id = jnp.cumsum(is_pad.astype(jnp.int32)) - 1
    row_tok = jnp.where(is_pad, 0, row_assign // TOP_K)
    row_dst = jnp.where(is_pad, (pad_id % TOP_K) * choice_stride + n_tok + pad_id // TOP_K,
                        (row_assign % TOP_K) * choice_stride + row_assign // TOP_K)
    row_dst_ext = jnp.concatenate([n_rows + jnp.arange(MOE_BLOCK, dtype=jnp.int32), row_dst])

    y_sel = _expert_ffn(f, row_tok, row_dst_ext, block_expert, w_gate, w_up, w_down,
                        n_out_rows=n_rows + 2 * MOE_BLOCK, spare_row0=n_rows + MOE_BLOCK)
    return _combine(y_sel, x, route, mod, post_norm, seq=seq, choice_stride=choice_stride)


def _rope_tables(seq, pad_rows):
    pos = jnp.arange(seq)
    nfreq = HEAD_DIM // 4
    inv_freq = ROPE_THETA ** (-jnp.arange(nfreq, dtype=F32) / nfreq)
    ang = jnp.concatenate([(pos // GRID_W).astype(F32)[:, None] * inv_freq,
                           (pos % GRID_W).astype(F32)[:, None] * inv_freq], axis=-1)
    ang = jnp.concatenate([ang, ang], axis=-1)
    sign = jnp.where(jnp.arange(HEAD_DIM) < HEAD_DIM // 2, -1.0, 1.0).astype(F32)
    cos = jnp.concatenate([jnp.cos(ang), jnp.ones((pad_rows, HEAD_DIM), F32)], axis=0)
    sin = jnp.concatenate([jnp.sin(ang) * sign, jnp.zeros((pad_rows, HEAD_DIM), F32)], axis=0)
    return cos, sin


def kernel(x, c, ctx, c_ctx, w_mod, b_mod, pre_mix_norm, post_mix_norm, pre_ffn_norm, post_ffn_norm, w_in, q_norm, k_norm, hg_norm, hg_lb_logits, w_att_branch, w_hg_branch, w_out, ffn_w_gate, ffn_w_up, ffn_w_down, moe_router, moe_w_gate, moe_w_up, moe_w_down):
    batch, seq, d = x.shape
    ctx_len = ctx.shape[1]
    depth = w_mod.shape[0]
    n_lat = batch * seq
    assert d == D_MODEL and w_in.shape[2] == PROJ_WIDTH
    assert seq % HGRN_STEP == 0 and ctx_len % HGRN_STEP == 0 and n_lat % TOKEN_TILE == 0
    assert seq % ctx_len == 0 and ctx_len % TOKEN_TILE == 0 and seq % ATT_K_TILE == 0

    c_rows = jnp.concatenate([c, c_ctx[None, :], jnp.zeros((3, d), F32)], axis=0)
    mod_all = _mod_vectors(c_rows, w_mod, b_mod)
    cos_tab, sin_tab = _rope_tables(seq, TOKEN_TILE)
    row = lambda v: v.reshape(1, -1)

    xt = jnp.concatenate([x.reshape(n_lat, d), ctx.reshape(batch * ctx_len, d)], axis=0)
    for layer in range(depth):
        last = layer == depth - 1
        mod = mod_all[layer].reshape(batch + 4, 1, N_MOD * d)
        (aq, ak, av, hq, kf, lf, kb, lb, hi, hg, ga, gh) = _in_projection(
            xt, mod, row(pre_mix_norm[layer]), w_in[layer].astype(BF16), cos_tab, sin_tab,
            row(q_norm[layer]), row(k_norm[layer]), hg_lb_logits, layer=layer, n_lat=n_lat, seq=seq)
        o_f, o_b = _hgrn_scan(hq, hi, kf, lf, kb, lb, batch=batch, seq=seq, ctx_len=ctx_len)
        att = _attention(aq, ak, av, batch=batch, seq=seq, ctx_len=ctx_len, with_ctx_queries=not last)
        rows = n_lat if last else xt.shape[0]
        xt = _merge(xt, att, o_f, o_b, hg, ga, gh, mod, row(hg_norm[layer]), row(post_mix_norm[layer]),
                    w_att_branch[layer].astype(BF16), w_hg_branch[layer].astype(BF16),
                    w_out[layer].astype(BF16), rows=rows, n_lat=n_lat, seq=seq)
        idx = layer // 2
        if layer % 2 == 0:
            xt = _dense_ffn(xt, mod, row(pre_ffn_norm[layer]), row(post_ffn_norm[layer]),
                            ffn_w_gate[idx].astype(BF16), ffn_w_up[idx].astype(BF16),
                            ffn_w_down[idx].astype(BF16), n_lat=n_lat, seq=seq)
        else:
            assert last, "the expert mixer is implemented for latent tokens only"
            xt = _moe_ffn(xt, mod, row(pre_ffn_norm[layer]), row(post_ffn_norm[layer]), moe_router[idx],
                          moe_w_gate[idx].astype(BF16), moe_w_up[idx].astype(BF16),
                          moe_w_down[idx].astype(BF16), seq=seq)
    return xt[:n_lat].reshape(batch, seq, d)
```

```python
import functools

import jax
import jax.numpy as jnp
from jax import lax
from jax.experimental import pallas as pl
from jax.experimental.pallas import tpu as pltpu

F32 = jnp.float32
BF16 = jnp.bfloat16

D_MODEL = 1024
NORM_EPS = 1e-6
LOG2_E = 1.4426950408889634
N_MOD = 6
GRID_W = 64
ROPE_THETA = 10000.0

HEAD_DIM = 128
ATT_HEADS = 8
ATT_KV_HEADS = 2
ATT_GROUP = ATT_HEADS // ATT_KV_HEADS
KV_WIDTH = ATT_KV_HEADS * HEAD_DIM
HGRN_HEADS = 8

N_EXPERTS = 8
TOP_K = 2
MOE_BLOCK = 256

SUBLANES = 8
LANES = 128
VMEM_LIMIT_BYTES = 56 * 1024 * 1024

TOKEN_TILE = 256
HGRN_CHUNK = 128
HGRN_STEP = 256
LOG_F_TERMS = 2
ATT_Q_TILE = 128
ATT_Q_STEP = 256
ATT_K_TILE = 512

_C_AQ = 0
_C_AK = _C_AQ + ATT_HEADS * HEAD_DIM
_C_AV = _C_AK + KV_WIDTH
_C_HQ = _C_AV + KV_WIDTH
_C_HFF = _C_HQ + D_MODEL
_C_HFB = _C_HFF + D_MODEL
_C_HI = _C_HFB + D_MODEL
_C_HG = _C_HI + D_MODEL
_C_GA = _C_HG + D_MODEL
_C_GH = _C_GA + D_MODEL
PROJ_WIDTH = _C_GH + D_MODEL


def _params(*sem):
    return pltpu.CompilerParams(dimension_semantics=sem, vmem_limit_bytes=VMEM_LIMIT_BYTES)


def _resident(shape, index_map):
    return pl.BlockSpec(shape, index_map, pipeline_mode=pl.Buffered(1))


def _rms(t, gain):
    return t * lax.rsqrt(jnp.mean(t * t, axis=-1, keepdims=True) + NORM_EPS) * gain


def _silu(t):
    return t * jax.nn.sigmoid(t)


def _dot(a, b):
    return jnp.dot(a, b, preferred_element_type=F32)


def _dot_nt(a, b):
    return lax.dot_general(a, b, (((1,), (1,)), ((), ())), preferred_element_type=F32)


def _dot_tn(a, b):
    return lax.dot_general(a, b, (((0,), (0,)), ((), ())), preferred_element_type=F32)


def _mod_kernel(c_ref, w_ref, b_ref, o_ref):
    a = _silu(c_ref[...])
    o_ref[...] = jnp.dot(a, w_ref[...], preferred_element_type=F32,
                         precision=lax.Precision.HIGHEST) + b_ref[...]


def _mod_vectors(c_rows, w_mod, b_mod):
    depth, d, width = w_mod.shape
    rows = c_rows.shape[0]
    tn = 1536
    return pl.pallas_call(
        _mod_kernel,
        grid=(depth, width // tn),
        in_specs=[
            pl.BlockSpec((rows, d), lambda l, j: (0, 0)),
            pl.BlockSpec((None, d, tn), lambda l, j: (l, 0, j)),
            pl.BlockSpec((None, 1, tn), lambda l, j: (l, 0, j)),
        ],
        out_specs=pl.BlockSpec((None, rows, tn), lambda l, j: (l, 0, j)),
        out_shape=jax.ShapeDtypeStruct((depth, rows, width), F32),
        compiler_params=_params("arbitrary", "arbitrary"),
        name="mod_vectors",
    )(c_rows, w_mod, b_mod.reshape(depth, 1, width))


def _inproj_kernel(x_ref, sh_ref, sc_ref, gain_ref, w_ref, cos_ref, sin_ref, qn_ref, kn_ref, lbl_ref,
                   aq_ref, ak_ref, av_ref, hq_ref, kf_ref, lf_ref, kb_ref, lb_ref, hi_ref, hg_ref,
                   ga_ref, gh_ref, *, layer):
    h = _rms(x_ref[...], gain_ref[...])
    hb = (h * (1.0 + sc_ref[...]) + sh_ref[...]).astype(BF16)
    cos = cos_ref[...]
    sin = sin_ref[...]

    def proj(c0, width):
        return _dot(hb, w_ref[:, c0:c0 + width])

    def head(t, i):
        return t[:, i * HEAD_DIM:(i + 1) * HEAD_DIM]

    def norm_rope(t, gain):
        r = _rms(t, gain)
        return r * cos + pltpu.roll(r, HEAD_DIM // 2, 1) * sin

    qn = qn_ref[...]
    for c in range(2):
        t = proj(_C_AQ + c * 512, 512)
        for i in range(4):
            hh = c * 4 + i
            aq_ref[:, hh * HEAD_DIM:(hh + 1) * HEAD_DIM] = (
                norm_rope(head(t, i), qn) * (HEAD_DIM ** -0.5 * LOG2_E)).astype(BF16)
    t = proj(_C_AK, 2 * KV_WIDTH)
    kn = kn_ref[...]
    for i in range(ATT_KV_HEADS):
        ak_ref[:, i * HEAD_DIM:(i + 1) * HEAD_DIM] = norm_rope(head(t, i), kn).astype(BF16)
    av_ref[...] = t[:, KV_WIDTH:].astype(BF16)

    for c in range(2):
        t = proj(_C_HQ + c * 512, 512)
        for i in range(4):
            hq_ref[c * 4 + i] = _silu(head(t, i)).astype(BF16)

    lg = lbl_ref[...]
    depth = lg.shape[0]
    mx = lg[0:1]
    for j in range(1, depth):
        mx = jnp.maximum(mx, lg[j:j + 1])
    es = [jnp.exp(lg[j:j + 1] - mx) for j in range(depth)]
    tot = es[0]
    for j in range(1, depth):
        tot = tot + es[j]
    low = jnp.zeros_like(tot)
    for j in range(1, layer + 1):
        low = low + es[j] / tot

    for c0, k_ref, l_ref in ((_C_HFF, kf_ref, lf_ref), (_C_HFB, kb_ref, lb_ref)):
        for c in range(2):
            t = proj(c0 + c * 512, 512)
            for i in range(4):
                hh = c * 4 + i
                lo = low[:, hh * HEAD_DIM:(hh + 1) * HEAD_DIM]
                f = lo + (1.0 - lo) * jax.nn.sigmoid(head(t, i))
                k_ref[hh] = (1.0 - f).astype(BF16)
                rest = jnp.log(f) * LOG2_E
                terms = []
                for _ in range(LOG_F_TERMS):
                    terms.append(rest.astype(BF16))
                    rest = rest - terms[-1].astype(F32)
                l_ref[hh] = jnp.concatenate(terms, axis=1)

    for c in range(2):
        t = proj(_C_HI + c * 512, 512)
        for i in range(4):
            hi_ref[c * 4 + i] = head(t, i).astype(BF16)
    for c in range(2):
        t = proj(_C_HG + c * 512, 512)
        for i in range(4):
            hg_ref[c * 4 + i] = _silu(head(t, i)).astype(BF16)

    for c in range(2):
        ga_ref[:, c * 512:(c + 1) * 512] = jax.nn.sigmoid(proj(_C_GA + c * 512, 512)).astype(BF16)
    for c in range(2):
        gh_ref[:, c * 512:(c + 1) * 512] = jax.nn.sigmoid(proj(_C_GH + c * 512, 512)).astype(BF16)


def _in_projection(x, mod, gain, w_in, cos_tab, sin_tab, q_norm, k_norm, lb_logits, *, layer, n_lat,
                   seq):
    t_rows, d = x.shape
    tm = TOKEN_TILE
    n_lat_tiles = n_lat // tm
    per_seq = seq // tm

    def mod_row(i):
        return jnp.where(i < n_lat_tiles, i // per_seq, mod.shape[0] - 1 - 3)

    def pos_blk(i):
        return jnp.where(i < n_lat_tiles, i % per_seq, per_seq)

    tok = lambda w: pl.BlockSpec((tm, w), lambda i: (i, 0))
    hm = pl.BlockSpec((HGRN_HEADS, tm, HEAD_DIM), lambda i: (0, i, 0))
    hm3 = pl.BlockSpec((HGRN_HEADS, tm, LOG_F_TERMS * HEAD_DIM), lambda i: (0, i, 0))
    hm_shape = lambda terms: jax.ShapeDtypeStruct((HGRN_HEADS, t_rows, terms * HEAD_DIM), BF16)
    tm_shape = lambda w: jax.ShapeDtypeStruct((t_rows, w), BF16)
    return pl.pallas_call(
        functools.partial(_inproj_kernel, layer=layer),
        grid=(t_rows // tm,),
        in_specs=[
            tok(d),
            pl.BlockSpec((None, 1, d), lambda i: (mod_row(i), 0, 0)),
            pl.BlockSpec((None, 1, d), lambda i: (mod_row(i), 0, 1)),
            pl.BlockSpec((1, d), lambda i: (0, 0)),
            _resident((d, PROJ_WIDTH), lambda i: (0, 0)),
            pl.BlockSpec((tm, HEAD_DIM), lambda i: (pos_blk(i), 0)),
            pl.BlockSpec((tm, HEAD_DIM), lambda i: (pos_blk(i), 0)),
            pl.BlockSpec((1, HEAD_DIM), lambda i: (0, 0)),
            pl.BlockSpec((1, HEAD_DIM), lambda i: (0, 0)),
            pl.BlockSpec(lb_logits.shape, lambda i: (0, 0)),
        ],
        out_specs=[tok(d), tok(KV_WIDTH), tok(KV_WIDTH), hm, hm, hm3, hm, hm3, hm, hm, tok(d), tok(d)],
        out_shape=[tm_shape(d), tm_shape(KV_WIDTH), tm_shape(KV_WIDTH), hm_shape(1), hm_shape(1),
                   hm_shape(LOG_F_TERMS), hm_shape(1), hm_shape(LOG_F_TERMS), hm_shape(1), hm_shape(1),
                   tm_shape(d), tm_shape(d)],
        compiler_params=_params("arbitrary"),
        name=f"in_projection_l{layer}",
    )(x, mod, mod, gain, w_in, cos_tab, sin_tab, q_norm, k_norm, lb_logits)


def _hgrn_constants(n, c):
    t = lax.broadcasted_iota(jnp.int32, (n, n), 0)
    s = lax.broadcasted_iota(jnp.int32, (n, n), 1)
    tri = ((t >= s).astype(BF16), (t <= s).astype(BF16))
    t = lax.broadcasted_iota(jnp.int32, (c, c), 0)
    s = lax.broadcasted_iota(jnp.int32, (c, c), 1)
    levels = []
    blk = 2
    while blk <= c:
        half = blk // 2
        same = (t // blk) == (s // blk)
        t_hi = (t % blk) >= half
        s_hi = (s % blk) >= half
        levels.append((blk, (same & t_hi & ~s_hi, same & ~t_hi & s_hi)))
        blk *= 2
    return tri, t == s, levels


def _boundary_decay(cum, blk, rev):
    n = cum.shape[0]
    half = blk // 2
    off = half if rev else half - 1
    if blk >= 2 * SUBLANES:
        rows = [jnp.broadcast_to(cum[a0 + off:a0 + off + 1, :], (blk, LANES)) for a0 in range(0, n, blk)]
        mid = rows[0] if len(rows) == 1 else jnp.concatenate(rows, axis=0)
    else:
        c3 = cum.reshape(n // SUBLANES, SUBLANES, LANES)
        sub = lax.broadcasted_iota(jnp.int32, c3.shape, 1)
        mid = jnp.broadcast_to(c3[:, off:off + 1, :], c3.shape)
        for a0 in range(blk, SUBLANES, blk):
            mid = jnp.where(sub >= a0, jnp.broadcast_to(c3[:, a0 + off:a0 + off + 1, :], c3.shape), mid)
        mid = mid.reshape(n, LANES)
    x = lax.bitcast_convert_type(cum - mid, jnp.uint32) | jnp.uint32(0x80000000)
    return jnp.exp2(lax.bitcast_convert_type(x, F32))


def _hgrn_block(q, k, v, g3, st_t, consts, chunk, rev):
    tri, eye, levels = consts
    n = q.shape[0]
    parts = _dot(tri[rev], g3)
    cum = parts[:, :LANES]
    for i in range(1, g3.shape[1] // LANES):
        cum = cum + parts[:, i * LANES:(i + 1) * LANES]
    tail = cum[0:1] if rev else cum[n - 1:n]

    qf = q.astype(F32)
    kf = k.astype(F32)
    scaled = lambda t, e: (t * e).astype(BF16)
    out = _dot_nt(scaled(qf, jnp.exp2(cum)), st_t.astype(BF16))
    k_end = scaled(kf, jnp.exp2(tail - cum))
    st_new = st_t * jnp.exp2(tail) + _dot_tn(v, k_end)

    blk = n
    far = []
    while blk > chunk:
        half = blk // 2
        e = _boundary_decay(cum, blk, rev)
        qe = scaled(qf, e)
        ke = scaled(kf, e)
        for a0 in range(0, n, blk):
            qr = a0 if rev else a0 + half
            kr = a0 + half if rev else a0
            sc = _dot_nt(qe[qr:qr + half], ke[kr:kr + half])
            far.append((qr, half, _dot(sc.astype(BF16), v[kr:kr + half])))
        blk //= 2

    sides = [(q, k)]
    for blk, _ in levels:
        if blk == 2:
            sides.append((scaled(qf, jnp.exp2(g3[:, :LANES].astype(F32))), k))
        else:
            e = _boundary_decay(cum, blk, rev)
            sides.append((scaled(qf, e), scaled(kf, e)))
    masks = [eye] + [m[rev] for _, m in levels]
    outs = []
    for c0 in range(0, n, chunk):
        a = jnp.zeros((chunk, chunk), F32)
        rows = slice(c0, c0 + chunk)
        zero = jnp.zeros((chunk, LANES), BF16)
        for i in range(0, len(sides) - 1, 2):
            (qa, ka), (qb, kb) = sides[i], sides[i + 1]
            keys = jnp.concatenate([jnp.concatenate([ka[rows], zero], axis=1),
                                    jnp.concatenate([zero, kb[rows]], axis=1)], axis=0)
            sc = _dot_nt(jnp.concatenate([qa[rows], qb[rows]], axis=1), keys)
            a = jnp.where(masks[i], sc[:, :chunk], a)
            a = jnp.where(masks[i + 1], sc[:, chunk:], a)
        if len(sides) % 2:
            a = jnp.where(masks[-1], _dot_nt(sides[-1][0][rows], sides[-1][1][rows]), a)
        o = out[c0:c0 + chunk] + _dot(a.astype(BF16), v[c0:c0 + chunk])
        for r0, rows, val in far:
            if r0 <= c0 < r0 + rows:
                o = o + val[c0 - r0:c0 - r0 + chunk]
        outs.append(o)
    return (outs[0] if len(outs) == 1 else jnp.concatenate(outs, axis=0)), st_new


def _softmax_tile(q, carry, k, v):
    m, acc = carry
    ones_col = (lax.broadcasted_iota(jnp.int32, v.shape, 1) == 0).astype(BF16)
    s = _dot_nt(q, k)
    m_new = jnp.maximum(m, jnp.max(s, axis=-1, keepdims=True))
    p = jnp.exp2((s - m_new).astype(BF16))
    acc = jnp.exp2(m - m_new) * acc + _dot(p, jnp.concatenate([v, ones_col], axis=1))
    return m_new, acc


def _softmax_init(rows):
    return jnp.full((rows, 1), -1e30, F32), jnp.zeros((rows, 2 * HEAD_DIM), F32)


def _softmax_result(carry):
    acc = carry[1]
    return acc[:, :HEAD_DIM] / acc[:, HEAD_DIM:HEAD_DIM + 1]


def _hgrn_kernel(qf_ref, vf_ref, kf_ref, lf_ref, qb_ref, vb_ref, kb_ref, lb_ref, of_ref, ob_ref,
                 st_ref, *, chunk):
    @pl.when(pl.program_id(1) == 0)
    def _():
        st_ref[...] = jnp.zeros(st_ref.shape, F32)

    consts = _hgrn_constants(qf_ref.shape[1], chunk)

    def head_body(h, carry):
        o, st = _hgrn_block(qf_ref[h], kf_ref[h], vf_ref[h], lf_ref[h], st_ref[0, h], consts, chunk, 0)
        of_ref[h] = o
        st_ref[0, h] = st
        o, st = _hgrn_block(qb_ref[h], kb_ref[h], vb_ref[h], lb_ref[h], st_ref[1, h], consts, chunk, 1)
        ob_ref[h] = o
        st_ref[1, h] = st
        return carry

    lax.fori_loop(0, qf_ref.shape[0], head_body, 0, unroll=2)


def _hgrn_scan(hq, hi, kf, lf, kb, lb, *, batch, seq, ctx_len):
    heads, t_rows, dh = hq.shape
    ts = HGRN_STEP
    n_ctx = ctx_len // ts
    n_seq = seq // ts
    ctx0 = batch * n_seq

    def fwd(b, s):
        return jnp.where(s < n_ctx, ctx0 + b * n_ctx + s, b * n_seq + s - n_ctx)

    def bwd(b, s):
        return jnp.where(s < n_ctx, ctx0 + b * n_ctx + n_ctx - 1 - s, b * n_seq + n_seq - 1 - (s - n_ctx))

    spec_f = pl.BlockSpec((heads, ts, dh), lambda b, s: (0, fwd(b, s), 0))
    spec_b = pl.BlockSpec((heads, ts, dh), lambda b, s: (0, bwd(b, s), 0))
    log_f = pl.BlockSpec((heads, ts, lf.shape[2]), lambda b, s: (0, fwd(b, s), 0))
    log_b = pl.BlockSpec((heads, ts, lb.shape[2]), lambda b, s: (0, bwd(b, s), 0))
    out = jax.ShapeDtypeStruct((heads, t_rows, dh), F32)
    return pl.pallas_call(
        functools.partial(_hgrn_kernel, chunk=HGRN_CHUNK),
        grid=(batch, n_ctx + n_seq),
        in_specs=[spec_f, spec_f, spec_f, log_f, spec_b, spec_b, spec_b, log_b],
        out_specs=[spec_f, spec_b],
        out_shape=[out, out],
        scratch_shapes=[pltpu.VMEM((2, heads, dh, dh), F32)],
        compiler_params=_params("arbitrary", "arbitrary"),
        name="hgrn_scan",
    )(hq, hi, kf, lf, hq, hi, kb, lb)


def _attn_kernel(q_ref, kc_ref, vc_ref, kl_ref, vl_ref, o_ref, *, n_lat_q_steps, k_tile):
    tq = ATT_Q_TILE
    is_latent = pl.program_id(2) < n_lat_q_steps

    def query_tiles(with_latent_keys):
        for r0 in range(0, q_ref.shape[0], tq):
            q = jnp.concatenate([q_ref[r0:r0 + tq, i * HEAD_DIM:(i + 1) * HEAD_DIM] for i in range(ATT_GROUP)],
                                axis=0)
            carry = _softmax_tile(q, _softmax_init(ATT_GROUP * tq), kc_ref[...], vc_ref[...])
            if with_latent_keys:
                for t in range(0, kl_ref.shape[0], k_tile):
                    carry = _softmax_tile(q, carry, kl_ref[t:t + k_tile, :], vl_ref[t:t + k_tile, :])
            out = _softmax_result(carry)
            for i in range(ATT_GROUP):
                o_ref[r0:r0 + tq, i * HEAD_DIM:(i + 1) * HEAD_DIM] = out[i * tq:(i + 1) * tq].astype(BF16)

    @pl.when(is_latent)
    def _():
        query_tiles(True)

    @pl.when(jnp.logical_not(is_latent))
    def _():
        query_tiles(False)


def _attention(aq, ak, av, *, batch, seq, ctx_len, with_ctx_queries):
    t_rows = aq.shape[0]
    tq = ATT_Q_STEP
    nq_lat = seq // tq
    nq_ctx = ctx_len // tq
    ctx_q0 = batch * nq_lat
    ctx_k0 = batch * seq // ctx_len
    gw = ATT_GROUP * HEAD_DIM

    def q_blk(b, i):
        return jnp.where(i < nq_lat, b * nq_lat + i, ctx_q0 + b * nq_ctx + i - nq_lat)

    q_spec = pl.BlockSpec((tq, gw), lambda b, g, i: (q_blk(b, i), g))
    kc_spec = pl.BlockSpec((ctx_len, HEAD_DIM), lambda b, g, i: (ctx_k0 + b, g))
    kl_spec = pl.BlockSpec((seq, HEAD_DIM), lambda b, g, i: (b, g))
    return pl.pallas_call(
        functools.partial(_attn_kernel, n_lat_q_steps=nq_lat, k_tile=ATT_K_TILE),
        grid=(batch, ATT_KV_HEADS, nq_lat + (nq_ctx if with_ctx_queries else 0)),
        in_specs=[q_spec, kc_spec, kc_spec, kl_spec, kl_spec],
        out_specs=q_spec,
        out_shape=jax.ShapeDtypeStruct((t_rows if with_ctx_queries else batch * seq, ATT_HEADS * HEAD_DIM),
                                       BF16),
        compiler_params=_params("arbitrary", "arbitrary", "arbitrary"),
        name="gqa_attention",
    )(aq, ak, av, ak, av)


def _merge_kernel(x_ref, att_ref, of_ref, ob_ref, hg_ref, ga_ref, gh_ref, g1_ref, hgn_ref, post_ref,
                  wa_ref, wh_ref, wo_ref, o_ref, hg_scr):
    hgn = hgn_ref[...]
    for i in range(HGRN_HEADS):
        o = _rms(of_ref[i] + ob_ref[i], hgn)
        hg_scr[:, i * HEAD_DIM:(i + 1) * HEAD_DIM] = (o * hg_ref[i].astype(F32)).astype(BF16)
    y = (ga_ref[...].astype(F32) * _dot(att_ref[...], wa_ref[...])
         + gh_ref[...].astype(F32) * _dot(hg_scr[...], wh_ref[...]))
    z = _rms(_dot(y.astype(BF16), wo_ref[...]), post_ref[...])
    o_ref[...] = x_ref[...] + g1_ref[...] * z


def _merge(x, att, o_f, o_b, hg, ga, gh, mod, hg_norm, post_norm, w_att, w_hg, w_out, *, rows, n_lat,
           seq):
    d = x.shape[1]
    tm = TOKEN_TILE
    n_lat_tiles = n_lat // tm
    per_seq = seq // tm

    def mod_row(i):
        return jnp.where(i < n_lat_tiles, i // per_seq, mod.shape[0] - 1 - 3)

    tok = pl.BlockSpec((tm, d), lambda i: (i, 0))
    hm = pl.BlockSpec((HGRN_HEADS, tm, HEAD_DIM), lambda i: (0, i, 0))
    vec = lambda w: pl.BlockSpec((1, w), lambda i: (0, 0))
    wgt = _resident((d, d), lambda i: (0, 0))
    return pl.pallas_call(
        _merge_kernel,
        grid=(rows // tm,),
        in_specs=[tok, tok, hm, hm, hm, tok, tok,
                  pl.BlockSpec((None, 1, d), lambda i: (mod_row(i), 0, 2)),
                  vec(HEAD_DIM), vec(d), wgt, wgt, wgt],
        out_specs=tok,
        out_shape=jax.ShapeDtypeStruct((rows, d), F32),
        scratch_shapes=[pltpu.VMEM((tm, d), BF16)],
        compiler_params=_params("arbitrary"),
        name="branch_merge",
    )(x, att, o_f, o_b, hg, ga, gh, mod, hg_norm, post_norm, w_att, w_hg, w_out)


def _ffn_kernel(x_ref, sh_ref, sc_ref, g2_ref, pre_ref, post_ref, wg_ref, wu_ref, wd_ref, o_ref, a_scr,
                *, ff_tile):
    x = x_ref[...]
    f = (_rms(x, pre_ref[...]) * (1.0 + sc_ref[...]) + sh_ref[...]).astype(BF16)
    for c in range(0, wg_ref.shape[1], ff_tile):
        g = _dot(f, wg_ref[:, c:c + ff_tile])
        u = _dot(f, wu_ref[:, c:c + ff_tile])
        a_scr[:, c:c + ff_tile] = (_silu(g) * u).astype(BF16)
    y = _rms(_dot(a_scr[...], wd_ref[...]), post_ref[...])
    o_ref[...] = x + g2_ref[...] * y


def _dense_ffn(x, mod, pre_norm, post_norm, w_gate, w_up, w_down, *, n_lat, seq):
    rows, d = x.shape
    d_ff = w_gate.shape[1]
    tm = TOKEN_TILE
    n_lat_tiles = n_lat // tm
    per_seq = seq // tm

    def mod_row(i):
        return jnp.where(i < n_lat_tiles, i // per_seq, mod.shape[0] - 1 - 3)

    tok = pl.BlockSpec((tm, d), lambda i: (i, 0))
    vec = pl.BlockSpec((1, d), lambda i: (0, 0))
    mod_spec = lambda col: pl.BlockSpec((None, 1, d), lambda i: (mod_row(i), 0, col))
    return pl.pallas_call(
        functools.partial(_ffn_kernel, ff_tile=256),
        grid=(rows // tm,),
        in_specs=[tok, mod_spec(3), mod_spec(4), mod_spec(5), vec, vec,
                  _resident((d, d_ff), lambda i: (0, 0)), _resident((d, d_ff), lambda i: (0, 0)),
                  _resident((d_ff, d), lambda i: (0, 0))],
        out_specs=tok,
        out_shape=jax.ShapeDtypeStruct((rows, d), F32),
        scratch_shapes=[pltpu.VMEM((tm, d_ff), BF16)],
        compiler_params=_params("arbitrary"),
        name="dense_swiglu",
    )(x, mod, mod, mod, pre_norm, post_norm, w_gate, w_up, w_down)


def _router_kernel(x_ref, sh_ref, sc_ref, pre_ref, wr_ref, f_ref, route_ref):
    f = _rms(x_ref[...], pre_ref[...]) * (1.0 + sc_ref[...]) + sh_ref[...]
    f_ref[...] = f
    logits = jnp.dot(f, wr_ref[...], preferred_element_type=F32, precision=lax.Precision.HIGHEST)
    lane = lax.broadcasted_iota(jnp.int32, logits.shape, 1)
    neg = jnp.float32(-jnp.inf)
    logits = jnp.where(lane < N_EXPERTS, logits, neg)
    m1 = jnp.max(logits, axis=-1, keepdims=True)
    i1 = jnp.min(jnp.where(logits == m1, lane, LANES), axis=-1, keepdims=True)
    rest = jnp.where(lane == i1, neg, logits)
    m2 = jnp.max(rest, axis=-1, keepdims=True)
    i2 = jnp.min(jnp.where(rest == m2, lane, LANES), axis=-1, keepdims=True)
    w1 = 1.0 / (1.0 + jnp.exp(m2 - m1))
    w2 = 1.0 - w1
    route = jnp.where(lane == 0, i1.astype(F32),
                      jnp.where(lane == 1, i2.astype(F32),
                                jnp.where(lane == 2, w1, jnp.where(lane == 3, w2, 0.0))))
    route_ref[...] = route


def _router(x, mod, pre_norm, w_router_padded, *, seq):
    rows, d = x.shape
    tm = TOKEN_TILE
    per_seq = seq // tm
    tok = pl.BlockSpec((tm, d), lambda i: (i, 0))
    mod_spec = lambda col: pl.BlockSpec((None, 1, d), lambda i: (i // per_seq, 0, col))
    return pl.pallas_call(
        _router_kernel,
        grid=(rows // tm,),
        in_specs=[tok, mod_spec(3), mod_spec(4), pl.BlockSpec((1, d), lambda i: (0, 0)),
                  pl.BlockSpec((d, LANES), lambda i: (0, 0))],
        out_specs=[tok, pl.BlockSpec((tm, LANES), lambda i: (i, 0))],
        out_shape=[jax.ShapeDtypeStruct((rows, d), F32), jax.ShapeDtypeStruct((rows, LANES), F32)],
        compiler_params=_params("arbitrary"),
        name="moe_router",
    )(x, mod, mod, pre_norm, w_router_padded)


def _expert_kernel(be_ref, tok_ref, dst_ref, f_hbm, wg_ref, wu_ref, wd_ref, ysel_hbm,
                   xbuf, ybuf, a_scr, gsem, ssem, *, ff_tile, n_blocks, spare_row0):
    del be_ref
    s = pl.program_id(0)
    rows = xbuf.shape[1]

    def gather_row(r, slot):
        pltpu.make_async_copy(f_hbm.at[pl.ds(tok_ref[r], 1)], xbuf.at[slot, pl.ds(r, 1)],
                              gsem.at[slot]).start()

    def scatter_row(r, slot, dst_row):
        pltpu.make_async_copy(ybuf.at[slot, pl.ds(r, 1)], ysel_hbm.at[pl.ds(dst_row, 1)],
                              ssem.at[slot]).start()

    def wait_gather(slot):
        pltpu.make_async_copy(f_hbm.at[pl.ds(0, rows)], xbuf.at[slot], gsem.at[slot]).wait()

    def wait_scatter(slot):
        pltpu.make_async_copy(ybuf.at[slot], ysel_hbm.at[pl.ds(0, rows)], ssem.at[slot]).wait()

    @pl.when(s == 0)
    def _():
        ybuf[...] = jnp.zeros(ybuf.shape, F32)

        def body(r, carry):
            gather_row(r, 0)
            scatter_row(r, 0, spare_row0 + r)
            return carry
        lax.fori_loop(0, rows, body, 0)

    @pl.when((s >= 1) & (s <= n_blocks))
    def _():
        slot = (s - 1) % 2
        wait_gather(slot)
        wait_scatter(slot)
        xb = xbuf[slot].astype(BF16)
        chunks = list(range(0, wg_ref.shape[1], ff_tile))
        per = -(-rows // (len(chunks) + 1))

        def move_rows(j):
            for r in range(j * per, min((j + 1) * per, rows)):
                gather_row(r, 1 - slot)
                scatter_row(r, 1 - slot, dst_ref[r])

        for j, c in enumerate(chunks):
            move_rows(j)
            g = _dot(xb, wg_ref[:, c:c + ff_tile])
            u = _dot(xb, wu_ref[:, c:c + ff_tile])
            a_scr[:, c:c + ff_tile] = (_silu(g) * u).astype(BF16)
        move_rows(len(chunks))
        ybuf[slot] = _dot(a_scr[...], wd_ref[...])

    @pl.when(s == n_blocks + 1)
    def _():
        slot = (n_blocks - 1) % 2

        def body(r, carry):
            scatter_row(r, slot, dst_ref[r])
            return carry
        lax.fori_loop(0, rows, body, 0)
        wait_scatter(1 - slot)
        wait_scatter(slot)
        wait_gather(1 - slot)


def _expert_ffn(f, row_tok, row_dst_ext, block_expert, w_gate, w_up, w_down, *, n_out_rows, spare_row0):
    n_rows = row_tok.shape[0]
    d = f.shape[1]
    d_ff = w_gate.shape[2]
    n_blocks = n_rows // MOE_BLOCK
    idx = lambda fn: pl.BlockSpec((MOE_BLOCK,), fn, memory_space=pltpu.SMEM)
    wgt = lambda shape: pl.BlockSpec((None,) + shape,
                                     lambda s, be: (be[jnp.clip(s - 1, 0, n_blocks - 1)], 0, 0),
                                     pipeline_mode=pl.Buffered(1))
    grid_spec = pltpu.PrefetchScalarGridSpec(
        num_scalar_prefetch=1,
        grid=(n_blocks + 2,),
        in_specs=[idx(lambda s, be: (jnp.minimum(s, n_blocks - 1),)),
                  idx(lambda s, be: (jnp.maximum(s - 1, 0),)),
                  pl.BlockSpec(memory_space=pl.ANY),
                  wgt((d, d_ff)), wgt((d, d_ff)), wgt((d_ff, d))],
        out_specs=pl.BlockSpec(memory_space=pl.ANY),
        scratch_shapes=[pltpu.VMEM((2, MOE_BLOCK, d), F32), pltpu.VMEM((2, MOE_BLOCK, d), F32),
                        pltpu.VMEM((MOE_BLOCK, d_ff), BF16),
                        pltpu.SemaphoreType.DMA((2,)), pltpu.SemaphoreType.DMA((2,))],
    )
    return pl.pallas_call(
        functools.partial(_expert_kernel, ff_tile=512, n_blocks=n_blocks, spare_row0=spare_row0),
        grid_spec=grid_spec,
        out_shape=jax.ShapeDtypeStruct((n_out_rows, d), F32),
        compiler_params=_params("arbitrary"),
        name="moe_expert_swiglu",
    )(block_expert, row_tok, row_dst_ext, f, w_gate, w_up, w_down)


def _combine_kernel(y0_ref, y1_ref, x_ref, route_ref, g2_ref, post_ref, o_ref):
    route = route_ref[...]
    y = route[:, 2:3] * y0_ref[...] + route[:, 3:4] * y1_ref[...]
    o_ref[...] = x_ref[...] + g2_ref[...] * _rms(y, post_ref[...])


def _combine(y_sel, x, route, mod, post_norm, *, seq, choice_stride):
    rows, d = x.shape
    tm = TOKEN_TILE
    per_seq = seq // tm
    tok = pl.BlockSpec((tm, d), lambda i: (i, 0))
    return pl.pallas_call(
        _combine_kernel,
        grid=(rows // tm,),
        in_specs=[tok, pl.BlockSpec((tm, d), lambda i: (choice_stride // tm + i, 0)), tok,
                  pl.BlockSpec((tm, LANES), lambda i: (i, 0)),
                  pl.BlockSpec((None, 1, d), lambda i: (i // per_seq, 0, 5)),
                  pl.BlockSpec((1, d), lambda i: (0, 0))],
        out_specs=tok,
        out_shape=jax.ShapeDtypeStruct((rows, d), F32),
        compiler_params=_params("arbitrary"),
        name="moe_combine",
    )(y_sel, y_sel, x, route, mod, post_norm)


def _moe_ffn(x, mod, pre_norm, post_norm, w_router, w_gate, w_up, w_down, *, seq):
    n_tok, d = x.shape
    n_assign = n_tok * TOP_K
    n_blocks = -(-n_assign // MOE_BLOCK) + N_EXPERTS
    n_rows = n_blocks * MOE_BLOCK
    n_pad = n_rows - n_assign
    assert TOP_K == 2 and n_pad % (TOP_K * TOKEN_TILE) == 0
    wr = jnp.pad(w_router, ((0, 0), (0, LANES - N_EXPERTS)))
    f, route = _router(x, mod, pre_norm, wr, seq=seq)

    flat_e = route[:, :TOP_K].astype(jnp.int32).reshape(-1)
    onehot = (flat_e[:, None] == jnp.arange(N_EXPERTS, dtype=jnp.int32)[None, :]).astype(jnp.int32)
    csum = jnp.cumsum(onehot, axis=0)
    counts = csum[-1]
    rank = jnp.sum((csum - onehot) * onehot, axis=1)
    padded = (counts + MOE_BLOCK - 1) // MOE_BLOCK * MOE_BLOCK
    pend = jnp.cumsum(padded)
    dest = ((pend - padded)[flat_e] + rank).astype(jnp.int32)
    block_expert = jnp.minimum(
        jnp.searchsorted(pend, jnp.arange(n_blocks, dtype=jnp.int32) * MOE_BLOCK, side='right'),
        N_EXPERTS - 1).astype(jnp.int32)
    choice_stride = n_tok + n_pad // TOP_K
    row_assign = jnp.full((n_rows,), -1, jnp.int32).at[dest].set(jnp.arange(n_assign, dtype=jnp.int32),
                                                                 unique_indices=True)
    is_pad = row_assign < 0
    pad_id = jnp.cumsum(is_pad.astype(jnp.int32)) - 1
    row_tok = jnp.where(is_pad, 0, row_assign // TOP_K)
    row_dst = jnp.where(is_pad, (pad_id % TOP_K) * choice_stride + n_tok + pad_id // TOP_K,
                        (row_assign % TOP_K) * choice_stride + row_assign // TOP_K)
    row_dst_ext = jnp.concatenate([n_rows + jnp.arange(MOE_BLOCK, dtype=jnp.int32), row_dst])

    y_sel = _expert_ffn(f, row_tok, row_dst_ext, block_expert, w_gate, w_up, w_down,
                        n_out_rows=n_rows + 2 * MOE_BLOCK, spare_row0=n_rows + MOE_BLOCK)
    return _combine(y_sel, x, route, mod, post_norm, seq=seq, choice_stride=choice_stride)


def _rope_tables(seq, pad_rows):
    pos = jnp.arange(seq)
    nfreq = HEAD_DIM // 4
    inv_freq = ROPE_THETA ** (-jnp.arange(nfreq, dtype=F32) / nfreq)
    ang = jnp.concatenate([(pos // GRID_W).astype(F32)[:, None] * inv_freq,
                           (pos % GRID_W).astype(F32)[:, None] * inv_freq], axis=-1)
    ang = jnp.concatenate([ang, ang], axis=-1)
    sign = jnp.where(jnp.arange(HEAD_DIM) < HEAD_DIM // 2, -1.0, 1.0).astype(F32)
    cos = jnp.concatenate([jnp.cos(ang), jnp.ones((pad_rows, HEAD_DIM), F32)], axis=0)
    sin = jnp.concatenate([jnp.sin(ang) * sign, jnp.zeros((pad_rows, HEAD_DIM), F32)], axis=0)
    return cos, sin


def kernel(x, c, ctx, c_ctx, w_mod, b_mod, pre_mix_norm, post_mix_norm, pre_ffn_norm, post_ffn_norm, w_in, q_norm, k_norm, hg_norm, hg_lb_logits, w_att_branch, w_hg_branch, w_out, ffn_w_gate, ffn_w_up, ffn_w_down, moe_router, moe_w_gate, moe_w_up, moe_w_down):
    batch, seq, d = x.shape
    ctx_len = ctx.shape[1]
    depth = w_mod.shape[0]
    n_lat = batch * seq
    assert d == D_MODEL and w_in.shape[2] == PROJ_WIDTH
    assert seq % HGRN_STEP == 0 and ctx_len % HGRN_STEP == 0 and n_lat % TOKEN_TILE == 0
    assert seq % ctx_len == 0 and ctx_len % TOKEN_TILE == 0 and seq % ATT_K_TILE == 0

    c_rows = jnp.concatenate([c, c_ctx[None, :], jnp.zeros((3, d), F32)], axis=0)
    mod_all = _mod_vectors(c_rows, w_mod, b_mod)
    cos_tab, sin_tab = _rope_tables(seq, TOKEN_TILE)
    row = lambda v: v.reshape(1, -1)

    xt = jnp.concatenate([x.reshape(n_lat, d), ctx.reshape(batch * ctx_len, d)], axis=0)
    for layer in range(depth):
        last = layer == depth - 1
        mod = mod_all[layer].reshape(batch + 4, 1, N_MOD * d)
        (aq, ak, av, hq, kf, lf, kb, lb, hi, hg, ga, gh) = _in_projection(
            xt, mod, row(pre_mix_norm[layer]), w_in[layer].astype(BF16), cos_tab, sin_tab,
            row(q_norm[layer]), row(k_norm[layer]), hg_lb_logits, layer=layer, n_lat=n_lat, seq=seq)
        o_f, o_b = _hgrn_scan(hq, hi, kf, lf, kb, lb, batch=batch, seq=seq, ctx_len=ctx_len)
        att = _attention(aq, ak, av, batch=batch, seq=seq, ctx_len=ctx_len, with_ctx_queries=not last)
        rows = n_lat if last else xt.shape[0]
        xt = _merge(xt, att, o_f, o_b, hg, ga, gh, mod, row(hg_norm[layer]), row(post_mix_norm[layer]),
                    w_att_branch[layer].astype(BF16), w_hg_branch[layer].astype(BF16),
                    w_out[layer].astype(BF16), rows=rows, n_lat=n_lat, seq=seq)
        idx = layer // 2
        if layer % 2 == 0:
            xt = _dense_ffn(xt, mod, row(pre_ffn_norm[layer]), row(post_ffn_norm[layer]),
                            ffn_w_gate[idx].astype(BF16), ffn_w_up[idx].astype(BF16),
                            ffn_w_down[idx].astype(BF16), n_lat=n_lat, seq=seq)
        else:
            assert last, "the expert mixer is implemented for latent tokens only"
            xt = _moe_ffn(xt, mod, row(pre_ffn_norm[layer]), row(post_ffn_norm[layer]), moe_router[idx],
                          moe_w_gate[idx].astype(BF16), moe_w_up[idx].astype(BF16),
                          moe_w_down[idx].astype(BF16), seq=seq)
    return xt[:n_lat].reshape(batch, seq, d)
```

```python
import functools

import jax
import jax.numpy as jnp
from jax import lax
from jax.experimental import pallas as pl
from jax.experimental.pallas import tpu as pltpu

F32 = jnp.float32
BF16 = jnp.bfloat16

D_MODEL = 1024
NORM_EPS = 1e-6
LOG2_E = 1.4426950408889634
N_MOD = 6
GRID_W = 64
ROPE_THETA = 10000.0

HEAD_DIM = 128
ATT_HEADS = 8
ATT_KV_HEADS = 2
ATT_GROUP = ATT_HEADS // ATT_KV_HEADS
KV_WIDTH = ATT_KV_HEADS * HEAD_DIM
HGRN_HEADS = 8

N_EXPERTS = 8
TOP_K = 2
MOE_BLOCK = 256

SUBLANES = 8
LANES = 128
VMEM_LIMIT_BYTES = 56 * 1024 * 1024

TOKEN_TILE = 256
INPROJ_TILE = 256
HGRN_CHUNK = 128
HGRN_STEP = 256
LOG_F_TERMS = 2
ATT_Q_TILE = 128
ATT_Q_STEP = 256
ATT_K_TILE = 512

_C_AQ = 0
_C_AK = _C_AQ + ATT_HEADS * HEAD_DIM
_C_AV = _C_AK + KV_WIDTH
_C_HQ = _C_AV + KV_WIDTH
_C_HFF = _C_HQ + D_MODEL
_C_HFB = _C_HFF + D_MODEL
_C_HI = _C_HFB + D_MODEL
_C_HG = _C_HI + D_MODEL
_C_GA = _C_HG + D_MODEL
_C_GH = _C_GA + D_MODEL
PROJ_WIDTH = _C_GH + D_MODEL


def _params(*sem):
    return pltpu.CompilerParams(dimension_semantics=sem, vmem_limit_bytes=VMEM_LIMIT_BYTES)


def _resident(shape, index_map):
    return pl.BlockSpec(shape, index_map, pipeline_mode=pl.Buffered(1))


def _rms(t, gain):
    return t * lax.rsqrt(jnp.mean(t * t, axis=-1, keepdims=True) + NORM_EPS) * gain


def _sigmoid(t):
    return 0.5 * jnp.tanh(0.5 * t) + 0.5


def _silu(t):
    return t * _sigmoid(t)


def _dot(a, b):
    return jnp.dot(a, b, preferred_element_type=F32)


def _dot_nt(a, b):
    return lax.dot_general(a, b, (((1,), (1,)), ((), ())), preferred_element_type=F32)


def _dot_tn(a, b):
    return lax.dot_general(a, b, (((0,), (0,)), ((), ())), preferred_element_type=F32)


def _mod_kernel(c_ref, w_ref, b_ref, o_ref):
    a = _silu(c_ref[...])
    o_ref[...] = jnp.dot(a, w_ref[...], preferred_element_type=F32,
                         precision=lax.Precision.HIGHEST) + b_ref[...]


def _mod_vectors(c_rows, w_mod, b_mod):
    depth, d, width = w_mod.shape
    rows = c_rows.shape[0]
    tn = 1536
    return pl.pallas_call(
        _mod_kernel,
        grid=(depth, width // tn),
        in_specs=[
            pl.BlockSpec((rows, d), lambda l, j: (0, 0)),
            pl.BlockSpec((None, d, tn), lambda l, j: (l, 0, j)),
            pl.BlockSpec((None, 1, tn), lambda l, j: (l, 0, j)),
        ],
        out_specs=pl.BlockSpec((None, rows, tn), lambda l, j: (l, 0, j)),
        out_shape=jax.ShapeDtypeStruct((depth, rows, width), F32),
        compiler_params=_params("arbitrary", "arbitrary"),
        name="mod_vectors",
    )(c_rows, w_mod, b_mod.reshape(depth, 1, width))


def _inproj_kernel(xl_ref, xc_ref, sh_ref, sc_ref, gain_ref, w_ref, cos_ref, sin_ref, qn_ref, kn_ref, lbl_ref,
                   aq_ref, ak_ref, av_ref, hq_ref, kf_ref, lf_ref, kb_ref, lb_ref, hi_ref, hg_ref,
                   ga_ref, gh_ref, *, layer, n_lat_tiles):
    x = jnp.where(pl.program_id(0) < n_lat_tiles, xl_ref[...], xc_ref[...])
    h = _rms(x, gain_ref[...])
    hb = (h * (1.0 + sc_ref[...]) + sh_ref[...]).astype(BF16)
    cos = cos_ref[...]
    sin = sin_ref[...]

    def proj(c0, width):
        return _dot(hb, w_ref[:, c0:c0 + width])

    def head(t, i):
        return t[:, i * HEAD_DIM:(i + 1) * HEAD_DIM]

    def norm_rope(t, gain):
        r = _rms(t, gain)
        return r * cos + pltpu.roll(r, HEAD_DIM // 2, 1) * sin

    qn = qn_ref[...]
    for c in range(2):
        t = proj(_C_AQ + c * 512, 512)
        for i in range(4):
            hh = c * 4 + i
            aq_ref[:, hh * HEAD_DIM:(hh + 1) * HEAD_DIM] = (
                norm_rope(head(t, i), qn) * (HEAD_DIM ** -0.5 * LOG2_E)).astype(BF16)
    t = proj(_C_AK, 2 * KV_WIDTH)
    kn = kn_ref[...]
    for i in range(ATT_KV_HEADS):
        ak_ref[:, i * HEAD_DIM:(i + 1) * HEAD_DIM] = norm_rope(head(t, i), kn).astype(BF16)
    av_ref[...] = t[:, KV_WIDTH:].astype(BF16)

    for c in range(2):
        t = proj(_C_HQ + c * 512, 512)
        for i in range(4):
            hq_ref[c * 4 + i] = _silu(head(t, i)).astype(BF16)

    lg = lbl_ref[...]
    depth = lg.shape[0]
    mx = lg[0:1]
    for j in range(1, depth):
        mx = jnp.maximum(mx, lg[j:j + 1])
    es = [jnp.exp(lg[j:j + 1] - mx) for j in range(depth)]
    tot = es[0]
    for j in range(1, depth):
        tot = tot + es[j]
    low = jnp.zeros_like(tot)
    for j in range(1, layer + 1):
        low = low + es[j] / tot

    for c0, k_ref, l_ref in ((_C_HFF, kf_ref, lf_ref), (_C_HFB, kb_ref, lb_ref)):
        for c in range(2):
            t = proj(c0 + c * 512, 512)
            for i in range(4):
                hh = c * 4 + i
                lo = low[:, hh * HEAD_DIM:(hh + 1) * HEAD_DIM]
                f = lo + (1.0 - lo) * jax.nn.sigmoid(head(t, i))
                k_ref[hh] = (1.0 - f).astype(BF16)
                rest = jnp.log(f) * LOG2_E
                terms = []
                for _ in range(LOG_F_TERMS):
                    terms.append(rest.astype(BF16))
                    rest = rest - terms[-1].astype(F32)
                l_ref[hh] = jnp.concatenate(terms, axis=1)

    for c in range(2):
        t = proj(_C_HG + c * 512, 512)
        for i in range(4):
            hg_ref[c * 4 + i] = _silu(head(t, i)).astype(BF16)

    for c in range(2):
        ga_ref[:, c * 512:(c + 1) * 512] = _sigmoid(proj(_C_GA + c * 512, 512)).astype(BF16)
    for c in range(2):
        gh_ref[:, c * 512:(c + 1) * 512] = _sigmoid(proj(_C_GH + c * 512, 512)).astype(BF16)

    for c in range(2):
        t = proj(_C_HI + c * 512, 512)
        for i in range(4):
            hi_ref[c * 4 + i] = head(t, i).astype(BF16)


def _in_projection(x_lat, x_ctx, ctx_row0, mod, gain, w_in, cos_tab, sin_tab, q_norm, k_norm, lb_logits, *,
                   layer, n_lat, t_rows, seq):
    d = x_lat.shape[1]
    tm = INPROJ_TILE
    assert n_lat % tm == 0 and t_rows % tm == 0 and seq % tm == 0 and ctx_row0 % tm == 0
    n_lat_tiles = n_lat // tm
    per_seq = seq // tm

    def mod_row(i):
        return jnp.where(i < n_lat_tiles, i // per_seq, mod.shape[0] - 1 - 3)

    def pos_blk(i):
        return jnp.where(i < n_lat_tiles, i % per_seq, per_seq)

    tok = lambda w: pl.BlockSpec((tm, w), lambda i: (i, 0))
    hm = pl.BlockSpec((HGRN_HEADS, tm, HEAD_DIM), lambda i: (0, i, 0))
    hm3 = pl.BlockSpec((HGRN_HEADS, tm, LOG_F_TERMS * HEAD_DIM), lambda i: (0, i, 0))
    hm_shape = lambda terms: jax.ShapeDtypeStruct((HGRN_HEADS, t_rows, terms * HEAD_DIM), BF16)
    tm_shape = lambda w: jax.ShapeDtypeStruct((t_rows, w), BF16)
    return pl.pallas_call(
        functools.partial(_inproj_kernel, layer=layer, n_lat_tiles=n_lat_tiles),
        grid=(t_rows // tm,),
        in_specs=[
            pl.BlockSpec((tm, d), lambda i: (jnp.minimum(i, n_lat_tiles - 1), 0)),
            pl.BlockSpec((tm, d), lambda i: (ctx_row0 // tm + jnp.maximum(i - n_lat_tiles, 0), 0)),
            pl.BlockSpec((None, 1, d), lambda i: (mod_row(i), 0, 0)),
            pl.BlockSpec((None, 1, d), lambda i: (mod_row(i), 0, 1)),
            pl.BlockSpec((1, d), lambda i: (0, 0)),
            _resident((d, PROJ_WIDTH), lambda i: (0, 0)),
            pl.BlockSpec((tm, HEAD_DIM), lambda i: (pos_blk(i), 0)),
            pl.BlockSpec((tm, HEAD_DIM), lambda i: (pos_blk(i), 0)),
            pl.BlockSpec((1, HEAD_DIM), lambda i: (0, 0)),
            pl.BlockSpec((1, HEAD_DIM), lambda i: (0, 0)),
            pl.BlockSpec(lb_logits.shape, lambda i: (0, 0)),
        ],
        out_specs=[tok(d), tok(KV_WIDTH), tok(KV_WIDTH), hm, hm, hm3, hm, hm3, hm, hm, tok(d), tok(d)],
        out_shape=[tm_shape(d), tm_shape(KV_WIDTH), tm_shape(KV_WIDTH), hm_shape(1), hm_shape(1),
                   hm_shape(LOG_F_TERMS), hm_shape(1), hm_shape(LOG_F_TERMS), hm_shape(1), hm_shape(1),
                   tm_shape(d), tm_shape(d)],
        compiler_params=_params("arbitrary"),
        name=f"in_projection_l{layer}",
    )(x_lat, x_ctx, mod, mod, gain, w_in, cos_tab, sin_tab, q_norm, k_norm, lb_logits)


def _hgrn_constants(n, c):
    t = lax.broadcasted_iota(jnp.int32, (n, n), 0)
    s = lax.broadcasted_iota(jnp.int32, (n, n), 1)
    tri = ((t >= s).astype(BF16), (t <= s).astype(BF16))
    t = lax.broadcasted_iota(jnp.int32, (c, c), 0)
    s = lax.broadcasted_iota(jnp.int32, (c, c), 1)
    levels = []
    blk = 2
    while blk <= c:
        half = blk // 2
        same = (t // blk) == (s // blk)
        t_hi = (t % blk) >= half
        s_hi = (s % blk) >= half
        levels.append((blk, (same & t_hi & ~s_hi, same & ~t_hi & s_hi)))
        blk *= 2
    return tri, t == s, levels


def _boundary_decay(cum, blk, rev):
    n = cum.shape[0]
    half = blk // 2
    off = half if rev else half - 1
    if blk >= 2 * SUBLANES:
        rows = [jnp.broadcast_to(cum[a0 + off:a0 + off + 1, :], (blk, LANES)) for a0 in range(0, n, blk)]
        mid = rows[0] if len(rows) == 1 else jnp.concatenate(rows, axis=0)
    else:
        c3 = cum.reshape(n // SUBLANES, SUBLANES, LANES)
        sub = lax.broadcasted_iota(jnp.int32, c3.shape, 1)
        mid = jnp.broadcast_to(c3[:, off:off + 1, :], c3.shape)
        for a0 in range(blk, SUBLANES, blk):
            mid = jnp.where(sub >= a0, jnp.broadcast_to(c3[:, a0 + off:a0 + off + 1, :], c3.shape), mid)
        mid = mid.reshape(n, LANES)
    x = lax.bitcast_convert_type(cum - mid, jnp.uint32) | jnp.uint32(0x80000000)
    return jnp.exp2(lax.bitcast_convert_type(x, F32))


def _hgrn_block(q, k, v, g3, st_t, consts, chunk, rev):
    tri, eye, levels = consts
    n = q.shape[0]
    parts = _dot(tri[rev], g3)
    cum = parts[:, :LANES]
    for i in range(1, g3.shape[1] // LANES):
        cum = cum + parts[:, i * LANES:(i + 1) * LANES]
    tail = cum[0:1] if rev else cum[n - 1:n]

    qf = q.astype(F32)
    kf = k.astype(F32)
    scaled = lambda t, e: (t * e).astype(BF16)
    out = _dot_nt(scaled(qf, jnp.exp2(cum)), st_t.astype(BF16))
    k_end = scaled(kf, jnp.exp2(tail - cum))
    st_new = st_t * jnp.exp2(tail) + _dot_tn(v, k_end)

    blk = n
    far = []
    while blk > chunk:
        half = blk // 2
        e = _boundary_decay(cum, blk, rev)
        qe = scaled(qf, e)
        ke = scaled(kf, e)
        for a0 in range(0, n, blk):
            qr = a0 if rev else a0 + half
            kr = a0 + half if rev else a0
            sc = _dot_nt(qe[qr:qr + half], ke[kr:kr + half])
            far.append((qr, half, _dot(sc.astype(BF16), v[kr:kr + half])))
        blk //= 2

    sides = [(q, k)]
    for blk, _ in levels:
        if blk == 2:
            sides.append((scaled(qf, jnp.exp2(g3[:, :LANES].astype(F32))), k))
        else:
            e = _boundary_decay(cum, blk, rev)
            sides.append((scaled(qf, e), scaled(kf, e)))
    masks = [eye] + [m[rev] for _, m in levels]
    outs = []
    for c0 in range(0, n, chunk):
        a = jnp.zeros((chunk, chunk), F32)
        rows = slice(c0, c0 + chunk)
        zero = jnp.zeros((chunk, LANES), BF16)
        for i in range(0, len(sides) - 1, 2):
            (qa, ka), (qb, kb) = sides[i], sides[i + 1]
            keys = jnp.concatenate([jnp.concatenate([ka[rows], zero], axis=1),
                                    jnp.concatenate([zero, kb[rows]], axis=1)], axis=0)
            sc = _dot_nt(jnp.concatenate([qa[rows], qb[rows]], axis=1), keys)
            a = jnp.where(masks[i], sc[:, :chunk], a)
            a = jnp.where(masks[i + 1], sc[:, chunk:], a)
        if len(sides) % 2:
            a = jnp.where(masks[-1], _dot_nt(sides[-1][0][rows], sides[-1][1][rows]), a)
        o = out[c0:c0 + chunk] + _dot(a.astype(BF16), v[c0:c0 + chunk])
        for r0, rows, val in far:
            if r0 <= c0 < r0 + rows:
                o = o + val[c0 - r0:c0 - r0 + chunk]
        outs.append(o)
    return (outs[0] if len(outs) == 1 else jnp.concatenate(outs, axis=0)), st_new


def _softmax_tile(q, carry, k, v):
    m, acc = carry
    ones_col = (lax.broadcasted_iota(jnp.int32, v.shape, 1) == 0).astype(BF16)
    s = _dot_nt(q, k)
    m_new = jnp.maximum(m, jnp.max(s, axis=-1, keepdims=True))
    p = jnp.exp2((s - m_new).astype(BF16))
    acc = jnp.exp2(m - m_new) * acc + _dot(p, jnp.concatenate([v, ones_col], axis=1))
    return m_new, acc


def _softmax_init(rows):
    return jnp.full((rows, 1), -1e30, F32), jnp.zeros((rows, 2 * HEAD_DIM), F32)


def _softmax_result(carry):
    acc = carry[1]
    return acc[:, :HEAD_DIM] / acc[:, HEAD_DIM:HEAD_DIM + 1]


def _hgrn_kernel(qf_ref, vf_ref, kf_ref, lf_ref, qb_ref, vb_ref, kb_ref, lb_ref, of_ref, ob_ref,
                 st_ref, *, chunk):
    @pl.when(pl.program_id(1) == 0)
    def _():
        st_ref[...] = jnp.zeros(st_ref.shape, F32)

    consts = _hgrn_constants(qf_ref.shape[1], chunk)

    def head_body(h, carry):
        o, st = _hgrn_block(qf_ref[h], kf_ref[h], vf_ref[h], lf_ref[h], st_ref[0, h], consts, chunk, 0)
        of_ref[h] = o
        st_ref[0, h] = st
        o, st = _hgrn_block(qb_ref[h], kb_ref[h], vb_ref[h], lb_ref[h], st_ref[1, h], consts, chunk, 1)
        ob_ref[h] = o
        st_ref[1, h] = st
        return carry

    lax.fori_loop(0, qf_ref.shape[0], head_body, 0, unroll=2)


def _hgrn_scan(hq, hi, kf, lf, kb, lb, *, batch, seq, ctx_len):
    heads, t_rows, dh = hq.shape
    ts = HGRN_STEP
    n_ctx = ctx_len // ts
    n_seq = seq // ts
    ctx0 = batch * n_seq

    def fwd(b, s):
        return jnp.where(s < n_ctx, ctx0 + b * n_ctx + s, b * n_seq + s - n_ctx)

    def bwd(b, s):
        return jnp.where(s < n_ctx, ctx0 + b * n_ctx + n_ctx - 1 - s, b * n_seq + n_seq - 1 - (s - n_ctx))

    spec_f = pl.BlockSpec((heads, ts, dh), lambda b, s: (0, fwd(b, s), 0))
    spec_b = pl.BlockSpec((heads, ts, dh), lambda b, s: (0, bwd(b, s), 0))
    log_f = pl.BlockSpec((heads, ts, lf.shape[2]), lambda b, s: (0, fwd(b, s), 0))
    log_b = pl.BlockSpec((heads, ts, lb.shape[2]), lambda b, s: (0, bwd(b, s), 0))
    out = jax.ShapeDtypeStruct((heads, t_rows, dh), F32)
    return pl.pallas_call(
        functools.partial(_hgrn_kernel, chunk=HGRN_CHUNK),
        grid=(batch, n_ctx + n_seq),
        in_specs=[spec_f, spec_f, spec_f, log_f, spec_b, spec_b, spec_b, log_b],
        out_specs=[spec_f, spec_b],
        out_shape=[out, out],
        scratch_shapes=[pltpu.VMEM((2, heads, dh, dh), F32)],
        compiler_params=_params("arbitrary", "arbitrary"),
        name="hgrn_scan",
    )(hq, hi, kf, lf, hq, hi, kb, lb)


def _attn_kernel(q_ref, kc_ref, vc_ref, kl_ref, vl_ref, o_ref, *, n_lat_q_steps, k_tile):
    tq = ATT_Q_TILE
    is_latent = pl.program_id(2) < n_lat_q_steps

    def query_tiles(with_latent_keys):
        for r0 in range(0, q_ref.shape[0], tq):
            q = jnp.concatenate([q_ref[r0:r0 + tq, i * HEAD_DIM:(i + 1) * HEAD_DIM] for i in range(ATT_GROUP)],
                                axis=0)
            carry = _softmax_tile(q, _softmax_init(ATT_GROUP * tq), kc_ref[...], vc_ref[...])
            if with_latent_keys:
                for t in range(0, kl_ref.shape[0], k_tile):
                    carry = _softmax_tile(q, carry, kl_ref[t:t + k_tile, :], vl_ref[t:t + k_tile, :])
            out = _softmax_result(carry)
            for i in range(ATT_GROUP):
                o_ref[r0:r0 + tq, i * HEAD_DIM:(i + 1) * HEAD_DIM] = out[i * tq:(i + 1) * tq].astype(BF16)

    @pl.when(is_latent)
    def _():
        query_tiles(True)

    @pl.when(jnp.logical_not(is_latent))
    def _():
        query_tiles(False)


def _attention(aq, ak, av, *, batch, seq, ctx_len, with_ctx_queries):
    t_rows = aq.shape[0]
    tq = ATT_Q_STEP
    nq_lat = seq // tq
    nq_ctx = ctx_len // tq
    ctx_q0 = batch * nq_lat
    ctx_k0 = batch * seq // ctx_len
    gw = ATT_GROUP * HEAD_DIM

    def q_blk(b, i):
        return jnp.where(i < nq_lat, b * nq_lat + i, ctx_q0 + b * nq_ctx + i - nq_lat)

    q_spec = pl.BlockSpec((tq, gw), lambda b, g, i: (q_blk(b, i), g))
    kc_spec = pl.BlockSpec((ctx_len, HEAD_DIM), lambda b, g, i: (ctx_k0 + b, g))
    kl_spec = pl.BlockSpec((seq, HEAD_DIM), lambda b, g, i: (b, g))
    return pl.pallas_call(
        functools.partial(_attn_kernel, n_lat_q_steps=nq_lat, k_tile=ATT_K_TILE),
        grid=(batch, ATT_KV_HEADS, nq_lat + (nq_ctx if with_ctx_queries else 0)),
        in_specs=[q_spec, kc_spec, kc_spec, kl_spec, kl_spec],
        out_specs=q_spec,
        out_shape=jax.ShapeDtypeStruct((t_rows if with_ctx_queries else batch * seq, ATT_HEADS * HEAD_DIM),
                                       BF16),
        compiler_params=_params("arbitrary", "arbitrary", "arbitrary"),
        name="gqa_attention",
    )(aq, ak, av, ak, av)


def _merge_kernel(xl_ref, xc_ref, att_ref, of_ref, ob_ref, hg_ref, ga_ref, gh_ref, g1_ref, hgn_ref, post_ref,
                  wa_ref, wh_ref, wo_ref, o_ref, hg_scr, *, n_lat_tiles):
    hgn = hgn_ref[...]
    for i in range(HGRN_HEADS):
        o = _rms(of_ref[i] + ob_ref[i], hgn)
        hg_scr[:, i * HEAD_DIM:(i + 1) * HEAD_DIM] = (o * hg_ref[i].astype(F32)).astype(BF16)
    y = (ga_ref[...].astype(F32) * _dot(att_ref[...], wa_ref[...])
         + gh_ref[...].astype(F32) * _dot(hg_scr[...], wh_ref[...]))
    z = _rms(_dot(y.astype(BF16), wo_ref[...]), post_ref[...])
    x = jnp.where(pl.program_id(0) < n_lat_tiles, xl_ref[...], xc_ref[...])
    o_ref[...] = x + g1_ref[...] * z


def _merge(x_lat, x_ctx, ctx_row0, att, o_f, o_b, hg, ga, gh, mod, hg_norm, post_norm, w_att, w_hg, w_out, *,
           rows, n_lat, seq):
    d = x_lat.shape[1]
    tm = TOKEN_TILE
    n_lat_tiles = n_lat // tm
    per_seq = seq // tm

    def mod_row(i):
        return jnp.where(i < n_lat_tiles, i // per_seq, mod.shape[0] - 1 - 3)

    tok = pl.BlockSpec((tm, d), lambda i: (i, 0))
    hm = pl.BlockSpec((HGRN_HEADS, tm, HEAD_DIM), lambda i: (0, i, 0))
    vec = lambda w: pl.BlockSpec((1, w), lambda i: (0, 0))
    wgt = _resident((d, d), lambda i: (0, 0))
    return pl.pallas_call(
        functools.partial(_merge_kernel, n_lat_tiles=n_lat_tiles),
        grid=(rows // tm,),
        in_specs=[pl.BlockSpec((tm, d), lambda i: (jnp.minimum(i, n_lat_tiles - 1), 0)),
                  pl.BlockSpec((tm, d), lambda i: (ctx_row0 // tm + jnp.maximum(i - n_lat_tiles, 0), 0)),
                  tok, hm, hm, hm, tok, tok,
                  pl.BlockSpec((None, 1, d), lambda i: (mod_row(i), 0, 2)),
                  vec(HEAD_DIM), vec(d), wgt, wgt, wgt],
        out_specs=tok,
        out_shape=jax.ShapeDtypeStruct((rows, d), F32),
        scratch_shapes=[pltpu.VMEM((tm, d), BF16)],
        compiler_params=_params("arbitrary"),
        name="branch_merge",
    )(x_lat, x_ctx, att, o_f, o_b, hg, ga, gh, mod, hg_norm, post_norm, w_att, w_hg, w_out)


def _ffn_kernel(x_ref, sh_ref, sc_ref, g2_ref, pre_ref, post_ref, wg_ref, wu_ref, wd_ref, o_ref, a_scr,
                *, ff_tile):
    x = x_ref[...]
    f = (_rms(x, pre_ref[...]) * (1.0 + sc_ref[...]) + sh_ref[...]).astype(BF16)
    for c in range(0, wg_ref.shape[1], ff_tile):
        g = _dot(f, wg_ref[:, c:c + ff_tile])
        u = _dot(f, wu_ref[:, c:c + ff_tile])
        a_scr[:, c:c + ff_tile] = (_silu(g) * u).astype(BF16)
    y = _rms(_dot(a_scr[...], wd_ref[...]), post_ref[...])
    o_ref[...] = x + g2_ref[...] * y


def _dense_ffn(x, mod, pre_norm, post_norm, w_gate, w_up, w_down, *, n_lat, seq):
    rows, d = x.shape
    d_ff = w_gate.shape[1]
    tm = TOKEN_TILE
    n_lat_tiles = n_lat // tm
    per_seq = seq // tm

    def mod_row(i):
        return jnp.where(i < n_lat_tiles, i // per_seq, mod.shape[0] - 1 - 3)

    tok = pl.BlockSpec((tm, d), lambda i: (i, 0))
    vec = pl.BlockSpec((1, d), lambda i: (0, 0))
    mod_spec = lambda col: pl.BlockSpec((None, 1, d), lambda i: (mod_row(i), 0, col))
    return pl.pallas_call(
        functools.partial(_ffn_kernel, ff_tile=256),
        grid=(rows // tm,),
        in_specs=[tok, mod_spec(3), mod_spec(4), mod_spec(5), vec, vec,
                  _resident((d, d_ff), lambda i: (0, 0)), _resident((d, d_ff), lambda i: (0, 0)),
                  _resident((d_ff, d), lambda i: (0, 0))],
        out_specs=tok,
        out_shape=jax.ShapeDtypeStruct((rows, d), F32),
        scratch_shapes=[pltpu.VMEM((tm, d_ff), BF16)],
        compiler_params=_params("arbitrary"),
        name="dense_swiglu",
    )(x, mod, mod, mod, pre_norm, post_norm, w_gate, w_up, w_down)


def _router_kernel(x_ref, sh_ref, sc_ref, pre_ref, wr_ref, f_ref, route_ref):
    f = _rms(x_ref[...], pre_ref[...]) * (1.0 + sc_ref[...]) + sh_ref[...]
    f_ref[...] = f
    lane = lax.broadcasted_iota(jnp.int32, (f.shape[0], LANES), 1)
    neg = jnp.float32(-jnp.inf)
    logits = jnp.full((f.shape[0], LANES), neg, F32)
    for e in range(wr_ref.shape[0]):
        logits = jnp.where(lane == e, jnp.sum(f * wr_ref[e:e + 1, :], axis=-1, keepdims=True), logits)
    m1 = jnp.max(logits, axis=-1, keepdims=True)
    i1 = jnp.min(jnp.where(logits == m1, lane, LANES), axis=-1, keepdims=True)
    rest = jnp.where(lane == i1, neg, logits)
    m2 = jnp.max(rest, axis=-1, keepdims=True)
    i2 = jnp.min(jnp.where(rest == m2, lane, LANES), axis=-1, keepdims=True)
    w1 = 1.0 / (1.0 + jnp.exp(m2 - m1))
    w2 = 1.0 - w1
    route = jnp.where(lane == 0, i1.astype(F32),
                      jnp.where(lane == 1, i2.astype(F32),
                                jnp.where(lane == 2, w1, jnp.where(lane == 3, w2, 0.0))))
    route_ref[...] = route


def _router(x, mod, pre_norm, w_router_t, *, seq):
    rows, d = x.shape
    tm = TOKEN_TILE
    per_seq = seq // tm
    tok = pl.BlockSpec((tm, d), lambda i: (i, 0))
    mod_spec = lambda col: pl.BlockSpec((None, 1, d), lambda i: (i // per_seq, 0, col))
    return pl.pallas_call(
        _router_kernel,
        grid=(rows // tm,),
        in_specs=[tok, mod_spec(3), mod_spec(4), pl.BlockSpec((1, d), lambda i: (0, 0)),
                  pl.BlockSpec(w_router_t.shape, lambda i: (0, 0))],
        out_specs=[tok, pl.BlockSpec((tm, LANES), lambda i: (i, 0))],
        out_shape=[jax.ShapeDtypeStruct((rows, d), F32), jax.ShapeDtypeStruct((rows, LANES), F32)],
        compiler_params=_params("arbitrary"),
        name="moe_router",
    )(x, mod, mod, pre_norm, w_router_t)


def _expert_kernel(be_ref, tok_ref, dst_ref, f_hbm, wg_ref, wu_ref, wd_ref, ysel_hbm,
                   xbuf, ybuf, a_scr, gsem, ssem, *, ff_tile, n_blocks, spare_row0):
    del be_ref
    s = pl.program_id(0)
    rows = xbuf.shape[1]

    def gather_row(r, slot):
        pltpu.make_async_copy(f_hbm.at[pl.ds(tok_ref[r], 1)], xbuf.at[slot, pl.ds(r, 1)],
                              gsem.at[slot]).start()

    def scatter_row(r, slot, dst_row):
        pltpu.make_async_copy(ybuf.at[slot, pl.ds(r, 1)], ysel_hbm.at[pl.ds(dst_row, 1)],
                              ssem.at[slot]).start()

    def wait_gather(slot):
        pltpu.make_async_copy(f_hbm.at[pl.ds(0, rows)], xbuf.at[slot], gsem.at[slot]).wait()

    def wait_scatter(slot):
        pltpu.make_async_copy(ybuf.at[slot], ysel_hbm.at[pl.ds(0, rows)], ssem.at[slot]).wait()

    @pl.when(s == 0)
    def _():
        ybuf[...] = jnp.zeros(ybuf.shape, F32)

        def body(r, carry):
            gather_row(r, 0)
            scatter_row(r, 0, spare_row0 + r)
            return carry
        lax.fori_loop(0, rows, body, 0)

    @pl.when((s >= 1) & (s <= n_blocks))
    def _():
        slot = (s - 1) % 2
        wait_gather(slot)
        wait_scatter(slot)
        xb = xbuf[slot].astype(BF16)
        chunks = list(range(0, wg_ref.shape[1], ff_tile))
        per = -(-rows // (len(chunks) + 1))

        def move_rows(j):
            for r in range(j * per, min((j + 1) * per, rows)):
                gather_row(r, 1 - slot)
                scatter_row(r, 1 - slot, dst_ref[r])

        for j, c in enumerate(chunks):
            move_rows(j)
            g = _dot(xb, wg_ref[:, c:c + ff_tile])
            u = _dot(xb, wu_ref[:, c:c + ff_tile])
            a_scr[:, c:c + ff_tile] = (_silu(g) * u).astype(BF16)
        move_rows(len(chunks))
        ybuf[slot] = _dot(a_scr[...], wd_ref[...])

    @pl.when(s == n_blocks + 1)
    def _():
        slot = (n_blocks - 1) % 2

        def body(r, carry):
            scatter_row(r, slot, dst_ref[r])
            return carry
        lax.fori_loop(0, rows, body, 0)
        wait_scatter(1 - slot)
        wait_scatter(slot)
        wait_gather(1 - slot)


def _expert_ffn(f, row_tok, row_dst_ext, block_expert, w_gate, w_up, w_down, *, n_out_rows, spare_row0):
    n_rows = row_tok.shape[0]
    d = f.shape[1]
    d_ff = w_gate.shape[2]
    n_blocks = n_rows // MOE_BLOCK
    idx = lambda fn: pl.BlockSpec((MOE_BLOCK,), fn, memory_space=pltpu.SMEM)
    wgt = lambda shape: pl.BlockSpec((None,) + shape,
                                     lambda s, be: (be[jnp.clip(s - 1, 0, n_blocks - 1)], 0, 0),
                                     pipeline_mode=pl.Buffered(1))
    grid_spec = pltpu.PrefetchScalarGridSpec(
        num_scalar_prefetch=1,
        grid=(n_blocks + 2,),
        in_specs=[idx(lambda s, be: (jnp.minimum(s, n_blocks - 1),)),
                  idx(lambda s, be: (jnp.maximum(s - 1, 0),)),
                  pl.BlockSpec(memory_space=pl.ANY),
                  wgt((d, d_ff)), wgt((d, d_ff)), wgt((d_ff, d))],
        out_specs=pl.BlockSpec(memory_space=pl.ANY),
        scratch_shapes=[pltpu.VMEM((2, MOE_BLOCK, d), F32), pltpu.VMEM((2, MOE_BLOCK, d), F32),
                        pltpu.VMEM((MOE_BLOCK, d_ff), BF16),
                        pltpu.SemaphoreType.DMA((2,)), pltpu.SemaphoreType.DMA((2,))],
    )
    return pl.pallas_call(
        functools.partial(_expert_kernel, ff_tile=512, n_blocks=n_blocks, spare_row0=spare_row0),
        grid_spec=grid_spec,
        out_shape=jax.ShapeDtypeStruct((n_out_rows, d), F32),
        compiler_params=_params("arbitrary"),
        name="moe_expert_swiglu",
    )(block_expert, row_tok, row_dst_ext, f, w_gate, w_up, w_down)


def _combine_kernel(y0_ref, y1_ref, x_ref, route_ref, g2_ref, post_ref, o_ref):
    route = route_ref[...]
    y = route[:, 2:3] * y0_ref[...] + route[:, 3:4] * y1_ref[...]
    o_ref[...] = x_ref[...] + g2_ref[...] * _rms(y, post_ref[...])


def _combine(y_sel, x, route, mod, post_norm, *, seq, choice_stride):
    rows, d = x.shape
    tm = TOKEN_TILE
    per_seq = seq // tm
    tok = pl.BlockSpec((tm, d), lambda i: (i, 0))
    return pl.pallas_call(
        _combine_kernel,
        grid=(rows // tm,),
        in_specs=[tok, pl.BlockSpec((tm, d), lambda i: (choice_stride // tm + i, 0)), tok,
                  pl.BlockSpec((tm, LANES), lambda i: (i, 0)),
                  pl.BlockSpec((None, 1, d), lambda i: (i // per_seq, 0, 5)),
                  pl.BlockSpec((1, d), lambda i: (0, 0))],
        out_specs=tok,
        out_shape=jax.ShapeDtypeStruct((rows, d), F32),
        compiler_params=_params("arbitrary"),
        name="moe_combine",
    )(y_sel, y_sel, x, route, mod, post_norm)


def _moe_ffn(x, mod, pre_norm, post_norm, w_router, w_gate, w_up, w_down, *, seq):
    n_tok, d = x.shape
    n_assign = n_tok * TOP_K
    n_blocks = -(-n_assign // MOE_BLOCK) + N_EXPERTS
    n_rows = n_blocks * MOE_BLOCK
    n_pad = n_rows - n_assign
    assert TOP_K == 2 and n_pad % (TOP_K * TOKEN_TILE) == 0
    f, route = _router(x, mod, pre_norm, w_router.T, seq=seq)

    flat_e = route[:, :TOP_K].astype(jnp.int32).reshape(-1)
    onehot = (flat_e[:, None] == jnp.arange(N_EXPERTS, dtype=jnp.int32)[None, :]).astype(jnp.int32)
    csum = jnp.cumsum(onehot, axis=0)
    counts = csum[-1]
    rank = jnp.sum((csum - onehot) * onehot, axis=1)
    padded = (counts + MOE_BLOCK - 1) // MOE_BLOCK * MOE_BLOCK
    pend = jnp.cumsum(padded)
    dest = ((pend - padded)[flat_e] + rank).astype(jnp.int32)
    block_expert = jnp.minimum(
        jnp.searchsorted(pend, jnp.arange(n_blocks, dtype=jnp.int32) * MOE_BLOCK, side='right'),
        N_EXPERTS - 1).astype(jnp.int32)
    choice_stride = n_tok + n_pad // TOP_K
    row_assign = jnp.full((n_rows,), -1, jnp.int32).at[dest].set(jnp.arange(n_assign, dtype=jnp.int32),
                                                                 unique_indices=True)
    is_pad = row_assign < 0
    pad_id = jnp.cumsum(is_pad.astype(jnp.int32)) - 1
    row_tok = jnp.where(is_pad, 0, row_assign // TOP_K)
    row_dst = jnp.where(is_pad, (pad_id % TOP_K) * choice_stride + n_tok + pad_id // TOP_K,
                        (row_assign % TOP_K) * choice_stride + row_assign // TOP_K)
    row_dst_ext = jnp.concatenate([n_rows + jnp.arange(MOE_BLOCK, dtype=jnp.int32), row_dst])

    y_sel = _expert_ffn(f, row_tok, row_dst_ext, block_expert, w_gate, w_up, w_down,
                        n_out_rows=n_rows + 2 * MOE_BLOCK, spare_row0=n_rows + MOE_BLOCK)
    return _combine(y_sel, x, route, mod, post_norm, seq=seq, choice_stride=choice_stride)


def _rope_tables(seq, pad_rows):
    pos = jnp.arange(seq)
    nfreq = HEAD_DIM // 4
    inv_freq = ROPE_THETA ** (-jnp.arange(nfreq, dtype=F32) / nfreq)
    ang = jnp.concatenate([(pos // GRID_W).astype(F32)[:, None] * inv_freq,
                           (pos % GRID_W).astype(F32)[:, None] * inv_freq], axis=-1)
    ang = jnp.concatenate([ang, ang], axis=-1)
    sign = jnp.where(jnp.arange(HEAD_DIM) < HEAD_DIM // 2, -1.0, 1.0).astype(F32)
    cos = jnp.concatenate([jnp.cos(ang), jnp.ones((pad_rows, HEAD_DIM), F32)], axis=0)
    sin = jnp.concatenate([jnp.sin(ang) * sign, jnp.zeros((pad_rows, HEAD_DIM), F32)], axis=0)
    return cos, sin


def kernel(x, c, ctx, c_ctx, w_mod, b_mod, pre_mix_norm, post_mix_norm, pre_ffn_norm, post_ffn_norm, w_in, q_norm, k_norm, hg_norm, hg_lb_logits, w_att_branch, w_hg_branch, w_out, ffn_w_gate, ffn_w_up, ffn_w_down, moe_router, moe_w_gate, moe_w_up, moe_w_down):
    batch, seq, d = x.shape
    ctx_len = ctx.shape[1]
    depth = w_mod.shape[0]
    n_lat = batch * seq
    assert d == D_MODEL and w_in.shape[2] == PROJ_WIDTH
    assert seq % HGRN_STEP == 0 and ctx_len % HGRN_STEP == 0 and n_lat % TOKEN_TILE == 0
    assert seq % ctx_len == 0 and ctx_len % TOKEN_TILE == 0 and seq % ATT_K_TILE == 0

    c_rows = jnp.concatenate([c, c_ctx[None, :], jnp.zeros((3, d), F32)], axis=0)
    mod_all = _mod_vectors(c_rows, w_mod, b_mod)
    cos_tab, sin_tab = _rope_tables(seq, INPROJ_TILE)
    row = lambda v: v.reshape(1, -1)

    t_rows = n_lat + batch * ctx_len
    x_lat, x_ctx, ctx_row0 = x.reshape(n_lat, d), ctx.reshape(batch * ctx_len, d), 0
    for layer in range(depth):
        last = layer == depth - 1
        mod = mod_all[layer].reshape(batch + 4, 1, N_MOD * d)
        (aq, ak, av, hq, kf, lf, kb, lb, hi, hg, ga, gh) = _in_projection(
            x_lat, x_ctx, ctx_row0, mod, row(pre_mix_norm[layer]), w_in[layer].astype(BF16), cos_tab, sin_tab,
            row(q_norm[layer]), row(k_norm[layer]), hg_lb_logits, layer=layer, n_lat=n_lat, t_rows=t_rows,
            seq=seq)
        o_f, o_b = _hgrn_scan(hq, hi, kf, lf, kb, lb, batch=batch, seq=seq, ctx_len=ctx_len)
        att = _attention(aq, ak, av, batch=batch, seq=seq, ctx_len=ctx_len, with_ctx_queries=not last)
        rows = n_lat if last else t_rows
        xt = _merge(x_lat, x_ctx, ctx_row0, att, o_f, o_b, hg, ga, gh, mod, row(hg_norm[layer]), row(post_mix_norm[layer]),
                    w_att_branch[layer].astype(BF16), w_hg_branch[layer].astype(BF16),
                    w_out[layer].astype(BF16), rows=rows, n_lat=n_lat, seq=seq)
        idx = layer // 2
        if layer % 2 == 0:
            xt = _dense_ffn(xt, mod, row(pre_ffn_norm[layer]), row(post_ffn_norm[layer]),
                            ffn_w_gate[idx].astype(BF16), ffn_w_up[idx].astype(BF16),
                            ffn_w_down[idx].astype(BF16), n_lat=n_lat, seq=seq)
            x_lat, x_ctx, ctx_row0 = xt, xt, n_lat
        else:
            assert last, "the expert mixer is implemented for latent tokens only"
            xt = _moe_ffn(xt, mod, row(pre_ffn_norm[layer]), row(post_ffn_norm[layer]), moe_router[idx],
                          moe_w_gate[idx].astype(BF16), moe_w_up[idx].astype(BF16),
                          moe_w_down[idx].astype(BF16), seq=seq)
    return xt[:n_lat].reshape(batch, seq, d)
```

```python
import functools

import jax
import jax.numpy as jnp
from jax import lax
from jax.experimental import pallas as pl
from jax.experimental.pallas import tpu as pltpu

F32 = jnp.float32
BF16 = jnp.bfloat16

D_MODEL = 1024
NORM_EPS = 1e-6
LOG2_E = 1.4426950408889634
N_MOD = 6
GRID_W = 64
ROPE_THETA = 10000.0

HEAD_DIM = 128
ATT_HEADS = 8
ATT_KV_HEADS = 2
ATT_GROUP = ATT_HEADS // ATT_KV_HEADS
KV_WIDTH = ATT_KV_HEADS * HEAD_DIM
HGRN_HEADS = 8

N_EXPERTS = 8
TOP_K = 2
MOE_BLOCK = 256

SUBLANES = 8
LANES = 128
VMEM_LIMIT_BYTES = 56 * 1024 * 1024

TOKEN_TILE = 256
INPROJ_TILE = 256
HGRN_CHUNK = 128
HGRN_STEP = 256
HGRN_STAGGER = 2
LOG_F_TERMS = 2
ATT_Q_TILE = 128
ATT_Q_STEP = 256
ATT_K_TILE = 512

_C_AQ = 0
_C_AK = _C_AQ + ATT_HEADS * HEAD_DIM
_C_AV = _C_AK + KV_WIDTH
_C_HQ = _C_AV + KV_WIDTH
_C_HFF = _C_HQ + D_MODEL
_C_HFB = _C_HFF + D_MODEL
_C_HI = _C_HFB + D_MODEL
_C_HG = _C_HI + D_MODEL
_C_GA = _C_HG + D_MODEL
_C_GH = _C_GA + D_MODEL
PROJ_WIDTH = _C_GH + D_MODEL


def _params(*sem):
    return pltpu.CompilerParams(dimension_semantics=sem, vmem_limit_bytes=VMEM_LIMIT_BYTES)


def _resident(shape, index_map):
    return pl.BlockSpec(shape, index_map, pipeline_mode=pl.Buffered(1))


def _rms(t, gain):
    return t * lax.rsqrt(jnp.mean(t * t, axis=-1, keepdims=True) + NORM_EPS) * gain


def _sigmoid(t):
    return 0.5 * jnp.tanh(0.5 * t) + 0.5


def _silu(t):
    return t * _sigmoid(t)


def _dot(a, b):
    return jnp.dot(a, b, preferred_element_type=F32)


def _dot_nt(a, b):
    return lax.dot_general(a, b, (((1,), (1,)), ((), ())), preferred_element_type=F32)


def _dot_tn(a, b):
    return lax.dot_general(a, b, (((0,), (0,)), ((), ())), preferred_element_type=F32)


def _mod_kernel(c_ref, w_ref, b_ref, o_ref):
    a = _silu(c_ref[...])
    o_ref[...] = jnp.dot(a, w_ref[...], preferred_element_type=F32,
                         precision=lax.Precision.HIGHEST) + b_ref[...]


def _mod_vectors(c_rows, w_mod, b_mod):
    depth, d, width = w_mod.shape
    rows = c_rows.shape[0]
    tn = 1536
    return pl.pallas_call(
        _mod_kernel,
        grid=(depth, width // tn),
        in_specs=[
            pl.BlockSpec((rows, d), lambda l, j: (0, 0)),
            pl.BlockSpec((None, d, tn), lambda l, j: (l, 0, j)),
            pl.BlockSpec((None, 1, tn), lambda l, j: (l, 0, j)),
        ],
        out_specs=pl.BlockSpec((None, rows, tn), lambda l, j: (l, 0, j)),
        out_shape=jax.ShapeDtypeStruct((depth, rows, width), F32),
        compiler_params=_params("arbitrary", "arbitrary"),
        name="mod_vectors",
    )(c_rows, w_mod, b_mod.reshape(depth, 1, width))


def _inproj_kernel(xl_ref, xc_ref, sh_ref, sc_ref, gain_ref, w_ref, cos_ref, sin_ref, qn_ref, kn_ref, lbl_ref,
                   aq_ref, ak_ref, av_ref, hq_ref, kf_ref, lf_ref, kb_ref, lb_ref, hi_ref, hg_ref,
                   ga_ref, gh_ref, *, layer, n_lat_tiles):
    x = jnp.where(pl.program_id(0) < n_lat_tiles, xl_ref[...], xc_ref[...])
    h = _rms(x, gain_ref[...])
    hb = (h * (1.0 + sc_ref[...]) + sh_ref[...]).astype(BF16)
    cos = cos_ref[...]
    sin = sin_ref[...]

    def proj(c0, width):
        return _dot(hb, w_ref[:, c0:c0 + width])

    def head(t, i):
        return t[:, i * HEAD_DIM:(i + 1) * HEAD_DIM]

    def norm_rope(t, gain):
        r = _rms(t, gain)
        return r * cos + pltpu.roll(r, HEAD_DIM // 2, 1) * sin

    qn = qn_ref[...]
    for c in range(2):
        t = proj(_C_AQ + c * 512, 512)
        for i in range(4):
            hh = c * 4 + i
            aq_ref[:, hh * HEAD_DIM:(hh + 1) * HEAD_DIM] = (
                norm_rope(head(t, i), qn) * (HEAD_DIM ** -0.5 * LOG2_E)).astype(BF16)
    t = proj(_C_AK, 2 * KV_WIDTH)
    kn = kn_ref[...]
    for i in range(ATT_KV_HEADS):
        ak_ref[:, i * HEAD_DIM:(i + 1) * HEAD_DIM] = norm_rope(head(t, i), kn).astype(BF16)
    av_ref[...] = t[:, KV_WIDTH:].astype(BF16)

    for c in range(2):
        t = proj(_C_HQ + c * 512, 512)
        for i in range(4):
            hq_ref[c * 4 + i] = _silu(head(t, i)).astype(BF16)

    lg = lbl_ref[...]
    depth = lg.shape[0]
    mx = lg[0:1]
    for j in range(1, depth):
        mx = jnp.maximum(mx, lg[j:j + 1])
    es = [jnp.exp(lg[j:j + 1] - mx) for j in range(depth)]
    tot = es[0]
    for j in range(1, depth):
        tot = tot + es[j]
    low = jnp.zeros_like(tot)
    for j in range(1, layer + 1):
        low = low + es[j] / tot

    for c0, k_ref, l_ref in ((_C_HFF, kf_ref, lf_ref), (_C_HFB, kb_ref, lb_ref)):
        for c in range(2):
            t = proj(c0 + c * 512, 512)
            for i in range(4):
                hh = c * 4 + i
                lo = low[:, hh * HEAD_DIM:(hh + 1) * HEAD_DIM]
                f = lo + (1.0 - lo) * jax.nn.sigmoid(head(t, i))
                k_ref[hh] = (1.0 - f).astype(BF16)
                rest = jnp.log(f) * LOG2_E
                terms = []
                for _ in range(LOG_F_TERMS):
                    terms.append(rest.astype(BF16))
                    rest = rest - terms[-1].astype(F32)
                l_ref[hh] = jnp.concatenate(terms, axis=1)

    for c in range(2):
        t = proj(_C_HG + c * 512, 512)
        for i in range(4):
            hg_ref[c * 4 + i] = _silu(head(t, i)).astype(BF16)

    for c in range(2):
        ga_ref[:, c * 512:(c + 1) * 512] = _sigmoid(proj(_C_GA + c * 512, 512)).astype(BF16)
    for c in range(2):
        gh_ref[:, c * 512:(c + 1) * 512] = _sigmoid(proj(_C_GH + c * 512, 512)).astype(BF16)

    for c in range(2):
        t = proj(_C_HI + c * 512, 512)
        for i in range(4):
            hi_ref[c * 4 + i] = head(t, i).astype(BF16)


def _in_projection(x_lat, x_ctx, ctx_row0, mod, gain, w_in, cos_tab, sin_tab, q_norm, k_norm, lb_logits, *,
                   layer, n_lat, t_rows, seq):
    d = x_lat.shape[1]
    tm = INPROJ_TILE
    assert n_lat % tm == 0 and t_rows % tm == 0 and seq % tm == 0 and ctx_row0 % tm == 0
    n_lat_tiles = n_lat // tm
    per_seq = seq // tm

    def mod_row(i):
        return jnp.where(i < n_lat_tiles, i // per_seq, mod.shape[0] - 1 - 3)

    def pos_blk(i):
        return jnp.where(i < n_lat_tiles, i % per_seq, per_seq)

    tok = lambda w: pl.BlockSpec((tm, w), lambda i: (i, 0))
    hm = pl.BlockSpec((HGRN_HEADS, tm, HEAD_DIM), lambda i: (0, i, 0))
    hm3 = pl.BlockSpec((HGRN_HEADS, tm, LOG_F_TERMS * HEAD_DIM), lambda i: (0, i, 0))
    hm_shape = lambda terms: jax.ShapeDtypeStruct((HGRN_HEADS, t_rows, terms * HEAD_DIM), BF16)
    tm_shape = lambda w: jax.ShapeDtypeStruct((t_rows, w), BF16)
    return pl.pallas_call(
        functools.partial(_inproj_kernel, layer=layer, n_lat_tiles=n_lat_tiles),
        grid=(t_rows // tm,),
        in_specs=[
            pl.BlockSpec((tm, d), lambda i: (jnp.minimum(i, n_lat_tiles - 1), 0)),
            pl.BlockSpec((tm, d), lambda i: (ctx_row0 // tm + jnp.maximum(i - n_lat_tiles, 0), 0)),
            pl.BlockSpec((None, 1, d), lambda i: (mod_row(i), 0, 0)),
            pl.BlockSpec((None, 1, d), lambda i: (mod_row(i), 0, 1)),
            pl.BlockSpec((1, d), lambda i: (0, 0)),
            _resident((d, PROJ_WIDTH), lambda i: (0, 0)),
            pl.BlockSpec((tm, HEAD_DIM), lambda i: (pos_blk(i), 0)),
            pl.BlockSpec((tm, HEAD_DIM), lambda i: (pos_blk(i), 0)),
            pl.BlockSpec((1, HEAD_DIM), lambda i: (0, 0)),
            pl.BlockSpec((1, HEAD_DIM), lambda i: (0, 0)),
            pl.BlockSpec(lb_logits.shape, lambda i: (0, 0)),
        ],
        out_specs=[tok(d), tok(KV_WIDTH), tok(KV_WIDTH), hm, hm, hm3, hm, hm3, hm, hm, tok(d), tok(d)],
        out_shape=[tm_shape(d), tm_shape(KV_WIDTH), tm_shape(KV_WIDTH), hm_shape(1), hm_shape(1),
                   hm_shape(LOG_F_TERMS), hm_shape(1), hm_shape(LOG_F_TERMS), hm_shape(1), hm_shape(1),
                   tm_shape(d), tm_shape(d)],
        compiler_params=_params("arbitrary"),
        name=f"in_projection_l{layer}",
    )(x_lat, x_ctx, mod, mod, gain, w_in, cos_tab, sin_tab, q_norm, k_norm, lb_logits)


def _hgrn_constants(n, c):
    t = lax.broadcasted_iota(jnp.int32, (n, n), 0)
    s = lax.broadcasted_iota(jnp.int32, (n, n), 1)
    tri = ((t >= s).astype(BF16), (t <= s).astype(BF16))
    t = lax.broadcasted_iota(jnp.int32, (c, c), 0)
    s = lax.broadcasted_iota(jnp.int32, (c, c), 1)
    levels = []
    blk = 2
    while blk <= c:
        half = blk // 2
        same = (t // blk) == (s // blk)
        t_hi = (t % blk) >= half
        s_hi = (s % blk) >= half
        levels.append((blk, (same & t_hi & ~s_hi, same & ~t_hi & s_hi)))
        blk *= 2
    return tri, t == s, levels


def _boundary_decay(cum, blk, rev):
    n = cum.shape[0]
    half = blk // 2
    off = half if rev else half - 1
    if blk >= 2 * SUBLANES:
        rows = [jnp.broadcast_to(cum[a0 + off:a0 + off + 1, :], (blk, LANES)) for a0 in range(0, n, blk)]
        mid = rows[0] if len(rows) == 1 else jnp.concatenate(rows, axis=0)
    else:
        c3 = cum.reshape(n // SUBLANES, SUBLANES, LANES)
        sub = lax.broadcasted_iota(jnp.int32, c3.shape, 1)
        mid = jnp.broadcast_to(c3[:, off:off + 1, :], c3.shape)
        for a0 in range(blk, SUBLANES, blk):
            mid = jnp.where(sub >= a0, jnp.broadcast_to(c3[:, a0 + off:a0 + off + 1, :], c3.shape), mid)
        mid = mid.reshape(n, LANES)
    x = lax.bitcast_convert_type(cum - mid, jnp.uint32) | jnp.uint32(0x80000000)
    return jnp.exp2(lax.bitcast_convert_type(x, F32))


def _hgrn_block(load, consts, chunk, rev, done):
    tri, eye, levels = consts
    q, k, v, g3, st_t = load()
    n = q.shape[0]
    parts = _dot(tri[rev], g3)
    yield
    cum = parts[:, :LANES]
    for i in range(1, g3.shape[1] // LANES):
        cum = cum + parts[:, i * LANES:(i + 1) * LANES]
    tail = cum[0:1] if rev else cum[n - 1:n]

    qf = q.astype(F32)
    kf = k.astype(F32)
    scaled = lambda t, e: (t * e).astype(BF16)
    out = _dot_nt(scaled(qf, jnp.exp2(cum)), st_t.astype(BF16))
    yield
    k_end = scaled(kf, jnp.exp2(tail - cum))
    st_new = st_t * jnp.exp2(tail) + _dot_tn(v, k_end)
    yield

    blk = n
    far = []
    while blk > chunk:
        half = blk // 2
        e = _boundary_decay(cum, blk, rev)
        qe = scaled(qf, e)
        ke = scaled(kf, e)
        for a0 in range(0, n, blk):
            qr = a0 if rev else a0 + half
            kr = a0 + half if rev else a0
            sc = _dot_nt(qe[qr:qr + half], ke[kr:kr + half])
            far.append((qr, half, _dot(sc.astype(BF16), v[kr:kr + half])))
        blk //= 2
    yield

    sides = [(q, k)]
    for blk, _ in levels:
        if blk == 2:
            sides.append((scaled(qf, jnp.exp2(g3[:, :LANES].astype(F32))), k))
        else:
            e = _boundary_decay(cum, blk, rev)
            sides.append((scaled(qf, e), scaled(kf, e)))
            yield
    masks = [eye] + [m[rev] for _, m in levels]
    outs = []
    for c0 in range(0, n, chunk):
        yield
        a = jnp.zeros((chunk, chunk), F32)
        rows = slice(c0, c0 + chunk)
        zero = jnp.zeros((chunk, LANES), BF16)
        for i in range(0, len(sides) - 1, 2):
            (qa, ka), (qb, kb) = sides[i], sides[i + 1]
            keys = jnp.concatenate([jnp.concatenate([ka[rows], zero], axis=1),
                                    jnp.concatenate([zero, kb[rows]], axis=1)], axis=0)
            sc = _dot_nt(jnp.concatenate([qa[rows], qb[rows]], axis=1), keys)
            a = jnp.where(masks[i], sc[:, :chunk], a)
            a = jnp.where(masks[i + 1], sc[:, chunk:], a)
            yield
        if len(sides) % 2:
            a = jnp.where(masks[-1], _dot_nt(sides[-1][0][rows], sides[-1][1][rows]), a)
            yield
        o = out[c0:c0 + chunk] + _dot(a.astype(BF16), v[c0:c0 + chunk])
        for r0, rows, val in far:
            if r0 <= c0 < r0 + rows:
                o = o + val[c0 - r0:c0 - r0 + chunk]
        outs.append(o)
    done(outs[0] if len(outs) == 1 else jnp.concatenate(outs, axis=0), st_new)


def _softmax_tile(q, carry, k, v):
    m, acc = carry
    ones_col = (lax.broadcasted_iota(jnp.int32, v.shape, 1) == 0).astype(BF16)
    s = _dot_nt(q, k)
    m_new = jnp.maximum(m, jnp.max(s, axis=-1, keepdims=True))
    p = jnp.exp2((s - m_new).astype(BF16))
    acc = jnp.exp2(m - m_new) * acc + _dot(p, jnp.concatenate([v, ones_col], axis=1))
    return m_new, acc


def _softmax_init(rows):
    return jnp.full((rows, 1), -1e30, F32), jnp.zeros((rows, 2 * HEAD_DIM), F32)


def _softmax_result(carry):
    acc = carry[1]
    return acc[:, :HEAD_DIM] / acc[:, HEAD_DIM:HEAD_DIM + 1]


def _hgrn_kernel(qf_ref, vf_ref, kf_ref, lf_ref, qb_ref, vb_ref, kb_ref, lb_ref, of_ref, ob_ref,
                 st_ref, *, chunk):
    @pl.when(pl.program_id(1) == 0)
    def _():
        st_ref[...] = jnp.zeros(st_ref.shape, F32)

    consts = _hgrn_constants(qf_ref.shape[1], chunk)

    def block(h, rev):
        q_ref, k_ref, v_ref, l_ref, o_ref = ((qf_ref, kf_ref, vf_ref, lf_ref, of_ref),
                                             (qb_ref, kb_ref, vb_ref, lb_ref, ob_ref))[rev]

        def done(o, st):
            o_ref[h] = o
            st_ref[rev, h] = st
        load = lambda: (q_ref[h], k_ref[h], v_ref[h], l_ref[h], st_ref[rev, h])
        return _hgrn_block(load, consts, chunk, rev, done)

    blocks = [block(h, rev) for h in range(qf_ref.shape[0]) for rev in (0, 1)]
    live = []
    turn = 0
    while blocks or live:
        if blocks and turn % HGRN_STAGGER == 0:
            live.append(blocks.pop(0))
        turn += 1
        for gen in list(live):
            if next(gen, StopIteration) is StopIteration:
                live.remove(gen)


def _hgrn_scan(hq, hi, kf, lf, kb, lb, *, batch, seq, ctx_len):
    heads, t_rows, dh = hq.shape
    ts = HGRN_STEP
    n_ctx = ctx_len // ts
    n_seq = seq // ts
    ctx0 = batch * n_seq

    def fwd(b, s):
        return jnp.where(s < n_ctx, ctx0 + b * n_ctx + s, b * n_seq + s - n_ctx)

    def bwd(b, s):
        return jnp.where(s < n_ctx, ctx0 + b * n_ctx + n_ctx - 1 - s, b * n_seq + n_seq - 1 - (s - n_ctx))

    spec_f = pl.BlockSpec((heads, ts, dh), lambda b, s: (0, fwd(b, s), 0))
    spec_b = pl.BlockSpec((heads, ts, dh), lambda b, s: (0, bwd(b, s), 0))
    log_f = pl.BlockSpec((heads, ts, lf.shape[2]), lambda b, s: (0, fwd(b, s), 0))
    log_b = pl.BlockSpec((heads, ts, lb.shape[2]), lambda b, s: (0, bwd(b, s), 0))
    out = jax.ShapeDtypeStruct((heads, t_rows, dh), F32)
    return pl.pallas_call(
        functools.partial(_hgrn_kernel, chunk=HGRN_CHUNK),
        grid=(batch, n_ctx + n_seq),
        in_specs=[spec_f, spec_f, spec_f, log_f, spec_b, spec_b, spec_b, log_b],
        out_specs=[spec_f, spec_b],
        out_shape=[out, out],
        scratch_shapes=[pltpu.VMEM((2, heads, dh, dh), F32)],
        compiler_params=_params("arbitrary", "arbitrary"),
        name="hgrn_scan",
    )(hq, hi, kf, lf, hq, hi, kb, lb)


def _attn_kernel(q_ref, kc_ref, vc_ref, kl_ref, vl_ref, o_ref, *, n_lat_q_steps, k_tile):
    tq = ATT_Q_TILE
    is_latent = pl.program_id(2) < n_lat_q_steps

    def query_tiles(with_latent_keys):
        for r0 in range(0, q_ref.shape[0], tq):
            q = jnp.concatenate([q_ref[r0:r0 + tq, i * HEAD_DIM:(i + 1) * HEAD_DIM] for i in range(ATT_GROUP)],
                                axis=0)
            carry = _softmax_tile(q, _softmax_init(ATT_GROUP * tq), kc_ref[...], vc_ref[...])
            if with_latent_keys:
                for t in range(0, kl_ref.shape[0], k_tile):
                    carry = _softmax_tile(q, carry, kl_ref[t:t + k_tile, :], vl_ref[t:t + k_tile, :])
            out = _softmax_result(carry)
            for i in range(ATT_GROUP):
                o_ref[r0:r0 + tq, i * HEAD_DIM:(i + 1) * HEAD_DIM] = out[i * tq:(i + 1) * tq].astype(BF16)

    @pl.when(is_latent)
    def _():
        query_tiles(True)

    @pl.when(jnp.logical_not(is_latent))
    def _():
        query_tiles(False)


def _attention(aq, ak, av, *, batch, seq, ctx_len, with_ctx_queries):
    t_rows = aq.shape[0]
    tq = ATT_Q_STEP
    nq_lat = seq // tq
    nq_ctx = ctx_len // tq
    ctx_q0 = batch * nq_lat
    ctx_k0 = batch * seq // ctx_len
    gw = ATT_GROUP * HEAD_DIM

    def q_blk(b, i):
        return jnp.where(i < nq_lat, b * nq_lat + i, ctx_q0 + b * nq_ctx + i - nq_lat)

    q_spec = pl.BlockSpec((tq, gw), lambda b, g, i: (q_blk(b, i), g))
    kc_spec = pl.BlockSpec((ctx_len, HEAD_DIM), lambda b, g, i: (ctx_k0 + b, g))
    kl_spec = pl.BlockSpec((seq, HEAD_DIM), lambda b, g, i: (b, g))
    return pl.pallas_call(
        functools.partial(_attn_kernel, n_lat_q_steps=nq_lat, k_tile=ATT_K_TILE),
        grid=(batch, ATT_KV_HEADS, nq_lat + (nq_ctx if with_ctx_queries else 0)),
        in_specs=[q_spec, kc_spec, kc_spec, kl_spec, kl_spec],
        out_specs=q_spec,
        out_shape=jax.ShapeDtypeStruct((t_rows if with_ctx_queries else batch * seq, ATT_HEADS * HEAD_DIM),
                                       BF16),
        compiler_params=_params("arbitrary", "arbitrary", "arbitrary"),
        name="gqa_attention",
    )(aq, ak, av, ak, av)


def _merge_kernel(xl_ref, xc_ref, att_ref, of_ref, ob_ref, hg_ref, ga_ref, gh_ref, g1_ref, hgn_ref, post_ref,
                  wa_ref, wh_ref, wo_ref, o_ref, hg_scr, *, n_lat_tiles):
    ya = _dot(att_ref[...], wa_ref[...])
    hgn = hgn_ref[...]
    for i in range(HGRN_HEADS):
        o = _rms(of_ref[i] + ob_ref[i], hgn)
        hg_scr[:, i * HEAD_DIM:(i + 1) * HEAD_DIM] = (o * hg_ref[i].astype(F32)).astype(BF16)
    y = ga_ref[...].astype(F32) * ya + gh_ref[...].astype(F32) * _dot(hg_scr[...], wh_ref[...])
    z = _rms(_dot(y.astype(BF16), wo_ref[...]), post_ref[...])
    x = jnp.where(pl.program_id(0) < n_lat_tiles, xl_ref[...], xc_ref[...])
    o_ref[...] = x + g1_ref[...] * z


def _merge(x_lat, x_ctx, ctx_row0, att, o_f, o_b, hg, ga, gh, mod, hg_norm, post_norm, w_att, w_hg, w_out, *,
           rows, n_lat, seq):
    d = x_lat.shape[1]
    tm = TOKEN_TILE
    n_lat_tiles = n_lat // tm
    per_seq = seq // tm

    def mod_row(i):
        return jnp.where(i < n_lat_tiles, i // per_seq, mod.shape[0] - 1 - 3)

    tok = pl.BlockSpec((tm, d), lambda i: (i, 0))
    hm = pl.BlockSpec((HGRN_HEADS, tm, HEAD_DIM), lambda i: (0, i, 0))
    vec = lambda w: pl.BlockSpec((1, w), lambda i: (0, 0))
    wgt = _resident((d, d), lambda i: (0, 0))
    return pl.pallas_call(
        functools.partial(_merge_kernel, n_lat_tiles=n_lat_tiles),
        grid=(rows // tm,),
        in_specs=[pl.BlockSpec((tm, d), lambda i: (jnp.minimum(i, n_lat_tiles - 1), 0)),
                  pl.BlockSpec((tm, d), lambda i: (ctx_row0 // tm + jnp.maximum(i - n_lat_tiles, 0), 0)),
                  tok, hm, hm, hm, tok, tok,
                  pl.BlockSpec((None, 1, d), lambda i: (mod_row(i), 0, 2)),
                  vec(HEAD_DIM), vec(d), wgt, wgt, wgt],
        out_specs=tok,
        out_shape=jax.ShapeDtypeStruct((rows, d), F32),
        scratch_shapes=[pltpu.VMEM((tm, d), BF16)],
        compiler_params=_params("arbitrary"),
        name="branch_merge",
    )(x_lat, x_ctx, att, o_f, o_b, hg, ga, gh, mod, hg_norm, post_norm, w_att, w_hg, w_out)


def _ffn_kernel(x_ref, sh_ref, sc_ref, g2_ref, pre_ref, post_ref, wg_ref, wu_ref, wd_ref, o_ref, a_scr,
                *, ff_tile):
    x = x_ref[...]
    f = (_rms(x, pre_ref[...]) * (1.0 + sc_ref[...]) + sh_ref[...]).astype(BF16)
    for c in range(0, wg_ref.shape[1], ff_tile):
        g = _dot(f, wg_ref[:, c:c + ff_tile])
        u = _dot(f, wu_ref[:, c:c + ff_tile])
        a_scr[:, c:c + ff_tile] = (_silu(g) * u).astype(BF16)
    y = _rms(_dot(a_scr[...], wd_ref[...]), post_ref[...])
    o_ref[...] = x + g2_ref[...] * y


def _dense_ffn(x, mod, pre_norm, post_norm, w_gate, w_up, w_down, *, n_lat, seq):
    rows, d = x.shape
    d_ff = w_gate.shape[1]
    tm = TOKEN_TILE
    n_lat_tiles = n_lat // tm
    per_seq = seq // tm

    def mod_row(i):
        return jnp.where(i < n_lat_tiles, i // per_seq, mod.shape[0] - 1 - 3)

    tok = pl.BlockSpec((tm, d), lambda i: (i, 0))
    vec = pl.BlockSpec((1, d), lambda i: (0, 0))
    mod_spec = lambda col: pl.BlockSpec((None, 1, d), lambda i: (mod_row(i), 0, col))
    return pl.pallas_call(
        functools.partial(_ffn_kernel, ff_tile=256),
        grid=(rows // tm,),
        in_specs=[tok, mod_spec(3), mod_spec(4), mod_spec(5), vec, vec,
                  _resident((d, d_ff), lambda i: (0, 0)), _resident((d, d_ff), lambda i: (0, 0)),
                  _resident((d_ff, d), lambda i: (0, 0))],
        out_specs=tok,
        out_shape=jax.ShapeDtypeStruct((rows, d), F32),
        scratch_shapes=[pltpu.VMEM((tm, d_ff), BF16)],
        compiler_params=_params("arbitrary"),
        name="dense_swiglu",
    )(x, mod, mod, mod, pre_norm, post_norm, w_gate, w_up, w_down)


def _router_kernel(x_ref, sh_ref, sc_ref, pre_ref, wr_ref, f_ref, route_ref):
    f = _rms(x_ref[...], pre_ref[...]) * (1.0 + sc_ref[...]) + sh_ref[...]
    f_ref[...] = f
    lane = lax.broadcasted_iota(jnp.int32, (f.shape[0], LANES), 1)
    neg = jnp.float32(-jnp.inf)
    logits = jnp.full((f.shape[0], LANES), neg, F32)
    for e in range(wr_ref.shape[0]):
        logits = jnp.where(lane == e, jnp.sum(f * wr_ref[e:e + 1, :], axis=-1, keepdims=True), logits)
    m1 = jnp.max(logits, axis=-1, keepdims=True)
    i1 = jnp.min(jnp.where(logits == m1, lane, LANES), axis=-1, keepdims=True)
    rest = jnp.where(lane == i1, neg, logits)
    m2 = jnp.max(rest, axis=-1, keepdims=True)
    i2 = jnp.min(jnp.where(rest == m2, lane, LANES), axis=-1, keepdims=True)
    w1 = 1.0 / (1.0 + jnp.exp(m2 - m1))
    w2 = 1.0 - w1
    route = jnp.where(lane == 0, i1.astype(F32),
                      jnp.where(lane == 1, i2.astype(F32),
                                jnp.where(lane == 2, w1, jnp.where(lane == 3, w2, 0.0))))
    route_ref[...] = route


def _router(x, mod, pre_norm, w_router_t, *, seq):
    rows, d = x.shape
    tm = TOKEN_TILE
    per_seq = seq // tm
    tok = pl.BlockSpec((tm, d), lambda i: (i, 0))
    mod_spec = lambda col: pl.BlockSpec((None, 1, d), lambda i: (i // per_seq, 0, col))
    return pl.pallas_call(
        _router_kernel,
        grid=(rows // tm,),
        in_specs=[tok, mod_spec(3), mod_spec(4), pl.BlockSpec((1, d), lambda i: (0, 0)),
                  pl.BlockSpec(w_router_t.shape, lambda i: (0, 0))],
        out_specs=[tok, pl.BlockSpec((tm, LANES), lambda i: (i, 0))],
        out_shape=[jax.ShapeDtypeStruct((rows, d), F32), jax.ShapeDtypeStruct((rows, LANES), F32)],
        compiler_params=_params("arbitrary"),
        name="moe_router",
    )(x, mod, mod, pre_norm, w_router_t)


def _expert_kernel(be_ref, tok_ref, dst_ref, f_hbm, wg_ref, wu_ref, wd_ref, ysel_hbm,
                   xbuf, ybuf, a_scr, gsem, ssem, *, ff_tile, n_blocks, spare_row0):
    del be_ref
    s = pl.program_id(0)
    rows = xbuf.shape[1]

    def gather_row(r, slot):
        pltpu.make_async_copy(f_hbm.at[pl.ds(tok_ref[r], 1)], xbuf.at[slot, pl.ds(r, 1)],
                              gsem.at[slot]).start()

    def scatter_row(r, slot, dst_row):
        pltpu.make_async_copy(ybuf.at[slot, pl.ds(r, 1)], ysel_hbm.at[pl.ds(dst_row, 1)],
                              ssem.at[slot]).start()

    def wait_gather(slot):
        pltpu.make_async_copy(f_hbm.at[pl.ds(0, rows)], xbuf.at[slot], gsem.at[slot]).wait()

    def wait_scatter(slot):
        pltpu.make_async_copy(ybuf.at[slot], ysel_hbm.at[pl.ds(0, rows)], ssem.at[slot]).wait()

    @pl.when(s == 0)
    def _():
        ybuf[...] = jnp.zeros(ybuf.shape, F32)

        def body(r, carry):
            gather_row(r, 0)
            scatter_row(r, 0, spare_row0 + r)
            return carry
        lax.fori_loop(0, rows, body, 0)

    @pl.when((s >= 1) & (s <= n_blocks))
    def _():
        slot = (s - 1) % 2
        wait_gather(slot)
        wait_scatter(slot)
        xb = xbuf[slot].astype(BF16)
        chunks = list(range(0, wg_ref.shape[1], ff_tile))
        per = -(-rows // (len(chunks) + 1))

        def move_rows(j):
            for r in range(j * per, min((j + 1) * per, rows)):
                gather_row(r, 1 - slot)
                scatter_row(r, 1 - slot, dst_ref[r])

        for j, c in enumerate(chunks):
            move_rows(j)
            g = _dot(xb, wg_ref[:, c:c + ff_tile])
            u = _dot(xb, wu_ref[:, c:c + ff_tile])
            a_scr[:, c:c + ff_tile] = (_silu(g) * u).astype(BF16)
        move_rows(len(chunks))
        ybuf[slot] = _dot(a_scr[...], wd_ref[...])

    @pl.when(s == n_blocks + 1)
    def _():
        slot = (n_blocks - 1) % 2

        def body(r, carry):
            scatter_row(r, slot, dst_ref[r])
            return carry
        lax.fori_loop(0, rows, body, 0)
        wait_scatter(1 - slot)
        wait_scatter(slot)
        wait_gather(1 - slot)


def _expert_ffn(f, row_tok, row_dst_ext, block_expert, w_gate, w_up, w_down, *, n_out_rows, spare_row0):
    n_rows = row_tok.shape[0]
    d = f.shape[1]
    d_ff = w_gate.shape[2]
    n_blocks = n_rows // MOE_BLOCK
    idx = lambda fn: pl.BlockSpec((MOE_BLOCK,), fn, memory_space=pltpu.SMEM)
    wgt = lambda shape: pl.BlockSpec((None,) + shape,
                                     lambda s, be: (be[jnp.clip(s - 1, 0, n_blocks - 1)], 0, 0),
                                     pipeline_mode=pl.Buffered(1))
    grid_spec = pltpu.PrefetchScalarGridSpec(
        num_scalar_prefetch=1,
        grid=(n_blocks + 2,),
        in_specs=[idx(lambda s, be: (jnp.minimum(s, n_blocks - 1),)),
                  idx(lambda s, be: (jnp.maximum(s - 1, 0),)),
                  pl.BlockSpec(memory_space=pl.ANY),
                  wgt((d, d_ff)), wgt((d, d_ff)), wgt((d_ff, d))],
        out_specs=pl.BlockSpec(memory_space=pl.ANY),
        scratch_shapes=[pltpu.VMEM((2, MOE_BLOCK, d), F32), pltpu.VMEM((2, MOE_BLOCK, d), F32),
                        pltpu.VMEM((MOE_BLOCK, d_ff), BF16),
                        pltpu.SemaphoreType.DMA((2,)), pltpu.SemaphoreType.DMA((2,))],
    )
    return pl.pallas_call(
        functools.partial(_expert_kernel, ff_tile=512, n_blocks=n_blocks, spare_row0=spare_row0),
        grid_spec=grid_spec,
        out_shape=jax.ShapeDtypeStruct((n_out_rows, d), F32),
        compiler_params=_params("arbitrary"),
        name="moe_expert_swiglu",
    )(block_expert, row_tok, row_dst_ext, f, w_gate, w_up, w_down)


def _combine_kernel(y0_ref, y1_ref, x_ref, route_ref, g2_ref, post_ref, o_ref):
    route = route_ref[...]
    y = route[:, 2:3] * y0_ref[...] + route[:, 3:4] * y1_ref[...]
    o_ref[...] = x_ref[...] + g2_ref[...] * _rms(y, post_ref[...])


def _combine(y_sel, x, route, mod, post_norm, *, seq, choice_stride):
    rows, d = x.shape
    tm = TOKEN_TILE
    per_seq = seq // tm
    tok = pl.BlockSpec((tm, d), lambda i: (i, 0))
    return pl.pallas_call(
        _combine_kernel,
        grid=(rows // tm,),
        in_specs=[tok, pl.BlockSpec((tm, d), lambda i: (choice_stride // tm + i, 0)), tok,
                  pl.BlockSpec((tm, LANES), lambda i: (i, 0)),
                  pl.BlockSpec((None, 1, d), lambda i: (i // per_seq, 0, 5)),
                  pl.BlockSpec((1, d), lambda i: (0, 0))],
        out_specs=tok,
        out_shape=jax.ShapeDtypeStruct((rows, d), F32),
        compiler_params=_params("arbitrary"),
        name="moe_combine",
    )(y_sel, y_sel, x, route, mod, post_norm)


def _moe_ffn(x, mod, pre_norm, post_norm, w_router, w_gate, w_up, w_down, *, seq):
    n_tok, d = x.shape
    n_assign = n_tok * TOP_K
    n_blocks = -(-n_assign // MOE_BLOCK) + N_EXPERTS
    n_rows = n_blocks * MOE_BLOCK
    n_pad = n_rows - n_assign
    assert TOP_K == 2 and n_pad % (TOP_K * TOKEN_TILE) == 0
    f, route = _router(x, mod, pre_norm, w_router.T, seq=seq)

    flat_e = route[:, :TOP_K].astype(jnp.int32).reshape(-1)
    onehot = (flat_e[:, None] == jnp.arange(N_EXPERTS, dtype=jnp.int32)[None, :]).astype(jnp.int32)
    csum = jnp.cumsum(onehot, axis=0)
    counts = csum[-1]
    rank = jnp.sum((csum - onehot) * onehot, axis=1)
    padded = (counts + MOE_BLOCK - 1) // MOE_BLOCK * MOE_BLOCK
    pend = jnp.cumsum(padded)
    dest = ((pend - padded)[flat_e] + rank).astype(jnp.int32)
    block_expert = jnp.minimum(
        jnp.searchsorted(pend, jnp.arange(n_blocks, dtype=jnp.int32) * MOE_BLOCK, side='right'),
        N_EXPERTS - 1).astype(jnp.int32)
    choice_stride = n_tok + n_pad // TOP_K
    row_assign = jnp.full((n_rows,), -1, jnp.int32).at[dest].set(jnp.arange(n_assign, dtype=jnp.int32),
                                                                 unique_indices=True)
    is_pad = row_assign < 0
    pad_id = jnp.cumsum(is_pad.astype(jnp.int32)) - 1
    row_tok = jnp.where(is_pad, 0, row_assign // TOP_K)
    row_dst = jnp.where(is_pad, (pad_id % TOP_K) * choice_stride + n_tok + pad_id // TOP_K,
                        (row_assign % TOP_K) * choice_stride + row_assign // TOP_K)
    row_dst_ext = jnp.concatenate([n_rows + jnp.arange(MOE_BLOCK, dtype=jnp.int32), row_dst])

    y_sel = _expert_ffn(f, row_tok, row_dst_ext, block_expert, w_gate, w_up, w_down,
                        n_out_rows=n_rows + 2 * MOE_BLOCK, spare_row0=n_rows + MOE_BLOCK)
    return _combine(y_sel, x, route, mod, post_norm, seq=seq, choice_stride=choice_stride)


def _rope_tables(seq, pad_rows):
    pos = jnp.arange(seq)
    nfreq = HEAD_DIM // 4
    inv_freq = ROPE_THETA ** (-jnp.arange(nfreq, dtype=F32) / nfreq)
    ang = jnp.concatenate([(pos // GRID_W).astype(F32)[:, None] * inv_freq,
                           (pos % GRID_W).astype(F32)[:, None] * inv_freq], axis=-1)
    ang = jnp.concatenate([ang, ang], axis=-1)
    sign = jnp.where(jnp.arange(HEAD_DIM) < HEAD_DIM // 2, -1.0, 1.0).astype(F32)
    cos = jnp.concatenate([jnp.cos(ang), jnp.ones((pad_rows, HEAD_DIM), F32)], axis=0)
    sin = jnp.concatenate([jnp.sin(ang) * sign, jnp.zeros((pad_rows, HEAD_DIM), F32)], axis=0)
    return cos, sin


def kernel(x, c, ctx, c_ctx, w_mod, b_mod, pre_mix_norm, post_mix_norm, pre_ffn_norm, post_ffn_norm, w_in, q_norm, k_norm, hg_norm, hg_lb_logits, w_att_branch, w_hg_branch, w_out, ffn_w_gate, ffn_w_up, ffn_w_down, moe_router, moe_w_gate, moe_w_up, moe_w_down):
    batch, seq, d = x.shape
    ctx_len = ctx.shape[1]
    depth = w_mod.shape[0]
    n_lat = batch * seq
    assert d == D_MODEL and w_in.shape[2] == PROJ_WIDTH
    assert seq % HGRN_STEP == 0 and ctx_len % HGRN_STEP == 0 and n_lat % TOKEN_TILE == 0
    assert seq % ctx_len == 0 and ctx_len % TOKEN_TILE == 0 and seq % ATT_K_TILE == 0

    c_rows = jnp.concatenate([c, c_ctx[None, :], jnp.zeros((3, d), F32)], axis=0)
    mod_all = _mod_vectors(c_rows, w_mod, b_mod)
    cos_tab, sin_tab = _rope_tables(seq, INPROJ_TILE)
    row = lambda v: v.reshape(1, -1)

    t_rows = n_lat + batch * ctx_len
    x_lat, x_ctx, ctx_row0 = x.reshape(n_lat, d), ctx.reshape(batch * ctx_len, d), 0
    for layer in range(depth):
        last = layer == depth - 1
        mod = mod_all[layer].reshape(batch + 4, 1, N_MOD * d)
        (aq, ak, av, hq, kf, lf, kb, lb, hi, hg, ga, gh) = _in_projection(
            x_lat, x_ctx, ctx_row0, mod, row(pre_mix_norm[layer]), w_in[layer].astype(BF16), cos_tab, sin_tab,
            row(q_norm[layer]), row(k_norm[layer]), hg_lb_logits, layer=layer, n_lat=n_lat, t_rows=t_rows,
            seq=seq)
        o_f, o_b = _hgrn_scan(hq, hi, kf, lf, kb, lb, batch=batch, seq=seq, ctx_len=ctx_len)
        att = _attention(aq, ak, av, batch=batch, seq=seq, ctx_len=ctx_len, with_ctx_queries=not last)
        rows = n_lat if last else t_rows
        xt = _merge(x_lat, x_ctx, ctx_row0, att, o_f, o_b, hg, ga, gh, mod, row(hg_norm[layer]), row(post_mix_norm[layer]),
                    w_att_branch[layer].astype(BF16), w_hg_branch[layer].astype(BF16),
                    w_out[layer].astype(BF16), rows=rows, n_lat=n_lat, seq=seq)
        idx = layer // 2
        if layer % 2 == 0:
            xt = _dense_ffn(xt, mod, row(pre_ffn_norm[layer]), row(post_ffn_norm[layer]),
                            ffn_w_gate[idx].astype(BF16), ffn_w_up[idx].astype(BF16),
                            ffn_w_down[idx].astype(BF16), n_lat=n_lat, seq=seq)
            x_lat, x_ctx, ctx_row0 = xt, xt, n_lat
        else:
            assert last, "the expert mixer is implemented for latent tokens only"
            xt = _moe_ffn(xt, mod, row(pre_ffn_norm[layer]), row(post_ffn_norm[layer]), moe_router[idx],
                          moe_w_gate[idx].astype(BF16), moe_w_up[idx].astype(BF16),
                          moe_w_down[idx].astype(BF16), seq=seq)
    return xt[:n_lat].reshape(batch, seq, d)
```

```python
import functools

import jax
import jax.numpy as jnp
from jax import lax
from jax.experimental import pallas as pl
from jax.experimental.pallas import tpu as pltpu

F32 = jnp.float32
BF16 = jnp.bfloat16

D_MODEL = 1024
NORM_EPS = 1e-6
LOG2_E = 1.4426950408889634
N_MOD = 6
GRID_W = 64
ROPE_THETA = 10000.0

HEAD_DIM = 128
ATT_HEADS = 8
ATT_KV_HEADS = 2
ATT_GROUP = ATT_HEADS // ATT_KV_HEADS
KV_WIDTH = ATT_KV_HEADS * HEAD_DIM
HGRN_HEADS = 8

N_EXPERTS = 8
TOP_K = 2
MOE_BLOCK = 256

SUBLANES = 8
LANES = 128
BF16_ROWS = 16
VMEM_LIMIT_BYTES = 56 * 1024 * 1024

TOKEN_TILE = 256
INPROJ_TILE = 256
HGRN_CHUNK = 128
HGRN_STEP = 256
HGRN_STAGGER = 2
LOG_F_TERMS = 2
ATT_Q_TILE = 128
ATT_Q_STEP = 256
ATT_K_TILE = 512

_C_AQ = 0
_C_AK = _C_AQ + ATT_HEADS * HEAD_DIM
_C_AV = _C_AK + KV_WIDTH
_C_HQ = _C_AV + KV_WIDTH
_C_HFF = _C_HQ + D_MODEL
_C_HFB = _C_HFF + D_MODEL
_C_HI = _C_HFB + D_MODEL
_C_HG = _C_HI + D_MODEL
_C_GA = _C_HG + D_MODEL
_C_GH = _C_GA + D_MODEL
PROJ_WIDTH = _C_GH + D_MODEL


def _params(*sem):
    return pltpu.CompilerParams(dimension_semantics=sem, vmem_limit_bytes=VMEM_LIMIT_BYTES)


def _resident(shape, index_map):
    return pl.BlockSpec(shape, index_map, pipeline_mode=pl.Buffered(1))


def _rms(t, gain):
    return t * lax.rsqrt(jnp.mean(t * t, axis=-1, keepdims=True) + NORM_EPS) * gain


def _sigmoid(t):
    return 0.5 * jnp.tanh(0.5 * t) + 0.5


def _silu(t):
    return t * _sigmoid(t)


def _dot(a, b):
    return jnp.dot(a, b, preferred_element_type=F32)


def _dot_nt(a, b):
    return lax.dot_general(a, b, (((1,), (1,)), ((), ())), preferred_element_type=F32)


def _dot_tn(a, b):
    return lax.dot_general(a, b, (((0,), (0,)), ((), ())), preferred_element_type=F32)


def _mod_kernel(c_ref, w_ref, b_ref, o_ref):
    a = _silu(c_ref[...])
    o_ref[...] = jnp.dot(a, w_ref[...], preferred_element_type=F32,
                         precision=lax.Precision.HIGHEST) + b_ref[...]


def _mod_vectors(c_rows, w_mod, b_mod):
    depth, d, width = w_mod.shape
    rows = c_rows.shape[0]
    tn = 1536
    return pl.pallas_call(
        _mod_kernel,
        grid=(depth, width // tn),
        in_specs=[
            pl.BlockSpec((rows, d), lambda l, j: (0, 0)),
            pl.BlockSpec((None, d, tn), lambda l, j: (l, 0, j)),
            pl.BlockSpec((None, 1, tn), lambda l, j: (l, 0, j)),
        ],
        out_specs=pl.BlockSpec((None, rows, tn), lambda l, j: (l, 0, j)),
        out_shape=jax.ShapeDtypeStruct((depth, rows, width), F32),
        compiler_params=_params("arbitrary", "arbitrary"),
        name="mod_vectors",
    )(c_rows, w_mod, b_mod.reshape(depth, 1, width))


def _inproj_kernel(xl_ref, xc_ref, sh_ref, sc_ref, gain_ref, w_ref, cos_ref, sin_ref, qn_ref, kn_ref, lbl_ref,
                   aq_ref, ak_ref, av_ref, hq_ref, kf_ref, lf_ref, kb_ref, lb_ref, hi_ref, hg_ref,
                   ga_ref, gh_ref, *, layer, n_lat_tiles):
    x = jnp.where(pl.program_id(0) < n_lat_tiles, xl_ref[...], xc_ref[...])
    h = _rms(x, gain_ref[...])
    hb = (h * (1.0 + sc_ref[...]) + sh_ref[...]).astype(BF16)
    cos = cos_ref[...]
    sin = sin_ref[...]

    def proj(c0, width):
        return _dot(hb, w_ref[:, c0:c0 + width])

    def head(t, i):
        return t[:, i * HEAD_DIM:(i + 1) * HEAD_DIM]

    def norm_rope(t, gain):
        r = _rms(t, gain)
        return r * cos + pltpu.roll(r, HEAD_DIM // 2, 1) * sin

    lg = lbl_ref[...]
    depth = lg.shape[0]
    mx = lg[0:1]
    for j in range(1, depth):
        mx = jnp.maximum(mx, lg[j:j + 1])
    es = [jnp.exp(lg[j:j + 1] - mx) for j in range(depth)]
    tot = es[0]
    for j in range(1, depth):
        tot = tot + es[j]
    low = jnp.zeros_like(tot)
    for j in range(1, layer + 1):
        low = low + es[j] / tot
    qn = qn_ref[...]
    kn = kn_ref[...]
    half = 4 * HEAD_DIM

    def queries(c):
        def epilogue(t):
            for i in range(4):
                hh = c * 4 + i
                aq_ref[:, hh * HEAD_DIM:(hh + 1) * HEAD_DIM] = (
                    norm_rope(head(t, i), qn) * (HEAD_DIM ** -0.5 * LOG2_E)).astype(BF16)
        return epilogue

    def keys_values(t):
        for i in range(ATT_KV_HEADS):
            ak_ref[:, i * HEAD_DIM:(i + 1) * HEAD_DIM] = norm_rope(head(t, i), kn).astype(BF16)
        av_ref[...] = t[:, KV_WIDTH:].astype(BF16)

    def per_head(ref, c, fn):
        def epilogue(t):
            for i in range(4):
                ref[c * 4 + i] = fn(head(t, i)).astype(BF16)
        return epilogue

    def forget_gate(k_ref, l_ref, c):
        def epilogue(t):
            for i in range(4):
                hh = c * 4 + i
                lo = low[:, hh * HEAD_DIM:(hh + 1) * HEAD_DIM]
                f = lo + (1.0 - lo) * jax.nn.sigmoid(head(t, i))
                k_ref[hh] = (1.0 - f).astype(BF16)
                rest = jnp.log(f) * LOG2_E
                terms = []
                for _ in range(LOG_F_TERMS):
                    terms.append(rest.astype(BF16))
                    rest = rest - terms[-1].astype(F32)
                l_ref[hh] = jnp.concatenate(terms, axis=1)
        return epilogue

    def branch_gate(ref, c):
        def epilogue(t):
            ref[:, c * half:(c + 1) * half] = _sigmoid(t).astype(BF16)
        return epilogue

    groups = [(_C_AQ + c * half, queries(c)) for c in range(2)]
    groups += [(_C_AK, keys_values)]
    groups += [(_C_HQ + c * half, per_head(hq_ref, c, _silu)) for c in range(2)]
    groups += [(_C_HFF + c * half, forget_gate(kf_ref, lf_ref, c)) for c in range(2)]
    groups += [(_C_HFB + c * half, forget_gate(kb_ref, lb_ref, c)) for c in range(2)]
    groups += [(_C_HG + c * half, per_head(hg_ref, c, _silu)) for c in range(2)]
    groups += [(_C_GA + c * half, branch_gate(ga_ref, c)) for c in range(2)]
    groups += [(_C_GH + c * half, branch_gate(gh_ref, c)) for c in range(2)]
    groups += [(_C_HI + c * half, per_head(hi_ref, c, lambda t: t)) for c in range(2)]

    pending = None
    for c0, epilogue in groups:
        t = proj(c0, half)
        if pending is not None:
            pending[1](pending[0])
        pending = (t, epilogue)
    pending[1](pending[0])


def _in_projection(x_lat, x_ctx, ctx_row0, mod, gain, w_in, cos_tab, sin_tab, q_norm, k_norm, lb_logits, *,
                   layer, n_lat, t_rows, seq):
    d = x_lat.shape[1]
    tm = INPROJ_TILE
    assert n_lat % tm == 0 and t_rows % tm == 0 and seq % tm == 0 and ctx_row0 % tm == 0
    n_lat_tiles = n_lat // tm
    per_seq = seq // tm

    def mod_row(i):
        return jnp.where(i < n_lat_tiles, i // per_seq, mod.shape[0] - 1 - 3)

    def pos_blk(i):
        return jnp.where(i < n_lat_tiles, i % per_seq, per_seq)

    tok = lambda w: pl.BlockSpec((tm, w), lambda i: (i, 0))
    hm = pl.BlockSpec((HGRN_HEADS, tm, HEAD_DIM), lambda i: (0, i, 0))
    hm3 = pl.BlockSpec((HGRN_HEADS, tm, LOG_F_TERMS * HEAD_DIM), lambda i: (0, i, 0))
    hm_shape = lambda terms: jax.ShapeDtypeStruct((HGRN_HEADS, t_rows, terms * HEAD_DIM), BF16)
    tm_shape = lambda w: jax.ShapeDtypeStruct((t_rows, w), BF16)
    return pl.pallas_call(
        functools.partial(_inproj_kernel, layer=layer, n_lat_tiles=n_lat_tiles),
        grid=(t_rows // tm,),
        in_specs=[
            pl.BlockSpec((tm, d), lambda i: (jnp.minimum(i, n_lat_tiles - 1), 0)),
            pl.BlockSpec((tm, d), lambda i: (ctx_row0 // tm + jnp.maximum(i - n_lat_tiles, 0), 0)),
            pl.BlockSpec((None, 1, d), lambda i: (mod_row(i), 0, 0)),
            pl.BlockSpec((None, 1, d), lambda i: (mod_row(i), 0, 1)),
            pl.BlockSpec((1, d), lambda i: (0, 0)),
            _resident((d, PROJ_WIDTH), lambda i: (0, 0)),
            pl.BlockSpec((tm, HEAD_DIM), lambda i: (pos_blk(i), 0)),
            pl.BlockSpec((tm, HEAD_DIM), lambda i: (pos_blk(i), 0)),
            pl.BlockSpec((1, HEAD_DIM), lambda i: (0, 0)),
            pl.BlockSpec((1, HEAD_DIM), lambda i: (0, 0)),
            pl.BlockSpec(lb_logits.shape, lambda i: (0, 0)),
        ],
        out_specs=[tok(d), tok(KV_WIDTH), tok(KV_WIDTH), hm, hm, hm3, hm, hm3, hm, hm, tok(d), tok(d)],
        out_shape=[tm_shape(d), tm_shape(KV_WIDTH), tm_shape(KV_WIDTH), hm_shape(1), hm_shape(1),
                   hm_shape(LOG_F_TERMS), hm_shape(1), hm_shape(LOG_F_TERMS), hm_shape(1), hm_shape(1),
                   tm_shape(d), tm_shape(d)],
        compiler_params=_params("arbitrary"),
        name=f"in_projection_l{layer}",
    )(x_lat, x_ctx, mod, mod, gain, w_in, cos_tab, sin_tab, q_norm, k_norm, lb_logits)


def _hgrn_constants(n, c):
    t = lax.broadcasted_iota(jnp.int32, (n, n), 0)
    s = lax.broadcasted_iota(jnp.int32, (n, n), 1)
    tri = ((t >= s).astype(BF16), (t <= s).astype(BF16))
    t = lax.broadcasted_iota(jnp.int32, (c, c), 0)
    s = lax.broadcasted_iota(jnp.int32, (c, c), 1)
    levels = []
    blk = 2
    while blk <= c:
        half = blk // 2
        same = (t // blk) == (s // blk)
        t_hi = (t % blk) >= half
        s_hi = (s % blk) >= half
        levels.append((blk, (same & t_hi & ~s_hi, same & ~t_hi & s_hi)))
        blk *= 2
    return tri, t == s, levels


def _boundary_decay(cum, blk, rev):
    n = cum.shape[0]
    half = blk // 2
    off = half if rev else half - 1
    if blk >= 2 * SUBLANES:
        pieces = []
        for a0 in range(0, n, blk):
            mid = cum[a0 + off:a0 + off + 1, :]
            first, second = cum[a0:a0 + half], cum[a0 + half:a0 + blk]
            pieces += [first - mid, mid - second] if rev else [mid - first, second - mid]
        return jnp.exp2(jnp.concatenate(pieces, axis=0))
    c3 = cum.reshape(n // SUBLANES, SUBLANES, LANES)
    sub = lax.broadcasted_iota(jnp.int32, c3.shape, 1)
    mid = jnp.broadcast_to(c3[:, off:off + 1, :], c3.shape)
    for a0 in range(blk, SUBLANES, blk):
        mid = jnp.where(sub >= a0, jnp.broadcast_to(c3[:, a0 + off:a0 + off + 1, :], c3.shape), mid)
    x = lax.bitcast_convert_type(cum - mid.reshape(n, LANES), jnp.uint32) | jnp.uint32(0x80000000)
    return jnp.exp2(lax.bitcast_convert_type(x, F32))


def _hgrn_block(load, consts, chunk, rev, done):
    tri, eye, levels = consts
    q, k, v, g3, st_t = load()
    n = q.shape[0]
    parts = _dot(tri[rev], g3)
    yield
    cum = parts[:, :LANES]
    for i in range(1, g3.shape[1] // LANES):
        cum = cum + parts[:, i * LANES:(i + 1) * LANES]
    tail = cum[0:1] if rev else cum[n - 1:n]

    qf = q.astype(F32)
    kf = k.astype(F32)
    scaled = lambda t, e: (t * e).astype(BF16)
    out = _dot_nt(scaled(qf, jnp.exp2(cum)), st_t.astype(BF16))
    yield
    k_end = scaled(kf, jnp.exp2(tail - cum))
    st_new = st_t * jnp.exp2(tail) + _dot_tn(v, k_end)
    yield

    blk = n
    far = []
    while blk > chunk:
        half = blk // 2
        e = _boundary_decay(cum, blk, rev)
        for a0 in range(0, n, blk):
            qr = a0 if rev else a0 + half
            kr = a0 + half if rev else a0
            sc = _dot_nt(scaled(qf[qr:qr + half], e[qr:qr + half]), scaled(kf[kr:kr + half], e[kr:kr + half]))
            far.append((qr, half, _dot(sc.astype(BF16), v[kr:kr + half])))
        blk //= 2
    yield

    sides = [(q, k)]
    for blk, _ in levels:
        if blk == 2:
            sides.append((scaled(qf, jnp.exp2(g3[:, :LANES].astype(F32))), k))
        elif blk < 2 * BF16_ROWS:
            e = _boundary_decay(cum, blk, rev)
            sides.append((scaled(qf, e), scaled(kf, e)))
            yield
        else:
            e = _boundary_decay(cum, blk, rev)
            half = blk // 2
            q_l, k_l = [], []
            for a0 in range(0, n, blk):
                first, second = slice(a0, a0 + half), slice(a0 + half, a0 + blk)
                q_rows, k_rows = (first, second) if rev else (second, first)
                scaled_q, scaled_k = scaled(qf[q_rows], e[q_rows]), scaled(kf[k_rows], e[k_rows])
                q_l += [scaled_q, q[second]] if rev else [q[first], scaled_q]
                k_l += [k[first], scaled_k] if rev else [scaled_k, k[second]]
            sides.append((jnp.concatenate(q_l, axis=0), jnp.concatenate(k_l, axis=0)))
            yield
    masks = [eye] + [m[rev] for _, m in levels]
    outs = []
    for c0 in range(0, n, chunk):
        yield
        a = jnp.zeros((chunk, chunk), F32)
        rows = slice(c0, c0 + chunk)
        zero = jnp.zeros((chunk, LANES), BF16)
        for i in range(0, len(sides) - 1, 2):
            (qa, ka), (qb, kb) = sides[i], sides[i + 1]
            keys = jnp.concatenate([jnp.concatenate([ka[rows], zero], axis=1),
                                    jnp.concatenate([zero, kb[rows]], axis=1)], axis=0)
            sc = _dot_nt(jnp.concatenate([qa[rows], qb[rows]], axis=1), keys)
            a = jnp.where(masks[i], sc[:, :chunk], a)
            a = jnp.where(masks[i + 1], sc[:, chunk:], a)
            yield
        if len(sides) % 2:
            a = jnp.where(masks[-1], _dot_nt(sides[-1][0][rows], sides[-1][1][rows]), a)
            yield
        o = out[c0:c0 + chunk] + _dot(a.astype(BF16), v[c0:c0 + chunk])
        for r0, rows, val in far:
            if r0 <= c0 < r0 + rows:
                o = o + val[c0 - r0:c0 - r0 + chunk]
        outs.append(o)
    done(outs[0] if len(outs) == 1 else jnp.concatenate(outs, axis=0), st_new)


def _softmax_tile(q, carry, k, v):
    m, acc = carry
    ones_col = (lax.broadcasted_iota(jnp.int32, v.shape, 1) == 0).astype(BF16)
    s = _dot_nt(q, k)
    m_new = jnp.maximum(m, jnp.max(s, axis=-1, keepdims=True))
    p = jnp.exp2((s - m_new).astype(BF16))
    acc = jnp.exp2(m - m_new) * acc + _dot(p, jnp.concatenate([v, ones_col], axis=1))
    return m_new, acc


def _softmax_init(rows):
    return jnp.full((rows, 1), -1e30, F32), jnp.zeros((rows, 2 * HEAD_DIM), F32)


def _softmax_result(carry):
    acc = carry[1]
    return acc[:, :HEAD_DIM] / acc[:, HEAD_DIM:HEAD_DIM + 1]


def _hgrn_kernel(qf_ref, vf_ref, kf_ref, lf_ref, qb_ref, vb_ref, kb_ref, lb_ref, of_ref, ob_ref,
                 st_ref, *, chunk):
    @pl.when(pl.program_id(1) == 0)
    def _():
        st_ref[...] = jnp.zeros(st_ref.shape, F32)

    consts = _hgrn_constants(qf_ref.shape[1], chunk)

    def block(h, rev):
        q_ref, k_ref, v_ref, l_ref, o_ref = ((qf_ref, kf_ref, vf_ref, lf_ref, of_ref),
                                             (qb_ref, kb_ref, vb_ref, lb_ref, ob_ref))[rev]

        def done(o, st):
            o_ref[h] = o
            st_ref[rev, h] = st
        load = lambda: (q_ref[h], k_ref[h], v_ref[h], l_ref[h], st_ref[rev, h])
        return _hgrn_block(load, consts, chunk, rev, done)

    blocks = [block(h, rev) for h in range(qf_ref.shape[0]) for rev in (0, 1)]
    live = []
    turn = 0
    while blocks or live:
        if blocks and turn % HGRN_STAGGER == 0:
            live.append(blocks.pop(0))
        turn += 1
        for gen in list(live):
            if next(gen, StopIteration) is StopIteration:
                live.remove(gen)


def _hgrn_scan(hq, hi, kf, lf, kb, lb, *, batch, seq, ctx_len):
    heads, t_rows, dh = hq.shape
    ts = HGRN_STEP
    n_ctx = ctx_len // ts
    n_seq = seq // ts
    ctx0 = batch * n_seq

    def fwd(b, s):
        return jnp.where(s < n_ctx, ctx0 + b * n_ctx + s, b * n_seq + s - n_ctx)

    def bwd(b, s):
        return jnp.where(s < n_ctx, ctx0 + b * n_ctx + n_ctx - 1 - s, b * n_seq + n_seq - 1 - (s - n_ctx))

    spec_f = pl.BlockSpec((heads, ts, dh), lambda b, s: (0, fwd(b, s), 0))
    spec_b = pl.BlockSpec((heads, ts, dh), lambda b, s: (0, bwd(b, s), 0))
    log_f = pl.BlockSpec((heads, ts, lf.shape[2]), lambda b, s: (0, fwd(b, s), 0))
    log_b = pl.BlockSpec((heads, ts, lb.shape[2]), lambda b, s: (0, bwd(b, s), 0))
    out = jax.ShapeDtypeStruct((heads, t_rows, dh), F32)
    return pl.pallas_call(
        functools.partial(_hgrn_kernel, chunk=HGRN_CHUNK),
        grid=(batch, n_ctx + n_seq),
        in_specs=[spec_f, spec_f, spec_f, log_f, spec_b, spec_b, spec_b, log_b],
        out_specs=[spec_f, spec_b],
        out_shape=[out, out],
        scratch_shapes=[pltpu.VMEM((2, heads, dh, dh), F32)],
        compiler_params=_params("arbitrary", "arbitrary"),
        name="hgrn_scan",
    )(hq, hi, kf, lf, hq, hi, kb, lb)


def _attn_kernel(q_ref, kc_ref, vc_ref, kl_ref, vl_ref, o_ref, *, n_lat_q_steps, k_tile):
    tq = ATT_Q_TILE
    is_latent = pl.program_id(1) < n_lat_q_steps

    def units(with_latent_keys):
        for g in range(ATT_KV_HEADS):
            kv = slice(g * HEAD_DIM, (g + 1) * HEAD_DIM)
            for r0 in range(0, q_ref.shape[0], tq):
                heads = [(g * ATT_GROUP + i) * HEAD_DIM for i in range(ATT_GROUP)]
                q = jnp.concatenate([q_ref[r0:r0 + tq, c:c + HEAD_DIM] for c in heads], axis=0)
                carry = _softmax_tile(q, _softmax_init(ATT_GROUP * tq), kc_ref[:, kv], vc_ref[:, kv])
                if with_latent_keys:
                    for t in range(0, kl_ref.shape[0], k_tile):
                        carry = _softmax_tile(q, carry, kl_ref[t:t + k_tile, kv], vl_ref[t:t + k_tile, kv])
                out = _softmax_result(carry)
                for i, c in enumerate(heads):
                    o_ref[r0:r0 + tq, c:c + HEAD_DIM] = out[i * tq:(i + 1) * tq].astype(BF16)

    @pl.when(is_latent)
    def _():
        units(True)

    @pl.when(jnp.logical_not(is_latent))
    def _():
        units(False)


def _attention(aq, ak, av, *, batch, seq, ctx_len, with_ctx_queries):
    t_rows, width = aq.shape
    tq = ATT_Q_STEP
    nq_lat = seq // tq
    nq_ctx = ctx_len // tq
    ctx_q0 = batch * nq_lat
    ctx_k0 = batch * seq // ctx_len

    def q_blk(b, i):
        return jnp.where(i < nq_lat, b * nq_lat + i, ctx_q0 + b * nq_ctx + i - nq_lat)

    q_spec = pl.BlockSpec((tq, width), lambda b, i: (q_blk(b, i), 0))
    kc_spec = pl.BlockSpec((ctx_len, KV_WIDTH), lambda b, i: (ctx_k0 + b, 0))
    kl_spec = pl.BlockSpec((seq, KV_WIDTH), lambda b, i: (b, 0))
    return pl.pallas_call(
        functools.partial(_attn_kernel, n_lat_q_steps=nq_lat, k_tile=ATT_K_TILE),
        grid=(batch, nq_lat + (nq_ctx if with_ctx_queries else 0)),
        in_specs=[q_spec, kc_spec, kc_spec, kl_spec, kl_spec],
        out_specs=q_spec,
        out_shape=jax.ShapeDtypeStruct((t_rows if with_ctx_queries else batch * seq, width), BF16),
        compiler_params=_params("arbitrary", "arbitrary"),
        name="gqa_attention",
    )(aq, ak, av, ak, av)


def _merge_kernel(xl_ref, xc_ref, att_ref, of_ref, ob_ref, hg_ref, ga_ref, gh_ref, g1_ref, hgn_ref, post_ref,
                  wa_ref, wh_ref, wo_ref, o_ref, hg_scr, *, n_lat_tiles):
    ya = _dot(att_ref[...], wa_ref[...])
    hgn = hgn_ref[...]
    for i in range(HGRN_HEADS):
        o = _rms(of_ref[i] + ob_ref[i], hgn)
        hg_scr[:, i * HEAD_DIM:(i + 1) * HEAD_DIM] = (o * hg_ref[i].astype(F32)).astype(BF16)
    y = ga_ref[...].astype(F32) * ya + gh_ref[...].astype(F32) * _dot(hg_scr[...], wh_ref[...])
    z = _rms(_dot(y.astype(BF16), wo_ref[...]), post_ref[...])
    x = jnp.where(pl.program_id(0) < n_lat_tiles, xl_ref[...], xc_ref[...])
    o_ref[...] = x + g1_ref[...] * z


def _merge(x_lat, x_ctx, ctx_row0, att, o_f, o_b, hg, ga, gh, mod, hg_norm, post_norm, w_att, w_hg, w_out, *,
           rows, n_lat, seq):
    d = x_lat.shape[1]
    tm = TOKEN_TILE
    n_lat_tiles = n_lat // tm
    per_seq = seq // tm

    def mod_row(i):
        return jnp.where(i < n_lat_tiles, i // per_seq, mod.shape[0] - 1 - 3)

    tok = pl.BlockSpec((tm, d), lambda i: (i, 0))
    hm = pl.BlockSpec((HGRN_HEADS, tm, HEAD_DIM), lambda i: (0, i, 0))
    vec = lambda w: pl.BlockSpec((1, w), lambda i: (0, 0))
    wgt = _resident((d, d), lambda i: (0, 0))
    return pl.pallas_call(
        functools.partial(_merge_kernel, n_lat_tiles=n_lat_tiles),
        grid=(rows // tm,),
        in_specs=[pl.BlockSpec((tm, d), lambda i: (jnp.minimum(i, n_lat_tiles - 1), 0)),
                  pl.BlockSpec((tm, d), lambda i: (ctx_row0 // tm + jnp.maximum(i - n_lat_tiles, 0), 0)),
                  tok, hm, hm, hm, tok, tok,
                  pl.BlockSpec((None, 1, d), lambda i: (mod_row(i), 0, 2)),
                  vec(HEAD_DIM), vec(d), wgt, wgt, wgt],
        out_specs=tok,
        out_shape=jax.ShapeDtypeStruct((rows, d), F32),
        scratch_shapes=[pltpu.VMEM((tm, d), BF16)],
        compiler_params=_params("arbitrary"),
        name="branch_merge",
    )(x_lat, x_ctx, att, o_f, o_b, hg, ga, gh, mod, hg_norm, post_norm, w_att, w_hg, w_out)


def _ffn_kernel(x_ref, sh_ref, sc_ref, g2_ref, pre_ref, post_ref, wg_ref, wu_ref, wd_ref, o_ref, a_scr,
                *, ff_tile):
    x = x_ref[...]
    f = (_rms(x, pre_ref[...]) * (1.0 + sc_ref[...]) + sh_ref[...]).astype(BF16)
    for c in range(0, wg_ref.shape[1], ff_tile):
        g = _dot(f, wg_ref[:, c:c + ff_tile])
        u = _dot(f, wu_ref[:, c:c + ff_tile])
        a_scr[:, c:c + ff_tile] = (_silu(g) * u).astype(BF16)
    y = _rms(_dot(a_scr[...], wd_ref[...]), post_ref[...])
    o_ref[...] = x + g2_ref[...] * y


def _dense_ffn(x, mod, pre_norm, post_norm, w_gate, w_up, w_down, *, n_lat, seq):
    rows, d = x.shape
    d_ff = w_gate.shape[1]
    tm = TOKEN_TILE
    n_lat_tiles = n_lat // tm
    per_seq = seq // tm

    def mod_row(i):
        return jnp.where(i < n_lat_tiles, i // per_seq, mod.shape[0] - 1 - 3)

    tok = pl.BlockSpec((tm, d), lambda i: (i, 0))
    vec = pl.BlockSpec((1, d), lambda i: (0, 0))
    mod_spec = lambda col: pl.BlockSpec((None, 1, d), lambda i: (mod_row(i), 0, col))
    return pl.pallas_call(
        functools.partial(_ffn_kernel, ff_tile=256),
        grid=(rows // tm,),
        in_specs=[tok, mod_spec(3), mod_spec(4), mod_spec(5), vec, vec,
                  _resident((d, d_ff), lambda i: (0, 0)), _resident((d, d_ff), lambda i: (0, 0)),
                  _resident((d_ff, d), lambda i: (0, 0))],
        out_specs=tok,
        out_shape=jax.ShapeDtypeStruct((rows, d), F32),
        scratch_shapes=[pltpu.VMEM((tm, d_ff), BF16)],
        compiler_params=_params("arbitrary"),
        name="dense_swiglu",
    )(x, mod, mod, mod, pre_norm, post_norm, w_gate, w_up, w_down)


def _router_kernel(x_ref, sh_ref, sc_ref, pre_ref, wr_ref, f_ref, route_ref):
    f = _rms(x_ref[...], pre_ref[...]) * (1.0 + sc_ref[...]) + sh_ref[...]
    f_ref[...] = f
    lane = lax.broadcasted_iota(jnp.int32, (f.shape[0], LANES), 1)
    neg = jnp.float32(-jnp.inf)
    logits = jnp.full((f.shape[0], LANES), neg, F32)
    for e in range(wr_ref.shape[0]):
        logits = jnp.where(lane == e, jnp.sum(f * wr_ref[e:e + 1, :], axis=-1, keepdims=True), logits)
    m1 = jnp.max(logits, axis=-1, keepdims=True)
    i1 = jnp.min(jnp.where(logits == m1, lane, LANES), axis=-1, keepdims=True)
    rest = jnp.where(lane == i1, neg, logits)
    m2 = jnp.max(rest, axis=-1, keepdims=True)
    i2 = jnp.min(jnp.where(rest == m2, lane, LANES), axis=-1, keepdims=True)
    w1 = 1.0 / (1.0 + jnp.exp(m2 - m1))
    w2 = 1.0 - w1
    route = jnp.where(lane == 0, i1.astype(F32),
                      jnp.where(lane == 1, i2.astype(F32),
                                jnp.where(lane == 2, w1, jnp.where(lane == 3, w2, 0.0))))
    route_ref[...] = route


def _router(x, mod, pre_norm, w_router_t, *, seq):
    rows, d = x.shape
    tm = TOKEN_TILE
    per_seq = seq // tm
    tok = pl.BlockSpec((tm, d), lambda i: (i, 0))
    mod_spec = lambda col: pl.BlockSpec((None, 1, d), lambda i: (i // per_seq, 0, col))
    return pl.pallas_call(
        _router_kernel,
        grid=(rows // tm,),
        in_specs=[tok, mod_spec(3), mod_spec(4), pl.BlockSpec((1, d), lambda i: (0, 0)),
                  pl.BlockSpec(w_router_t.shape, lambda i: (0, 0))],
        out_specs=[tok, pl.BlockSpec((tm, LANES), lambda i: (i, 0))],
        out_shape=[jax.ShapeDtypeStruct((rows, d), F32), jax.ShapeDtypeStruct((rows, LANES), F32)],
        compiler_params=_params("arbitrary"),
        name="moe_router",
    )(x, mod, mod, pre_norm, w_router_t)


def _expert_kernel(be_ref, tok_ref, dst_ref, f_hbm, wg_ref, wu_ref, wd_ref, ysel_hbm,
                   xbuf, ybuf, a_scr, gsem, ssem, *, ff_tile, n_blocks, spare_row0):
    del be_ref
    s = pl.program_id(0)
    rows = xbuf.shape[1]

    def gather_row(r, slot):
        pltpu.make_async_copy(f_hbm.at[pl.ds(tok_ref[r], 1)], xbuf.at[slot, pl.ds(r, 1)],
                              gsem.at[slot]).start()

    def scatter_row(r, slot, dst_row):
        pltpu.make_async_copy(ybuf.at[slot, pl.ds(r, 1)], ysel_hbm.at[pl.ds(dst_row, 1)],
                              ssem.at[slot]).start()

    def wait_gather(slot):
        pltpu.make_async_copy(f_hbm.at[pl.ds(0, rows)], xbuf.at[slot], gsem.at[slot]).wait()

    def wait_scatter(slot):
        pltpu.make_async_copy(ybuf.at[slot], ysel_hbm.at[pl.ds(0, rows)], ssem.at[slot]).wait()

    @pl.when(s == 0)
    def _():
        ybuf[...] = jnp.zeros(ybuf.shape, F32)

        def body(r, carry):
            gather_row(r, 0)
            scatter_row(r, 0, spare_row0 + r)
            return carry
        lax.fori_loop(0, rows, body, 0)

    @pl.when((s >= 1) & (s <= n_blocks))
    def _():
        slot = (s - 1) % 2
        wait_gather(slot)
        wait_scatter(slot)
        xb = xbuf[slot].astype(BF16)
        chunks = list(range(0, wg_ref.shape[1], ff_tile))
        per = -(-rows // (len(chunks) + 1))

        def move_rows(j):
            for r in range(j * per, min((j + 1) * per, rows)):
                gather_row(r, 1 - slot)
                scatter_row(r, 1 - slot, dst_ref[r])

        for j, c in enumerate(chunks):
            move_rows(j)
            g = _dot(xb, wg_ref[:, c:c + ff_tile])
            u = _dot(xb, wu_ref[:, c:c + ff_tile])
            a_scr[:, c:c + ff_tile] = (_silu(g) * u).astype(BF16)
        move_rows(len(chunks))
        ybuf[slot] = _dot(a_scr[...], wd_ref[...])

    @pl.when(s == n_blocks + 1)
    def _():
        slot = (n_blocks - 1) % 2

        def body(r, carry):
            scatter_row(r, slot, dst_ref[r])
            return carry
        lax.fori_loop(0, rows, body, 0)
        wait_scatter(1 - slot)
        wait_scatter(slot)
        wait_gather(1 - slot)


def _expert_ffn(f, row_tok, row_dst_ext, block_expert, w_gate, w_up, w_down, *, n_out_rows, spare_row0):
    n_rows = row_tok.shape[0]
    d = f.shape[1]
    d_ff = w_gate.shape[2]
    n_blocks = n_rows // MOE_BLOCK
    idx = lambda fn: pl.BlockSpec((MOE_BLOCK,), fn, memory_space=pltpu.SMEM)
    wgt = lambda shape: pl.BlockSpec((None,) + shape,
                                     lambda s, be: (be[jnp.clip(s - 1, 0, n_blocks - 1)], 0, 0),
                                     pipeline_mode=pl.Buffered(1))
    grid_spec = pltpu.PrefetchScalarGridSpec(
        num_scalar_prefetch=1,
        grid=(n_blocks + 2,),
        in_specs=[idx(lambda s, be: (jnp.minimum(s, n_blocks - 1),)),
                  idx(lambda s, be: (jnp.maximum(s - 1, 0),)),
                  pl.BlockSpec(memory_space=pl.ANY),
                  wgt((d, d_ff)), wgt((d, d_ff)), wgt((d_ff, d))],
        out_specs=pl.BlockSpec(memory_space=pl.ANY),
        scratch_shapes=[pltpu.VMEM((2, MOE_BLOCK, d), F32), pltpu.VMEM((2, MOE_BLOCK, d), F32),
                        pltpu.VMEM((MOE_BLOCK, d_ff), BF16),
                        pltpu.SemaphoreType.DMA((2,)), pltpu.SemaphoreType.DMA((2,))],
    )
    return pl.pallas_call(
        functools.partial(_expert_kernel, ff_tile=512, n_blocks=n_blocks, spare_row0=spare_row0),
        grid_spec=grid_spec,
        out_shape=jax.ShapeDtypeStruct((n_out_rows, d), F32),
        compiler_params=_params("arbitrary"),
        name="moe_expert_swiglu",
    )(block_expert, row_tok, row_dst_ext, f, w_gate, w_up, w_down)


def _combine_kernel(y0_ref, y1_ref, x_ref, route_ref, g2_ref, post_ref, o_ref):
    route = route_ref[...]
    y = route[:, 2:3] * y0_ref[...] + route[:, 3:4] * y1_ref[...]
    o_ref[...] = x_ref[...] + g2_ref[...] * _rms(y, post_ref[...])


def _combine(y_sel, x, route, mod, post_norm, *, seq, choice_stride):
    rows, d = x.shape
    tm = TOKEN_TILE
    per_seq = seq // tm
    tok = pl.BlockSpec((tm, d), lambda i: (i, 0))
    return pl.pallas_call(
        _combine_kernel,
        grid=(rows // tm,),
        in_specs=[tok, pl.BlockSpec((tm, d), lambda i: (choice_stride // tm + i, 0)), tok,
                  pl.BlockSpec((tm, LANES), lambda i: (i, 0)),
                  pl.BlockSpec((None, 1, d), lambda i: (i // per_seq, 0, 5)),
                  pl.BlockSpec((1, d), lambda i: (0, 0))],
        out_specs=tok,
        out_shape=jax.ShapeDtypeStruct((rows, d), F32),
        compiler_params=_params("arbitrary"),
        name="moe_combine",
    )(y_sel, y_sel, x, route, mod, post_norm)


def _moe_ffn(x, mod, pre_norm, post_norm, w_router, w_gate, w_up, w_down, *, seq):
    n_tok, d = x.shape
    n_assign = n_tok * TOP_K
    n_blocks = -(-n_assign // MOE_BLOCK) + N_EXPERTS
    n_rows = n_blocks * MOE_BLOCK
    n_pad = n_rows - n_assign
    assert TOP_K == 2 and n_pad % (TOP_K * TOKEN_TILE) == 0
    f, route = _router(x, mod, pre_norm, w_router.T, seq=seq)

    flat_e = route[:, :TOP_K].astype(jnp.int32).reshape(-1)
    onehot = (flat_e[:, None] == jnp.arange(N_EXPERTS, dtype=jnp.int32)[None, :]).astype(jnp.int32)
    csum = jnp.cumsum(onehot, axis=0)
    counts = csum[-1]
    rank = jnp.sum((csum - onehot) * onehot, axis=1)
    padded = (counts + MOE_BLOCK - 1) // MOE_BLOCK * MOE_BLOCK
    pend = jnp.cumsum(padded)
    dest = ((pend - padded)[flat_e] + rank).astype(jnp.int32)
    block_expert = jnp.minimum(
        jnp.searchsorted(pend, jnp.arange(n_blocks, dtype=jnp.int32) * MOE_BLOCK, side='right'),
        N_EXPERTS - 1).astype(jnp.int32)
    choice_stride = n_tok + n_pad // TOP_K
    row_assign = jnp.full((n_rows,), -1, jnp.int32).at[dest].set(jnp.arange(n_assign, dtype=jnp.int32),
                                                                 unique_indices=True)
    is_pad = row_assign < 0
    pad_id = jnp.cumsum(is_pad.astype(jnp.int32)) - 1
    row_tok = jnp.where(is_pad, 0, row_assign // TOP_K)
    row_dst = jnp.where(is_pad, (pad_id % TOP_K) * choice_stride + n_tok + pad_id // TOP_K,
                        (row_assign % TOP_K) * choice_stride + row_assign // TOP_K)
    row_dst_ext = jnp.concatenate([n_rows + jnp.arange(MOE_BLOCK, dtype=jnp.int32), row_dst])

    y_sel = _expert_ffn(f, row_tok, row_dst_ext, block_expert, w_gate, w_up, w_down,
                        n_out_rows=n_rows + 2 * MOE_BLOCK, spare_row0=n_rows + MOE_BLOCK)
    return _combine(y_sel, x, route, mod, post_norm, seq=seq, choice_stride=choice_stride)


def _rope_tables(seq, pad_rows):
    pos = jnp.arange(seq)
    nfreq = HEAD_DIM // 4
    inv_freq = ROPE_THETA ** (-jnp.arange(nfreq, dtype=F32) / nfreq)
    ang = jnp.concatenate([(pos // GRID_W).astype(F32)[:, None] * inv_freq,
                           (pos % GRID_W).astype(F32)[:, None] * inv_freq], axis=-1)
    ang = jnp.concatenate([ang, ang], axis=-1)
    sign = jnp.where(jnp.arange(HEAD_DIM) < HEAD_DIM // 2, -1.0, 1.0).astype(F32)
    cos = jnp.concatenate([jnp.cos(ang), jnp.ones((pad_rows, HEAD_DIM), F32)], axis=0)
    sin = jnp.concatenate([jnp.sin(ang) * sign, jnp.zeros((pad_rows, HEAD_DIM), F32)], axis=0)
    return cos, sin


def kernel(x, c, ctx, c_ctx, w_mod, b_mod, pre_mix_norm, post_mix_norm, pre_ffn_norm, post_ffn_norm, w_in, q_norm, k_norm, hg_norm, hg_lb_logits, w_att_branch, w_hg_branch, w_out, ffn_w_gate, ffn_w_up, ffn_w_down, moe_router, moe_w_gate, moe_w_up, moe_w_down):
    batch, seq, d = x.shape
    ctx_len = ctx.shape[1]
    depth = w_mod.shape[0]
    n_lat = batch * seq
    assert d == D_MODEL and w_in.shape[2] == PROJ_WIDTH
    assert seq % HGRN_STEP == 0 and ctx_len % HGRN_STEP == 0 and n_lat % TOKEN_TILE == 0
    assert seq % ctx_len == 0 and ctx_len % TOKEN_TILE == 0 and seq % ATT_K_TILE == 0

    c_rows = jnp.concatenate([c, c_ctx[None, :], jnp.zeros((3, d), F32)], axis=0)
    mod_all = _mod_vectors(c_rows, w_mod, b_mod)
    cos_tab, sin_tab = _rope_tables(seq, INPROJ_TILE)
    row = lambda v: v.reshape(1, -1)

    t_rows = n_lat + batch * ctx_len
    x_lat, x_ctx, ctx_row0 = x.reshape(n_lat, d), ctx.reshape(batch * ctx_len, d), 0
    for layer in range(depth):
        last = layer == depth - 1
        mod = mod_all[layer].reshape(batch + 4, 1, N_MOD * d)
        (aq, ak, av, hq, kf, lf, kb, lb, hi, hg, ga, gh) = _in_projection(
            x_lat, x_ctx, ctx_row0, mod, row(pre_mix_norm[layer]), w_in[layer].astype(BF16), cos_tab, sin_tab,
            row(q_norm[layer]), row(k_norm[layer]), hg_lb_logits, layer=layer, n_lat=n_lat, t_rows=t_rows,
            seq=seq)
        o_f, o_b = _hgrn_scan(hq, hi, kf, lf, kb, lb, batch=batch, seq=seq, ctx_len=ctx_len)
        att = _attention(aq, ak, av, batch=batch, seq=seq, ctx_len=ctx_len, with_ctx_queries=not last)
        rows = n_lat if last else t_rows
        xt = _merge(x_lat, x_ctx, ctx_row0, att, o_f, o_b, hg, ga, gh, mod, row(hg_norm[layer]), row(post_mix_norm[layer]),
                    w_att_branch[layer].astype(BF16), w_hg_branch[layer].astype(BF16),
                    w_out[layer].astype(BF16), rows=rows, n_lat=n_lat, seq=seq)
        idx = layer // 2
        if layer % 2 == 0:
            xt = _dense_ffn(xt, mod, row(pre_ffn_norm[layer]), row(post_ffn_norm[layer]),
                            ffn_w_gate[idx].astype(BF16), ffn_w_up[idx].astype(BF16),
                            ffn_w_down[idx].astype(BF16), n_lat=n_lat, seq=seq)
            x_lat, x_ctx, ctx_row0 = xt, xt, n_lat
        else:
            assert last, "the expert mixer is implemented for latent tokens only"
            xt = _moe_ffn(xt, mod, row(pre_ffn_norm[layer]), row(post_ffn_norm[layer]), moe_router[idx],
                          moe_w_gate[idx].astype(BF16), moe_w_up[idx].astype(BF16),
                          moe_w_down[idx].astype(BF16), seq=seq)
    return xt[:n_lat].reshape(batch, seq, d)
```

```python
import functools

import jax
import jax.numpy as jnp
from jax import lax
from jax.experimental import pallas as pl
from jax.experimental.pallas import tpu as pltpu

F32 = jnp.float32
BF16 = jnp.bfloat16

D_MODEL = 1024
NORM_EPS = 1e-6
LOG2_E = 1.4426950408889634
N_MOD = 6
GRID_W = 64
ROPE_THETA = 10000.0

HEAD_DIM = 128
ATT_HEADS = 8
ATT_KV_HEADS = 2
ATT_GROUP = ATT_HEADS // ATT_KV_HEADS
KV_WIDTH = ATT_KV_HEADS * HEAD_DIM
HGRN_HEADS = 8

N_EXPERTS = 8
TOP_K = 2
MOE_BLOCK = 256

SUBLANES = 8
LANES = 128
BF16_ROWS = 16
VMEM_LIMIT_BYTES = 56 * 1024 * 1024

TOKEN_TILE = 256
INPROJ_TILE = 256
HGRN_CHUNK = 128
HGRN_STEP = 256
HGRN_STAGGER = 2
LOG_F_TERMS = 2
ATT_Q_TILE = 128
ATT_Q_STEP = 256
ATT_K_TILE = 256

_C_AQ = 0
_C_AK = _C_AQ + ATT_HEADS * HEAD_DIM
_C_AV = _C_AK + KV_WIDTH
_C_HQ = _C_AV + KV_WIDTH
_C_HFF = _C_HQ + D_MODEL
_C_HFB = _C_HFF + D_MODEL
_C_HI = _C_HFB + D_MODEL
_C_HG = _C_HI + D_MODEL
_C_GA = _C_HG + D_MODEL
_C_GH = _C_GA + D_MODEL
PROJ_WIDTH = _C_GH + D_MODEL


def _params(*sem):
    return pltpu.CompilerParams(dimension_semantics=sem, vmem_limit_bytes=VMEM_LIMIT_BYTES)


def _resident(shape, index_map):
    return pl.BlockSpec(shape, index_map, pipeline_mode=pl.Buffered(1))


def _rms(t, gain):
    return t * lax.rsqrt(jnp.mean(t * t, axis=-1, keepdims=True) + NORM_EPS) * gain


def _sigmoid(t):
    return 0.5 * jnp.tanh(0.5 * t) + 0.5


def _silu(t):
    return t * _sigmoid(t)


def _dot(a, b):
    return jnp.dot(a, b, preferred_element_type=F32)


def _dot_nt(a, b):
    return lax.dot_general(a, b, (((1,), (1,)), ((), ())), preferred_element_type=F32)


def _dot_tn(a, b):
    return lax.dot_general(a, b, (((0,), (0,)), ((), ())), preferred_element_type=F32)


def _mod_kernel(c_ref, w_ref, b_ref, o_ref):
    a = _silu(c_ref[...])
    o_ref[...] = jnp.dot(a, w_ref[...], preferred_element_type=F32,
                         precision=lax.Precision.HIGHEST) + b_ref[...]


def _mod_vectors(c_rows, w_mod, b_mod):
    depth, d, width = w_mod.shape
    rows = c_rows.shape[0]
    tn = 1536
    return pl.pallas_call(
        _mod_kernel,
        grid=(depth, width // tn),
        in_specs=[
            pl.BlockSpec((rows, d), lambda l, j: (0, 0)),
            pl.BlockSpec((None, d, tn), lambda l, j: (l, 0, j)),
            pl.BlockSpec((None, 1, tn), lambda l, j: (l, 0, j)),
        ],
        out_specs=pl.BlockSpec((None, rows, tn), lambda l, j: (l, 0, j)),
        out_shape=jax.ShapeDtypeStruct((depth, rows, width), F32),
        compiler_params=_params("arbitrary", "arbitrary"),
        name="mod_vectors",
    )(c_rows, w_mod, b_mod.reshape(depth, 1, width))


def _inproj_kernel(xl_ref, xc_ref, sh_ref, sc_ref, gain_ref, w_ref, cos_ref, sin_ref, qn_ref, kn_ref, lbl_ref,
                   aq_ref, ak_ref, av_ref, hq_ref, kf_ref, lf_ref, kb_ref, lb_ref, hi_ref, hg_ref,
                   ga_ref, gh_ref, *, layer, n_lat_tiles):
    x = jnp.where(pl.program_id(0) < n_lat_tiles, xl_ref[...], xc_ref[...])
    h = _rms(x, gain_ref[...])
    hb = (h * (1.0 + sc_ref[...]) + sh_ref[...]).astype(BF16)
    cos = cos_ref[...]
    sin = sin_ref[...]

    def proj(c0, width):
        return _dot(hb, w_ref[:, c0:c0 + width])

    def head(t, i):
        return t[:, i * HEAD_DIM:(i + 1) * HEAD_DIM]

    def norm_rope(t, gain):
        r = _rms(t, gain)
        return r * cos + pltpu.roll(r, HEAD_DIM // 2, 1) * sin

    lg = lbl_ref[...]
    depth = lg.shape[0]
    mx = lg[0:1]
    for j in range(1, depth):
        mx = jnp.maximum(mx, lg[j:j + 1])
    es = [jnp.exp(lg[j:j + 1] - mx) for j in range(depth)]
    tot = es[0]
    for j in range(1, depth):
        tot = tot + es[j]
    low = jnp.zeros_like(tot)
    for j in range(1, layer + 1):
        low = low + es[j] / tot
    qn = qn_ref[...]
    kn = kn_ref[...]
    half = 4 * HEAD_DIM

    def queries(c):
        def epilogue(t):
            for i in range(4):
                hh = c * 4 + i
                aq_ref[:, hh * HEAD_DIM:(hh + 1) * HEAD_DIM] = (
                    norm_rope(head(t, i), qn) * (HEAD_DIM ** -0.5 * LOG2_E)).astype(BF16)
        return epilogue

    def keys_values(t):
        for i in range(ATT_KV_HEADS):
            ak_ref[:, i * HEAD_DIM:(i + 1) * HEAD_DIM] = norm_rope(head(t, i), kn).astype(BF16)
        av_ref[...] = t[:, KV_WIDTH:].astype(BF16)

    def per_head(ref, c, fn):
        def epilogue(t):
            for i in range(4):
                ref[c * 4 + i] = fn(head(t, i)).astype(BF16)
        return epilogue

    def forget_gate(k_ref, l_ref, c):
        def epilogue(t):
            for i in range(4):
                hh = c * 4 + i
                lo = low[:, hh * HEAD_DIM:(hh + 1) * HEAD_DIM]
                f = lo + (1.0 - lo) * jax.nn.sigmoid(head(t, i))
                k_ref[hh] = (1.0 - f).astype(BF16)
                rest = jnp.log(f) * LOG2_E
                terms = []
                for _ in range(LOG_F_TERMS):
                    terms.append(rest.astype(BF16))
                    rest = rest - terms[-1].astype(F32)
                l_ref[hh] = jnp.concatenate(terms, axis=1)
        return epilogue

    def branch_gate(ref, c):
        def epilogue(t):
            ref[:, c * half:(c + 1) * half] = _sigmoid(t).astype(BF16)
        return epilogue

    groups = [(_C_AQ + c * half, queries(c)) for c in range(2)]
    groups += [(_C_AK, keys_values)]
    groups += [(_C_HQ + c * half, per_head(hq_ref, c, _silu)) for c in range(2)]
    groups += [(_C_HFF + c * half, forget_gate(kf_ref, lf_ref, c)) for c in range(2)]
    groups += [(_C_HFB + c * half, forget_gate(kb_ref, lb_ref, c)) for c in range(2)]
    groups += [(_C_HG + c * half, per_head(hg_ref, c, _silu)) for c in range(2)]
    groups += [(_C_GA + c * half, branch_gate(ga_ref, c)) for c in range(2)]
    groups += [(_C_GH + c * half, branch_gate(gh_ref, c)) for c in range(2)]
    groups += [(_C_HI + c * half, per_head(hi_ref, c, lambda t: t)) for c in range(2)]

    pending = None
    for c0, epilogue in groups:
        t = proj(c0, half)
        if pending is not None:
            pending[1](pending[0])
        pending = (t, epilogue)
    pending[1](pending[0])


def _in_projection(x_lat, x_ctx, ctx_row0, mod, gain, w_in, cos_tab, sin_tab, q_norm, k_norm, lb_logits, *,
                   layer, n_lat, t_rows, seq):
    d = x_lat.shape[1]
    tm = INPROJ_TILE
    assert n_lat % tm == 0 and t_rows % tm == 0 and seq % tm == 0 and ctx_row0 % tm == 0
    n_lat_tiles = n_lat // tm
    per_seq = seq // tm

    def mod_row(i):
        return jnp.where(i < n_lat_tiles, i // per_seq, mod.shape[0] - 1 - 3)

    def pos_blk(i):
        return jnp.where(i < n_lat_tiles, i % per_seq, per_seq)

    tok = lambda w: pl.BlockSpec((tm, w), lambda i: (i, 0))
    hm = pl.BlockSpec((HGRN_HEADS, tm, HEAD_DIM), lambda i: (0, i, 0))
    hm3 = pl.BlockSpec((HGRN_HEADS, tm, LOG_F_TERMS * HEAD_DIM), lambda i: (0, i, 0))
    hm_shape = lambda terms: jax.ShapeDtypeStruct((HGRN_HEADS, t_rows, terms * HEAD_DIM), BF16)
    tm_shape = lambda w: jax.ShapeDtypeStruct((t_rows, w), BF16)
    return pl.pallas_call(
        functools.partial(_inproj_kernel, layer=layer, n_lat_tiles=n_lat_tiles),
        grid=(t_rows // tm,),
        in_specs=[
            pl.BlockSpec((tm, d), lambda i: (jnp.minimum(i, n_lat_tiles - 1), 0)),
            pl.BlockSpec((tm, d), lambda i: (ctx_row0 // tm + jnp.maximum(i - n_lat_tiles, 0), 0)),
            pl.BlockSpec((None, 1, d), lambda i: (mod_row(i), 0, 0)),
            pl.BlockSpec((None, 1, d), lambda i: (mod_row(i), 0, 1)),
            pl.BlockSpec((1, d), lambda i: (0, 0)),
            _resident((d, PROJ_WIDTH), lambda i: (0, 0)),
            pl.BlockSpec((tm, HEAD_DIM), lambda i: (pos_blk(i), 0)),
            pl.BlockSpec((tm, HEAD_DIM), lambda i: (pos_blk(i), 0)),
            pl.BlockSpec((1, HEAD_DIM), lambda i: (0, 0)),
            pl.BlockSpec((1, HEAD_DIM), lambda i: (0, 0)),
            pl.BlockSpec(lb_logits.shape, lambda i: (0, 0)),
        ],
        out_specs=[tok(d), tok(KV_WIDTH), tok(KV_WIDTH), hm, hm, hm3, hm, hm3, hm, hm, tok(d), tok(d)],
        out_shape=[tm_shape(d), tm_shape(KV_WIDTH), tm_shape(KV_WIDTH), hm_shape(1), hm_shape(1),
                   hm_shape(LOG_F_TERMS), hm_shape(1), hm_shape(LOG_F_TERMS), hm_shape(1), hm_shape(1),
                   tm_shape(d), tm_shape(d)],
        compiler_params=_params("arbitrary"),
        name=f"in_projection_l{layer}",
    )(x_lat, x_ctx, mod, mod, gain, w_in, cos_tab, sin_tab, q_norm, k_norm, lb_logits)


def _hgrn_constants(n, c):
    t = lax.broadcasted_iota(jnp.int32, (n, n), 0)
    s = lax.broadcasted_iota(jnp.int32, (n, n), 1)
    tri = ((t >= s).astype(BF16), (t <= s).astype(BF16))
    t = lax.broadcasted_iota(jnp.int32, (c, c), 0)
    s = lax.broadcasted_iota(jnp.int32, (c, c), 1)
    levels = []
    blk = 2
    while blk <= c:
        half = blk // 2
        same = (t // blk) == (s // blk)
        t_hi = (t % blk) >= half
        s_hi = (s % blk) >= half
        levels.append((blk, (same & t_hi & ~s_hi, same & ~t_hi & s_hi)))
        blk *= 2
    return tri, t == s, levels


def _boundary_decay(cum, blk, rev):
    n = cum.shape[0]
    half = blk // 2
    off = half if rev else half - 1
    if blk >= 2 * SUBLANES:
        pieces = []
        for a0 in range(0, n, blk):
            mid = cum[a0 + off:a0 + off + 1, :]
            first, second = cum[a0:a0 + half], cum[a0 + half:a0 + blk]
            pieces += [first - mid, mid - second] if rev else [mid - first, second - mid]
        return jnp.exp2(jnp.concatenate(pieces, axis=0))
    c3 = cum.reshape(n // SUBLANES, SUBLANES, LANES)
    sub = lax.broadcasted_iota(jnp.int32, c3.shape, 1)
    mid = jnp.broadcast_to(c3[:, off:off + 1, :], c3.shape)
    for a0 in range(blk, SUBLANES, blk):
        mid = jnp.where(sub >= a0, jnp.broadcast_to(c3[:, a0 + off:a0 + off + 1, :], c3.shape), mid)
    x = lax.bitcast_convert_type(cum - mid.reshape(n, LANES), jnp.uint32) | jnp.uint32(0x80000000)
    return jnp.exp2(lax.bitcast_convert_type(x, F32))


def _hgrn_block(load, consts, chunk, rev, done):
    tri, eye, levels = consts
    q, k, v, g3, st_t = load()
    n = q.shape[0]
    parts = _dot(tri[rev], g3)
    yield
    cum = parts[:, :LANES]
    for i in range(1, g3.shape[1] // LANES):
        cum = cum + parts[:, i * LANES:(i + 1) * LANES]
    tail = cum[0:1] if rev else cum[n - 1:n]

    qf = q.astype(F32)
    kf = k.astype(F32)
    scaled = lambda t, e: (t * e).astype(BF16)
    out = _dot_nt(scaled(qf, jnp.exp2(cum)), st_t.astype(BF16))
    yield
    k_end = scaled(kf, jnp.exp2(tail - cum))
    st_new = st_t * jnp.exp2(tail) + _dot_tn(v, k_end)
    yield

    blk = n
    far = []
    while blk > chunk:
        half = blk // 2
        e = _boundary_decay(cum, blk, rev)
        for a0 in range(0, n, blk):
            qr = a0 if rev else a0 + half
            kr = a0 + half if rev else a0
            sc = _dot_nt(scaled(qf[qr:qr + half], e[qr:qr + half]), scaled(kf[kr:kr + half], e[kr:kr + half]))
            far.append((qr, half, _dot(sc.astype(BF16), v[kr:kr + half])))
        blk //= 2
    yield

    sides = [(q, k)]
    for blk, _ in levels:
        if blk == 2:
            sides.append((scaled(qf, jnp.exp2(g3[:, :LANES].astype(F32))), k))
        elif blk < 2 * BF16_ROWS:
            e = _boundary_decay(cum, blk, rev)
            sides.append((scaled(qf, e), scaled(kf, e)))
            yield
        else:
            e = _boundary_decay(cum, blk, rev)
            half = blk // 2
            q_l, k_l = [], []
            for a0 in range(0, n, blk):
                first, second = slice(a0, a0 + half), slice(a0 + half, a0 + blk)
                q_rows, k_rows = (first, second) if rev else (second, first)
                scaled_q, scaled_k = scaled(qf[q_rows], e[q_rows]), scaled(kf[k_rows], e[k_rows])
                q_l += [scaled_q, q[second]] if rev else [q[first], scaled_q]
                k_l += [k[first], scaled_k] if rev else [scaled_k, k[second]]
            sides.append((jnp.concatenate(q_l, axis=0), jnp.concatenate(k_l, axis=0)))
            yield
    masks = [eye] + [m[rev] for _, m in levels]
    outs = []
    for c0 in range(0, n, chunk):
        yield
        a = jnp.zeros((chunk, chunk), F32)
        rows = slice(c0, c0 + chunk)
        zero = jnp.zeros((chunk, LANES), BF16)
        for i in range(0, len(sides) - 1, 2):
            (qa, ka), (qb, kb) = sides[i], sides[i + 1]
            keys = jnp.concatenate([jnp.concatenate([ka[rows], zero], axis=1),
                                    jnp.concatenate([zero, kb[rows]], axis=1)], axis=0)
            sc = _dot_nt(jnp.concatenate([qa[rows], qb[rows]], axis=1), keys)
            a = jnp.where(masks[i], sc[:, :chunk], a)
            a = jnp.where(masks[i + 1], sc[:, chunk:], a)
            yield
        if len(sides) % 2:
            a = jnp.where(masks[-1], _dot_nt(sides[-1][0][rows], sides[-1][1][rows]), a)
            yield
        o = out[c0:c0 + chunk] + _dot(a.astype(BF16), v[c0:c0 + chunk])
        for r0, rows, val in far:
            if r0 <= c0 < r0 + rows:
                o = o + val[c0 - r0:c0 - r0 + chunk]
        outs.append(o)
    done(outs[0] if len(outs) == 1 else jnp.concatenate(outs, axis=0), st_new)


def _softmax_tile(q, carry, k, v):
    m, acc = carry
    ones_col = (lax.broadcasted_iota(jnp.int32, v.shape, 1) == 0).astype(BF16)
    s = _dot_nt(q, k)
    m_new = jnp.maximum(m, jnp.max(s, axis=-1, keepdims=True))
    p = jnp.exp2((s - m_new).astype(BF16))
    acc = jnp.exp2(m - m_new) * acc + _dot(p, jnp.concatenate([v, ones_col], axis=1))
    return m_new, acc


def _softmax_init(rows):
    return jnp.full((rows, 1), -1e30, F32), jnp.zeros((rows, 2 * HEAD_DIM), F32)


def _softmax_result(carry):
    acc = carry[1]
    return acc[:, :HEAD_DIM] / acc[:, HEAD_DIM:HEAD_DIM + 1]


def _hgrn_kernel(qf_ref, vf_ref, kf_ref, lf_ref, qb_ref, vb_ref, kb_ref, lb_ref, of_ref, ob_ref,
                 st_ref, *, chunk):
    @pl.when(pl.program_id(1) == 0)
    def _():
        st_ref[...] = jnp.zeros(st_ref.shape, F32)

    consts = _hgrn_constants(qf_ref.shape[1], chunk)

    def block(h, rev):
        q_ref, k_ref, v_ref, l_ref, o_ref = ((qf_ref, kf_ref, vf_ref, lf_ref, of_ref),
                                             (qb_ref, kb_ref, vb_ref, lb_ref, ob_ref))[rev]

        def done(o, st):
            o_ref[h] = o
            st_ref[rev, h] = st
        load = lambda: (q_ref[h], k_ref[h], v_ref[h], l_ref[h], st_ref[rev, h])
        return _hgrn_block(load, consts, chunk, rev, done)

    blocks = [block(h, rev) for h in range(qf_ref.shape[0]) for rev in (0, 1)]
    live = []
    turn = 0
    while blocks or live:
        if blocks and turn % HGRN_STAGGER == 0:
            live.append(blocks.pop(0))
        turn += 1
        for gen in list(live):
            if next(gen, StopIteration) is StopIteration:
                live.remove(gen)


def _hgrn_scan(hq, hi, kf, lf, kb, lb, *, batch, seq, ctx_len):
    heads, t_rows, dh = hq.shape
    ts = HGRN_STEP
    n_ctx = ctx_len // ts
    n_seq = seq // ts
    ctx0 = batch * n_seq

    def fwd(b, s):
        return jnp.where(s < n_ctx, ctx0 + b * n_ctx + s, b * n_seq + s - n_ctx)

    def bwd(b, s):
        return jnp.where(s < n_ctx, ctx0 + b * n_ctx + n_ctx - 1 - s, b * n_seq + n_seq - 1 - (s - n_ctx))

    spec_f = pl.BlockSpec((heads, ts, dh), lambda b, s: (0, fwd(b, s), 0))
    spec_b = pl.BlockSpec((heads, ts, dh), lambda b, s: (0, bwd(b, s), 0))
    log_f = pl.BlockSpec((heads, ts, lf.shape[2]), lambda b, s: (0, fwd(b, s), 0))
    log_b = pl.BlockSpec((heads, ts, lb.shape[2]), lambda b, s: (0, bwd(b, s), 0))
    out = jax.ShapeDtypeStruct((heads, t_rows, dh), F32)
    return pl.pallas_call(
        functools.partial(_hgrn_kernel, chunk=HGRN_CHUNK),
        grid=(batch, n_ctx + n_seq),
        in_specs=[spec_f, spec_f, spec_f, log_f, spec_b, spec_b, spec_b, log_b],
        out_specs=[spec_f, spec_b],
        out_shape=[out, out],
        scratch_shapes=[pltpu.VMEM((2, heads, dh, dh), F32)],
        compiler_params=_params("arbitrary", "arbitrary"),
        name="hgrn_scan",
    )(hq, hi, kf, lf, hq, hi, kb, lb)


def _attn_kernel(q_ref, kc_ref, vc_ref, kl_ref, vl_ref, o_ref, *, n_lat_q_steps, k_tile):
    tq = ATT_Q_TILE
    is_latent = pl.program_id(1) < n_lat_q_steps

    def units(with_latent_keys):
        for g in range(ATT_KV_HEADS):
            kv = slice(g * HEAD_DIM, (g + 1) * HEAD_DIM)
            for r0 in range(0, q_ref.shape[0], tq):
                heads = [(g * ATT_GROUP + i) * HEAD_DIM for i in range(ATT_GROUP)]
                q = jnp.concatenate([q_ref[r0:r0 + tq, c:c + HEAD_DIM] for c in heads], axis=0)
                carry = _softmax_tile(q, _softmax_init(ATT_GROUP * tq), kc_ref[:, kv], vc_ref[:, kv])
                if with_latent_keys:
                    for t in range(0, kl_ref.shape[0], k_tile):
                        carry = _softmax_tile(q, carry, kl_ref[t:t + k_tile, kv], vl_ref[t:t + k_tile, kv])
                out = _softmax_result(carry)
                for i, c in enumerate(heads):
                    o_ref[r0:r0 + tq, c:c + HEAD_DIM] = out[i * tq:(i + 1) * tq].astype(BF16)

    @pl.when(is_latent)
    def _():
        units(True)

    @pl.when(jnp.logical_not(is_latent))
    def _():
        units(False)


def _attention(aq, ak, av, *, batch, seq, ctx_len, with_ctx_queries):
    t_rows, width = aq.shape
    tq = ATT_Q_STEP
    nq_lat = seq // tq
    nq_ctx = ctx_len // tq
    ctx_q0 = batch * nq_lat
    ctx_k0 = batch * seq // ctx_len

    def q_blk(b, i):
        return jnp.where(i < nq_lat, b * nq_lat + i, ctx_q0 + b * nq_ctx + i - nq_lat)

    q_spec = pl.BlockSpec((tq, width), lambda b, i: (q_blk(b, i), 0))
    kc_spec = pl.BlockSpec((ctx_len, KV_WIDTH), lambda b, i: (ctx_k0 + b, 0))
    kl_spec = pl.BlockSpec((seq, KV_WIDTH), lambda b, i: (b, 0))
    return pl.pallas_call(
        functools.partial(_attn_kernel, n_lat_q_steps=nq_lat, k_tile=ATT_K_TILE),
        grid=(batch, nq_lat + (nq_ctx if with_ctx_queries else 0)),
        in_specs=[q_spec, kc_spec, kc_spec, kl_spec, kl_spec],
        out_specs=q_spec,
        out_shape=jax.ShapeDtypeStruct((t_rows if with_ctx_queries else batch * seq, width), BF16),
        compiler_params=_params("arbitrary", "arbitrary"),
        name="gqa_attention",
    )(aq, ak, av, ak, av)


def _merge_kernel(xl_ref, xc_ref, att_ref, of_ref, ob_ref, hg_ref, ga_ref, gh_ref, g1_ref, hgn_ref, post_ref,
                  wa_ref, wh_ref, wo_ref, o_ref, hg_scr, *, n_lat_tiles):
    ya = _dot(att_ref[...], wa_ref[...])
    hgn = hgn_ref[...]
    for i in range(HGRN_HEADS):
        o = _rms(of_ref[i] + ob_ref[i], hgn)
        hg_scr[:, i * HEAD_DIM:(i + 1) * HEAD_DIM] = (o * hg_ref[i].astype(F32)).astype(BF16)
    y = ga_ref[...].astype(F32) * ya + gh_ref[...].astype(F32) * _dot(hg_scr[...], wh_ref[...])
    z = _rms(_dot(y.astype(BF16), wo_ref[...]), post_ref[...])
    x = jnp.where(pl.program_id(0) < n_lat_tiles, xl_ref[...], xc_ref[...])
    o_ref[...] = x + g1_ref[...] * z


def _merge(x_lat, x_ctx, ctx_row0, att, o_f, o_b, hg, ga, gh, mod, hg_norm, post_norm, w_att, w_hg, w_out, *,
           rows, n_lat, seq):
    d = x_lat.shape[1]
    tm = TOKEN_TILE
    n_lat_tiles = n_lat // tm
    per_seq = seq // tm

    def mod_row(i):
        return jnp.where(i < n_lat_tiles, i // per_seq, mod.shape[0] - 1 - 3)

    tok = pl.BlockSpec((tm, d), lambda i: (i, 0))
    hm = pl.BlockSpec((HGRN_HEADS, tm, HEAD_DIM), lambda i: (0, i, 0))
    vec = lambda w: pl.BlockSpec((1, w), lambda i: (0, 0))
    wgt = _resident((d, d), lambda i: (0, 0))
    return pl.pallas_call(
        functools.partial(_merge_kernel, n_lat_tiles=n_lat_tiles),
        grid=(rows // tm,),
        in_specs=[pl.BlockSpec((tm, d), lambda i: (jnp.minimum(i, n_lat_tiles - 1), 0)),
                  pl.BlockSpec((tm, d), lambda i: (ctx_row0 // tm + jnp.maximum(i - n_lat_tiles, 0), 0)),
                  tok, hm, hm, hm, tok, tok,
                  pl.BlockSpec((None, 1, d), lambda i: (mod_row(i), 0, 2)),
                  vec(HEAD_DIM), vec(d), wgt, wgt, wgt],
        out_specs=tok,
        out_shape=jax.ShapeDtypeStruct((rows, d), F32),
        scratch_shapes=[pltpu.VMEM((tm, d), BF16)],
        compiler_params=_params("arbitrary"),
        name="branch_merge",
    )(x_lat, x_ctx, att, o_f, o_b, hg, ga, gh, mod, hg_norm, post_norm, w_att, w_hg, w_out)


def _ffn_kernel(x_ref, sh_ref, sc_ref, g2_ref, pre_ref, post_ref, wg_ref, wu_ref, wd_ref, o_ref, a_scr,
                *, ff_tile):
    x = x_ref[...]
    f = (_rms(x, pre_ref[...]) * (1.0 + sc_ref[...]) + sh_ref[...]).astype(BF16)
    for c in range(0, wg_ref.shape[1], ff_tile):
        g = _dot(f, wg_ref[:, c:c + ff_tile])
        u = _dot(f, wu_ref[:, c:c + ff_tile])
        a_scr[:, c:c + ff_tile] = (_silu(g) * u).astype(BF16)
    y = _rms(_dot(a_scr[...], wd_ref[...]), post_ref[...])
    o_ref[...] = x + g2_ref[...] * y


def _dense_ffn(x, mod, pre_norm, post_norm, w_gate, w_up, w_down, *, n_lat, seq):
    rows, d = x.shape
    d_ff = w_gate.shape[1]
    tm = TOKEN_TILE
    n_lat_tiles = n_lat // tm
    per_seq = seq // tm

    def mod_row(i):
        return jnp.where(i < n_lat_tiles, i // per_seq, mod.shape[0] - 1 - 3)

    tok = pl.BlockSpec((tm, d), lambda i: (i, 0))
    vec = pl.BlockSpec((1, d), lambda i: (0, 0))
    mod_spec = lambda col: pl.BlockSpec((None, 1, d), lambda i: (mod_row(i), 0, col))
    return pl.pallas_call(
        functools.partial(_ffn_kernel, ff_tile=256),
        grid=(rows // tm,),
        in_specs=[tok, mod_spec(3), mod_spec(4), mod_spec(5), vec, vec,
                  _resident((d, d_ff), lambda i: (0, 0)), _resident((d, d_ff), lambda i: (0, 0)),
                  _resident((d_ff, d), lambda i: (0, 0))],
        out_specs=tok,
        out_shape=jax.ShapeDtypeStruct((rows, d), F32),
        scratch_shapes=[pltpu.VMEM((tm, d_ff), BF16)],
        compiler_params=_params("arbitrary"),
        name="dense_swiglu",
    )(x, mod, mod, mod, pre_norm, post_norm, w_gate, w_up, w_down)


def _router_kernel(x_ref, sh_ref, sc_ref, pre_ref, wr_ref, f_ref, route_ref):
    f = _rms(x_ref[...], pre_ref[...]) * (1.0 + sc_ref[...]) + sh_ref[...]
    f_ref[...] = f
    lane = lax.broadcasted_iota(jnp.int32, (f.shape[0], LANES), 1)
    neg = jnp.float32(-jnp.inf)
    logits = jnp.full((f.shape[0], LANES), neg, F32)
    for e in range(wr_ref.shape[0]):
        logits = jnp.where(lane == e, jnp.sum(f * wr_ref[e:e + 1, :], axis=-1, keepdims=True), logits)
    m1 = jnp.max(logits, axis=-1, keepdims=True)
    i1 = jnp.min(jnp.where(logits == m1, lane, LANES), axis=-1, keepdims=True)
    rest = jnp.where(lane == i1, neg, logits)
    m2 = jnp.max(rest, axis=-1, keepdims=True)
    i2 = jnp.min(jnp.where(rest == m2, lane, LANES), axis=-1, keepdims=True)
    w1 = 1.0 / (1.0 + jnp.exp(m2 - m1))
    w2 = 1.0 - w1
    route = jnp.where(lane == 0, i1.astype(F32),
                      jnp.where(lane == 1, i2.astype(F32),
                                jnp.where(lane == 2, w1, jnp.where(lane == 3, w2, 0.0))))
    route_ref[...] = route


def _router(x, mod, pre_norm, w_router_t, *, seq):
    rows, d = x.shape
    tm = TOKEN_TILE
    per_seq = seq // tm
    tok = pl.BlockSpec((tm, d), lambda i: (i, 0))
    mod_spec = lambda col: pl.BlockSpec((None, 1, d), lambda i: (i // per_seq, 0, col))
    return pl.pallas_call(
        _router_kernel,
        grid=(rows // tm,),
        in_specs=[tok, mod_spec(3), mod_spec(4), pl.BlockSpec((1, d), lambda i: (0, 0)),
                  pl.BlockSpec(w_router_t.shape, lambda i: (0, 0))],
        out_specs=[tok, pl.BlockSpec((tm, LANES), lambda i: (i, 0))],
        out_shape=[jax.ShapeDtypeStruct((rows, d), F32), jax.ShapeDtypeStruct((rows, LANES), F32)],
        compiler_params=_params("arbitrary"),
        name="moe_router",
    )(x, mod, mod, pre_norm, w_router_t)


def _expert_kernel(be_ref, tok_ref, dst_ref, f_hbm, wg_ref, wu_ref, wd_ref, ysel_hbm,
                   xbuf, ybuf, a_scr, gsem, ssem, *, ff_tile, n_blocks, spare_row0):
    del be_ref
    s = pl.program_id(0)
    rows = xbuf.shape[1]

    def gather_row(r, slot):
        pltpu.make_async_copy(f_hbm.at[pl.ds(tok_ref[r], 1)], xbuf.at[slot, pl.ds(r, 1)],
                              gsem.at[slot]).start()

    def scatter_row(r, slot, dst_row):
        pltpu.make_async_copy(ybuf.at[slot, pl.ds(r, 1)], ysel_hbm.at[pl.ds(dst_row, 1)],
                              ssem.at[slot]).start()

    def wait_gather(slot):
        pltpu.make_async_copy(f_hbm.at[pl.ds(0, rows)], xbuf.at[slot], gsem.at[slot]).wait()

    def wait_scatter(slot):
        pltpu.make_async_copy(ybuf.at[slot], ysel_hbm.at[pl.ds(0, rows)], ssem.at[slot]).wait()

    @pl.when(s == 0)
    def _():
        ybuf[...] = jnp.zeros(ybuf.shape, F32)

        def body(r, carry):
            gather_row(r, 0)
            scatter_row(r, 0, spare_row0 + r)
            return carry
        lax.fori_loop(0, rows, body, 0)

    @pl.when((s >= 1) & (s <= n_blocks))
    def _():
        slot = (s - 1) % 2
        wait_gather(slot)
        wait_scatter(slot)
        xb = xbuf[slot].astype(BF16)
        chunks = list(range(0, wg_ref.shape[1], ff_tile))
        per = -(-rows // (len(chunks) + 1))

        def move_rows(j):
            for r in range(j * per, min((j + 1) * per, rows)):
                gather_row(r, 1 - slot)
                scatter_row(r, 1 - slot, dst_ref[r])

        for j, c in enumerate(chunks):
            move_rows(j)
            g = _dot(xb, wg_ref[:, c:c + ff_tile])
            u = _dot(xb, wu_ref[:, c:c + ff_tile])
            a_scr[:, c:c + ff_tile] = (_silu(g) * u).astype(BF16)
        move_rows(len(chunks))
        ybuf[slot] = _dot(a_scr[...], wd_ref[...])

    @pl.when(s == n_blocks + 1)
    def _():
        slot = (n_blocks - 1) % 2

        def body(r, carry):
            scatter_row(r, slot, dst_ref[r])
            return carry
        lax.fori_loop(0, rows, body, 0)
        wait_scatter(1 - slot)
        wait_scatter(slot)
        wait_gather(1 - slot)


def _expert_ffn(f, row_tok, row_dst_ext, block_expert, w_gate, w_up, w_down, *, n_out_rows, spare_row0):
    n_rows = row_tok.shape[0]
    d = f.shape[1]
    d_ff = w_gate.shape[2]
    n_blocks = n_rows // MOE_BLOCK
    idx = lambda fn: pl.BlockSpec((MOE_BLOCK,), fn, memory_space=pltpu.SMEM)
    wgt = lambda shape: pl.BlockSpec((None,) + shape,
                                     lambda s, be: (be[jnp.clip(s - 1, 0, n_blocks - 1)], 0, 0),
                                     pipeline_mode=pl.Buffered(1))
    grid_spec = pltpu.PrefetchScalarGridSpec(
        num_scalar_prefetch=1,
        grid=(n_blocks + 2,),
        in_specs=[idx(lambda s, be: (jnp.minimum(s, n_blocks - 1),)),
                  idx(lambda s, be: (jnp.maximum(s - 1, 0),)),
                  pl.BlockSpec(memory_space=pl.ANY),
                  wgt((d, d_ff)), wgt((d, d_ff)), wgt((d_ff, d))],
        out_specs=pl.BlockSpec(memory_space=pl.ANY),
        scratch_shapes=[pltpu.VMEM((2, MOE_BLOCK, d), F32), pltpu.VMEM((2, MOE_BLOCK, d), F32),
                        pltpu.VMEM((MOE_BLOCK, d_ff), BF16),
                        pltpu.SemaphoreType.DMA((2,)), pltpu.SemaphoreType.DMA((2,))],
    )
    return pl.pallas_call(
        functools.partial(_expert_kernel, ff_tile=256, n_blocks=n_blocks, spare_row0=spare_row0),
        grid_spec=grid_spec,
        out_shape=jax.ShapeDtypeStruct((n_out_rows, d), F32),
        compiler_params=_params("arbitrary"),
        name="moe_expert_swiglu",
    )(block_expert, row_tok, row_dst_ext, f, w_gate, w_up, w_down)


def _combine_kernel(y0_ref, y1_ref, x_ref, route_ref, g2_ref, post_ref, o_ref):
    route = route_ref[...]
    y = route[:, 2:3] * y0_ref[...] + route[:, 3:4] * y1_ref[...]
    o_ref[...] = x_ref[...] + g2_ref[...] * _rms(y, post_ref[...])


def _combine(y_sel, x, route, mod, post_norm, *, seq, choice_stride):
    rows, d = x.shape
    tm = TOKEN_TILE
    per_seq = seq // tm
    tok = pl.BlockSpec((tm, d), lambda i: (i, 0))
    return pl.pallas_call(
        _combine_kernel,
        grid=(rows // tm,),
        in_specs=[tok, pl.BlockSpec((tm, d), lambda i: (choice_stride // tm + i, 0)), tok,
                  pl.BlockSpec((tm, LANES), lambda i: (i, 0)),
                  pl.BlockSpec((None, 1, d), lambda i: (i // per_seq, 0, 5)),
                  pl.BlockSpec((1, d), lambda i: (0, 0))],
        out_specs=tok,
        out_shape=jax.ShapeDtypeStruct((rows, d), F32),
        compiler_params=_params("arbitrary"),
        name="moe_combine",
    )(y_sel, y_sel, x, route, mod, post_norm)


def _moe_ffn(x, mod, pre_norm, post_norm, w_router, w_gate, w_up, w_down, *, seq):
    n_tok, d = x.shape
    n_assign = n_tok * TOP_K
    n_blocks = -(-n_assign // MOE_BLOCK) + N_EXPERTS
    n_rows = n_blocks * MOE_BLOCK
    n_pad = n_rows - n_assign
    assert TOP_K == 2 and n_pad % (TOP_K * TOKEN_TILE) == 0
    f, route = _router(x, mod, pre_norm, w_router.T, seq=seq)

    flat_e = route[:, :TOP_K].astype(jnp.int32).reshape(-1)
    onehot = (flat_e[:, None] == jnp.arange(N_EXPERTS, dtype=jnp.int32)[None, :]).astype(jnp.int32)
    csum = jnp.cumsum(onehot, axis=0)
    counts = csum[-1]
    rank = jnp.sum((csum - onehot) * onehot, axis=1)
    padded = (counts + MOE_BLOCK - 1) // MOE_BLOCK * MOE_BLOCK
    pend = jnp.cumsum(padded)
    dest = ((pend - padded)[flat_e] + rank).astype(jnp.int32)
    block_expert = jnp.minimum(
        jnp.searchsorted(pend, jnp.arange(n_blocks, dtype=jnp.int32) * MOE_BLOCK, side='right'),
        N_EXPERTS - 1).astype(jnp.int32)
    choice_stride = n_tok + n_pad // TOP_K
    row_assign = jnp.full((n_rows,), -1, jnp.int32).at[dest].set(jnp.arange(n_assign, dtype=jnp.int32),
                                                                 unique_indices=True)
    is_pad = row_assign < 0
    pad_id = jnp.cumsum(is_pad.astype(jnp.int32)) - 1
    row_tok = jnp.where(is_pad, 0, row_assign // TOP_K)
    row_dst = jnp.where(is_pad, (pad_id % TOP_K) * choice_stride + n_tok + pad_id // TOP_K,
                        (row_assign % TOP_K) * choice_stride + row_assign // TOP_K)
    row_dst_ext = jnp.concatenate([n_rows + jnp.arange(MOE_BLOCK, dtype=jnp.int32), row_dst])

    y_sel = _expert_ffn(f, row_tok, row_dst_ext, block_expert, w_gate, w_up, w_down,
                        n_out_rows=n_rows + 2 * MOE_BLOCK, spare_row0=n_rows + MOE_BLOCK)
    return _combine(y_sel, x, route, mod, post_norm, seq=seq, choice_stride=choice_stride)


def _rope_tables(seq, pad_rows):
    pos = jnp.arange(seq)
    nfreq = HEAD_DIM // 4
    inv_freq = ROPE_THETA ** (-jnp.arange(nfreq, dtype=F32) / nfreq)
    ang = jnp.concatenate([(pos // GRID_W).astype(F32)[:, None] * inv_freq,
                           (pos % GRID_W).astype(F32)[:, None] * inv_freq], axis=-1)
    ang = jnp.concatenate([ang, ang], axis=-1)
    sign = jnp.where(jnp.arange(HEAD_DIM) < HEAD_DIM // 2, -1.0, 1.0).astype(F32)
    cos = jnp.concatenate([jnp.cos(ang), jnp.ones((pad_rows, HEAD_DIM), F32)], axis=0)
    sin = jnp.concatenate([jnp.sin(ang) * sign, jnp.zeros((pad_rows, HEAD_DIM), F32)], axis=0)
    return cos, sin


def kernel(x, c, ctx, c_ctx, w_mod, b_mod, pre_mix_norm, post_mix_norm, pre_ffn_norm, post_ffn_norm, w_in, q_norm, k_norm, hg_norm, hg_lb_logits, w_att_branch, w_hg_branch, w_out, ffn_w_gate, ffn_w_up, ffn_w_down, moe_router, moe_w_gate, moe_w_up, moe_w_down):
    batch, seq, d = x.shape
    ctx_len = ctx.shape[1]
    depth = w_mod.shape[0]
    n_lat = batch * seq
    assert d == D_MODEL and w_in.shape[2] == PROJ_WIDTH
    assert seq % HGRN_STEP == 0 and ctx_len % HGRN_STEP == 0 and n_lat % TOKEN_TILE == 0
    assert seq % ctx_len == 0 and ctx_len % TOKEN_TILE == 0 and seq % ATT_K_TILE == 0

    c_rows = jnp.concatenate([c, c_ctx[None, :], jnp.zeros((3, d), F32)], axis=0)
    mod_all = _mod_vectors(c_rows, w_mod, b_mod)
    cos_tab, sin_tab = _rope_tables(seq, INPROJ_TILE)
    row = lambda v: v.reshape(1, -1)

    t_rows = n_lat + batch * ctx_len
    x_lat, x_ctx, ctx_row0 = x.reshape(n_lat, d), ctx.reshape(batch * ctx_len, d), 0
    for layer in range(depth):
        last = layer == depth - 1
        mod = mod_all[layer].reshape(batch + 4, 1, N_MOD * d)
        (aq, ak, av, hq, kf, lf, kb, lb, hi, hg, ga, gh) = _in_projection(
            x_lat, x_ctx, ctx_row0, mod, row(pre_mix_norm[layer]), w_in[layer].astype(BF16), cos_tab, sin_tab,
            row(q_norm[layer]), row(k_norm[layer]), hg_lb_logits, layer=layer, n_lat=n_lat, t_rows=t_rows,
            seq=seq)
        o_f, o_b = _hgrn_scan(hq, hi, kf, lf, kb, lb, batch=batch, seq=seq, ctx_len=ctx_len)
        att = _attention(aq, ak, av, batch=batch, seq=seq, ctx_len=ctx_len, with_ctx_queries=not last)
        rows = n_lat if last else t_rows
        xt = _merge(x_lat, x_ctx, ctx_row0, att, o_f, o_b, hg, ga, gh, mod, row(hg_norm[layer]), row(post_mix_norm[layer]),
                    w_att_branch[layer].astype(BF16), w_hg_branch[layer].astype(BF16),
                    w_out[layer].astype(BF16), rows=rows, n_lat=n_lat, seq=seq)
        idx = layer // 2
        if layer % 2 == 0:
            xt = _dense_ffn(xt, mod, row(pre_ffn_norm[layer]), row(post_ffn_norm[layer]),
                            ffn_w_gate[idx].astype(BF16), ffn_w_up[idx].astype(BF16),
                            ffn_w_down[idx].astype(BF16), n_lat=n_lat, seq=seq)
            x_lat, x_ctx, ctx_row0 = xt, xt, n_lat
        else:
            assert last, "the expert mixer is implemented for latent tokens only"
            xt = _moe_ffn(xt, mod, row(pre_ffn_norm[layer]), row(post_ffn_norm[layer]), moe_router[idx],
                          moe_w_gate[idx].astype(BF16), moe_w_up[idx].astype(BF16),
                          moe_w_down[idx].astype(BF16), seq=seq)
    return xt[:n_lat].reshape(batch, seq, d)
```

```python
import functools

import jax
import jax.numpy as jnp
from jax import lax
from jax.experimental import pallas as pl
from jax.experimental.pallas import tpu as pltpu

F32 = jnp.float32
BF16 = jnp.bfloat16

D_MODEL = 1024
NORM_EPS = 1e-6
LOG2_E = 1.4426950408889634
N_MOD = 6
GRID_W = 64
ROPE_THETA = 10000.0

HEAD_DIM = 128
ATT_HEADS = 8
ATT_KV_HEADS = 2
ATT_GROUP = ATT_HEADS // ATT_KV_HEADS
KV_WIDTH = ATT_KV_HEADS * HEAD_DIM
HGRN_HEADS = 8

N_EXPERTS = 8
TOP_K = 2
MOE_BLOCK = 256

SUBLANES = 8
LANES = 128
BF16_ROWS = 16
VMEM_LIMIT_BYTES = 56 * 1024 * 1024

TOKEN_TILE = 256
INPROJ_TILE = 256
HGRN_CHUNK = 128
HGRN_STEP = 256
HGRN_STAGGER = 2
LOG_F_TERMS = 2
ATT_Q_TILE = 128
ATT_Q_STEP = 256
ATT_K_TILE = 256

_C_AQ = 0
_C_AK = _C_AQ + ATT_HEADS * HEAD_DIM
_C_AV = _C_AK + KV_WIDTH
_C_HQ = _C_AV + KV_WIDTH
_C_HFF = _C_HQ + D_MODEL
_C_HFB = _C_HFF + D_MODEL
_C_HI = _C_HFB + D_MODEL
_C_HG = _C_HI + D_MODEL
_C_GA = _C_HG + D_MODEL
_C_GH = _C_GA + D_MODEL
PROJ_WIDTH = _C_GH + D_MODEL


def _params(*sem):
    return pltpu.CompilerParams(dimension_semantics=sem, vmem_limit_bytes=VMEM_LIMIT_BYTES)


def _resident(shape, index_map):
    return pl.BlockSpec(shape, index_map, pipeline_mode=pl.Buffered(1))


def _rms(t, gain):
    return t * lax.rsqrt(jnp.mean(t * t, axis=-1, keepdims=True) + NORM_EPS) * gain


def _sigmoid(t):
    return 0.5 * jnp.tanh(0.5 * t) + 0.5


def _silu(t):
    return t * _sigmoid(t)


def _dot(a, b):
    return jnp.dot(a, b, preferred_element_type=F32)


def _dot_nt(a, b):
    return lax.dot_general(a, b, (((1,), (1,)), ((), ())), preferred_element_type=F32)


def _dot_tn(a, b):
    return lax.dot_general(a, b, (((0,), (0,)), ((), ())), preferred_element_type=F32)


def _mod_kernel(c_ref, w_ref, b_ref, o_ref):
    a = _silu(c_ref[...])
    o_ref[...] = jnp.dot(a, w_ref[...], preferred_element_type=F32,
                         precision=lax.Precision.HIGHEST) + b_ref[...]


def _mod_vectors(c_rows, w_mod, b_mod):
    depth, d, width = w_mod.shape
    rows = c_rows.shape[0]
    tn = 1536
    return pl.pallas_call(
        _mod_kernel,
        grid=(depth, width // tn),
        in_specs=[
            pl.BlockSpec((rows, d), lambda l, j: (0, 0)),
            pl.BlockSpec((None, d, tn), lambda l, j: (l, 0, j)),
            pl.BlockSpec((None, 1, tn), lambda l, j: (l, 0, j)),
        ],
        out_specs=pl.BlockSpec((None, rows, tn), lambda l, j: (l, 0, j)),
        out_shape=jax.ShapeDtypeStruct((depth, rows, width), F32),
        compiler_params=_params("arbitrary", "arbitrary"),
        name="mod_vectors",
    )(c_rows, w_mod, b_mod.reshape(depth, 1, width))


def _inproj_kernel(xl_ref, xc_ref, sh_ref, sc_ref, gain_ref, w_ref, cos_ref, sin_ref, qn_ref, kn_ref, lbl_ref,
                   aq_ref, ak_ref, av_ref, hq_ref, kf_ref, lf_ref, kb_ref, lb_ref, hi_ref, hg_ref,
                   ga_ref, gh_ref, *, layer, n_lat_tiles):
    x = jnp.where(pl.program_id(0) < n_lat_tiles, xl_ref[...], xc_ref[...])
    h = _rms(x, gain_ref[...])
    hb = (h * (1.0 + sc_ref[...]) + sh_ref[...]).astype(BF16)
    cos = cos_ref[...]
    sin = sin_ref[...]

    def proj(c0, width):
        return _dot(hb, w_ref[:, c0:c0 + width])

    def head(t, i):
        return t[:, i * HEAD_DIM:(i + 1) * HEAD_DIM]

    def norm_rope(t, gain):
        r = _rms(t, gain)
        return r * cos + pltpu.roll(r, HEAD_DIM // 2, 1) * sin

    lg = lbl_ref[...]
    depth = lg.shape[0]
    mx = lg[0:1]
    for j in range(1, depth):
        mx = jnp.maximum(mx, lg[j:j + 1])
    es = [jnp.exp(lg[j:j + 1] - mx) for j in range(depth)]
    tot = es[0]
    for j in range(1, depth):
        tot = tot + es[j]
    low = jnp.zeros_like(tot)
    for j in range(1, layer + 1):
        low = low + es[j] / tot
    qn = qn_ref[...]
    kn = kn_ref[...]
    half = 4 * HEAD_DIM

    def queries(c):
        def epilogue(t):
            for i in range(4):
                hh = c * 4 + i
                aq_ref[:, hh * HEAD_DIM:(hh + 1) * HEAD_DIM] = (
                    norm_rope(head(t, i), qn) * (HEAD_DIM ** -0.5 * LOG2_E)).astype(BF16)
        return epilogue

    def keys_values(t):
        for i in range(ATT_KV_HEADS):
            ak_ref[:, i * HEAD_DIM:(i + 1) * HEAD_DIM] = norm_rope(head(t, i), kn).astype(BF16)
        av_ref[...] = t[:, KV_WIDTH:].astype(BF16)

    def per_head(ref, c, fn):
        def epilogue(t):
            for i in range(4):
                ref[c * 4 + i] = fn(head(t, i)).astype(BF16)
        return epilogue

    def forget_gate(k_ref, l_ref, c):
        def epilogue(t):
            for i in range(4):
                hh = c * 4 + i
                lo = low[:, hh * HEAD_DIM:(hh + 1) * HEAD_DIM]
                f = lo + (1.0 - lo) * jax.nn.sigmoid(head(t, i))
                k_ref[hh] = (1.0 - f).astype(BF16)
                rest = jnp.log(f) * LOG2_E
                terms = []
                for _ in range(LOG_F_TERMS):
                    terms.append(rest.astype(BF16))
                    rest = rest - terms[-1].astype(F32)
                l_ref[hh] = jnp.concatenate(terms, axis=1)
        return epilogue

    def branch_gate(ref, c):
        def epilogue(t):
            ref[:, c * half:(c + 1) * half] = _sigmoid(t).astype(BF16)
        return epilogue

    groups = [(_C_AQ + c * half, queries(c)) for c in range(2)]
    groups += [(_C_AK, keys_values)]
    groups += [(_C_HQ + c * half, per_head(hq_ref, c, _silu)) for c in range(2)]
    groups += [(_C_HFF + c * half, forget_gate(kf_ref, lf_ref, c)) for c in range(2)]
    groups += [(_C_HFB + c * half, forget_gate(kb_ref, lb_ref, c)) for c in range(2)]
    groups += [(_C_HG + c * half, per_head(hg_ref, c, _silu)) for c in range(2)]
    groups += [(_C_GA + c * half, branch_gate(ga_ref, c)) for c in range(2)]
    groups += [(_C_GH + c * half, branch_gate(gh_ref, c)) for c in range(2)]
    groups += [(_C_HI + c * half, per_head(hi_ref, c, lambda t: t)) for c in range(2)]

    pending = None
    for c0, epilogue in groups:
        t = proj(c0, half)
        if pending is not None:
            pending[1](pending[0])
        pending = (t, epilogue)
    pending[1](pending[0])


def _in_projection(x_lat, x_ctx, ctx_row0, mod, gain, w_in, cos_tab, sin_tab, q_norm, k_norm, lb_logits, *,
                   layer, n_lat, t_rows, seq):
    d = x_lat.shape[1]
    tm = INPROJ_TILE
    assert n_lat % tm == 0 and t_rows % tm == 0 and seq % tm == 0 and ctx_row0 % tm == 0
    n_lat_tiles = n_lat // tm
    per_seq = seq // tm

    def mod_row(i):
        return jnp.where(i < n_lat_tiles, i // per_seq, mod.shape[0] - 1 - 3)

    def pos_blk(i):
        return jnp.where(i < n_lat_tiles, i % per_seq, per_seq)

    tok = lambda w: pl.BlockSpec((tm, w), lambda i: (i, 0))
    hm = pl.BlockSpec((HGRN_HEADS, tm, HEAD_DIM), lambda i: (0, i, 0))
    hm3 = pl.BlockSpec((HGRN_HEADS, tm, LOG_F_TERMS * HEAD_DIM), lambda i: (0, i, 0))
    hm_shape = lambda terms: jax.ShapeDtypeStruct((HGRN_HEADS, t_rows, terms * HEAD_DIM), BF16)
    tm_shape = lambda w: jax.ShapeDtypeStruct((t_rows, w), BF16)
    return pl.pallas_call(
        functools.partial(_inproj_kernel, layer=layer, n_lat_tiles=n_lat_tiles),
        grid=(t_rows // tm,),
        in_specs=[
            pl.BlockSpec((tm, d), lambda i: (jnp.minimum(i, n_lat_tiles - 1), 0)),
            pl.BlockSpec((tm, d), lambda i: (ctx_row0 // tm + jnp.maximum(i - n_lat_tiles, 0), 0)),
            pl.BlockSpec((None, 1, d), lambda i: (mod_row(i), 0, 0)),
            pl.BlockSpec((None, 1, d), lambda i: (mod_row(i), 0, 1)),
            pl.BlockSpec((1, d), lambda i: (0, 0)),
            _resident((d, PROJ_WIDTH), lambda i: (0, 0)),
            pl.BlockSpec((tm, HEAD_DIM), lambda i: (pos_blk(i), 0)),
            pl.BlockSpec((tm, HEAD_DIM), lambda i: (pos_blk(i), 0)),
            pl.BlockSpec((1, HEAD_DIM), lambda i: (0, 0)),
            pl.BlockSpec((1, HEAD_DIM), lambda i: (0, 0)),
            pl.BlockSpec(lb_logits.shape, lambda i: (0, 0)),
        ],
        out_specs=[tok(d), tok(KV_WIDTH), tok(KV_WIDTH), hm, hm, hm3, hm, hm3, hm, hm, tok(d), tok(d)],
        out_shape=[tm_shape(d), tm_shape(KV_WIDTH), tm_shape(KV_WIDTH), hm_shape(1), hm_shape(1),
                   hm_shape(LOG_F_TERMS), hm_shape(1), hm_shape(LOG_F_TERMS), hm_shape(1), hm_shape(1),
                   tm_shape(d), tm_shape(d)],
        compiler_params=_params("arbitrary"),
        name=f"in_projection_l{layer}",
    )(x_lat, x_ctx, mod, mod, gain, w_in, cos_tab, sin_tab, q_norm, k_norm, lb_logits)


def _hgrn_constants(n, c):
    t = lax.broadcasted_iota(jnp.int32, (n, n), 0)
    s = lax.broadcasted_iota(jnp.int32, (n, n), 1)
    tri = ((t >= s).astype(BF16), (t <= s).astype(BF16))
    t = lax.broadcasted_iota(jnp.int32, (c, c), 0)
    s = lax.broadcasted_iota(jnp.int32, (c, c), 1)
    levels = []
    blk = 2
    while blk <= c:
        half = blk // 2
        same = (t // blk) == (s // blk)
        t_hi = (t % blk) >= half
        s_hi = (s % blk) >= half
        levels.append((blk, (same & t_hi & ~s_hi, same & ~t_hi & s_hi)))
        blk *= 2
    return tri, t == s, levels


def _boundary_decay(cum, blk, rev):
    n = cum.shape[0]
    half = blk // 2
    off = half if rev else half - 1
    if blk >= 2 * SUBLANES:
        pieces = []
        for a0 in range(0, n, blk):
            mid = cum[a0 + off:a0 + off + 1, :]
            first, second = cum[a0:a0 + half], cum[a0 + half:a0 + blk]
            pieces += [first - mid, mid - second] if rev else [mid - first, second - mid]
        return jnp.exp2(jnp.concatenate(pieces, axis=0))
    c3 = cum.reshape(n // SUBLANES, SUBLANES, LANES)
    sub = lax.broadcasted_iota(jnp.int32, c3.shape, 1)
    mid = jnp.broadcast_to(c3[:, off:off + 1, :], c3.shape)
    for a0 in range(blk, SUBLANES, blk):
        mid = jnp.where(sub >= a0, jnp.broadcast_to(c3[:, a0 + off:a0 + off + 1, :], c3.shape), mid)
    x = lax.bitcast_convert_type(cum - mid.reshape(n, LANES), jnp.uint32) | jnp.uint32(0x80000000)
    return jnp.exp2(lax.bitcast_convert_type(x, F32))


def _hgrn_block(load, consts, chunk, rev, done):
    tri, eye, levels = consts
    q, k, v, g3, st_t = load()
    n = q.shape[0]
    parts = _dot(tri[rev], g3)
    yield
    cum = parts[:, :LANES]
    for i in range(1, g3.shape[1] // LANES):
        cum = cum + parts[:, i * LANES:(i + 1) * LANES]
    tail = cum[0:1] if rev else cum[n - 1:n]

    qf = q.astype(F32)
    kf = k.astype(F32)
    scaled = lambda t, e: (t * e).astype(BF16)
    out = _dot_nt(scaled(qf, jnp.exp2(cum)), st_t.astype(BF16))
    yield
    k_end = scaled(kf, jnp.exp2(tail - cum))
    st_new = st_t * jnp.exp2(tail) + _dot_tn(v, k_end)
    yield

    blk = n
    far = []
    while blk > chunk:
        half = blk // 2
        e = _boundary_decay(cum, blk, rev)
        for a0 in range(0, n, blk):
            qr = a0 if rev else a0 + half
            kr = a0 + half if rev else a0
            sc = _dot_nt(scaled(qf[qr:qr + half], e[qr:qr + half]), scaled(kf[kr:kr + half], e[kr:kr + half]))
            far.append((qr, half, _dot(sc.astype(BF16), v[kr:kr + half])))
        blk //= 2
    yield

    sides = [(q, k)]
    for blk, _ in levels:
        if blk == 2:
            sides.append((scaled(qf, jnp.exp2(g3[:, :LANES].astype(F32))), k))
        elif blk < 2 * BF16_ROWS:
            e = _boundary_decay(cum, blk, rev)
            sides.append((scaled(qf, e), scaled(kf, e)))
            yield
        else:
            e = _boundary_decay(cum, blk, rev)
            half = blk // 2
            q_l, k_l = [], []
            for a0 in range(0, n, blk):
                first, second = slice(a0, a0 + half), slice(a0 + half, a0 + blk)
                q_rows, k_rows = (first, second) if rev else (second, first)
                scaled_q, scaled_k = scaled(qf[q_rows], e[q_rows]), scaled(kf[k_rows], e[k_rows])
                q_l += [scaled_q, q[second]] if rev else [q[first], scaled_q]
                k_l += [k[first], scaled_k] if rev else [scaled_k, k[second]]
            sides.append((jnp.concatenate(q_l, axis=0), jnp.concatenate(k_l, axis=0)))
            yield
    masks = [eye] + [m[rev] for _, m in levels]
    outs = []
    for c0 in range(0, n, chunk):
        yield
        a = jnp.zeros((chunk, chunk), F32)
        rows = slice(c0, c0 + chunk)
        zero = jnp.zeros((chunk, LANES), BF16)
        for i in range(0, len(sides) - 1, 2):
            (qa, ka), (qb, kb) = sides[i], sides[i + 1]
            keys = jnp.concatenate([jnp.concatenate([ka[rows], zero], axis=1),
                                    jnp.concatenate([zero, kb[rows]], axis=1)], axis=0)
            sc = _dot_nt(jnp.concatenate([qa[rows], qb[rows]], axis=1), keys)
            a = jnp.where(masks[i], sc[:, :chunk], a)
            a = jnp.where(masks[i + 1], sc[:, chunk:], a)
            yield
        if len(sides) % 2:
            a = jnp.where(masks[-1], _dot_nt(sides[-1][0][rows], sides[-1][1][rows]), a)
            yield
        o = out[c0:c0 + chunk] + _dot(a.astype(BF16), v[c0:c0 + chunk])
        for r0, rows, val in far:
            if r0 <= c0 < r0 + rows:
                o = o + val[c0 - r0:c0 - r0 + chunk]
        outs.append(o)
    done(outs[0] if len(outs) == 1 else jnp.concatenate(outs, axis=0), st_new)


def _softmax_tile(q, carry, k, v):
    m, acc = carry
    ones_col = (lax.broadcasted_iota(jnp.int32, v.shape, 1) == 0).astype(BF16)
    s = _dot_nt(q, k)
    m_new = jnp.maximum(m, jnp.max(s, axis=-1, keepdims=True))
    p = jnp.exp2((s - m_new).astype(BF16))
    acc = jnp.exp2(m - m_new) * acc + _dot(p, jnp.concatenate([v, ones_col], axis=1))
    return m_new, acc


def _softmax_init(rows):
    return jnp.full((rows, 1), -1e30, F32), jnp.zeros((rows, 2 * HEAD_DIM), F32)


def _softmax_result(carry):
    acc = carry[1]
    return acc[:, :HEAD_DIM] / acc[:, HEAD_DIM:HEAD_DIM + 1]


def _hgrn_kernel(qf_ref, vf_ref, kf_ref, lf_ref, qb_ref, vb_ref, kb_ref, lb_ref, of_ref, ob_ref,
                 st_ref, *, chunk):
    @pl.when(pl.program_id(1) == 0)
    def _():
        st_ref[...] = jnp.zeros(st_ref.shape, F32)

    consts = _hgrn_constants(qf_ref.shape[1], chunk)

    def block(h, rev):
        q_ref, k_ref, v_ref, l_ref, o_ref = ((qf_ref, kf_ref, vf_ref, lf_ref, of_ref),
                                             (qb_ref, kb_ref, vb_ref, lb_ref, ob_ref))[rev]

        def done(o, st):
            o_ref[h] = o
            st_ref[rev, h] = st
        load = lambda: (q_ref[h], k_ref[h], v_ref[h], l_ref[h], st_ref[rev, h])
        return _hgrn_block(load, consts, chunk, rev, done)

    blocks = [block(h, rev) for h in range(qf_ref.shape[0]) for rev in (0, 1)]
    live = []
    turn = 0
    while blocks or live:
        if blocks and turn % HGRN_STAGGER == 0:
            live.append(blocks.pop(0))
        turn += 1
        for gen in list(live):
            if next(gen, StopIteration) is StopIteration:
                live.remove(gen)


def _hgrn_scan(hq, hi, kf, lf, kb, lb, *, batch, seq, ctx_len):
    heads, t_rows, dh = hq.shape
    ts = HGRN_STEP
    n_ctx = ctx_len // ts
    n_seq = seq // ts
    ctx0 = batch * n_seq

    def fwd(b, s):
        return jnp.where(s < n_ctx, ctx0 + b * n_ctx + s, b * n_seq + s - n_ctx)

    def bwd(b, s):
        return jnp.where(s < n_ctx, ctx0 + b * n_ctx + n_ctx - 1 - s, b * n_seq + n_seq - 1 - (s - n_ctx))

    spec_f = pl.BlockSpec((heads, ts, dh), lambda b, s: (0, fwd(b, s), 0))
    spec_b = pl.BlockSpec((heads, ts, dh), lambda b, s: (0, bwd(b, s), 0))
    log_f = pl.BlockSpec((heads, ts, lf.shape[2]), lambda b, s: (0, fwd(b, s), 0))
    log_b = pl.BlockSpec((heads, ts, lb.shape[2]), lambda b, s: (0, bwd(b, s), 0))
    out = jax.ShapeDtypeStruct((heads, t_rows, dh), F32)
    return pl.pallas_call(
        functools.partial(_hgrn_kernel, chunk=HGRN_CHUNK),
        grid=(batch, n_ctx + n_seq),
        in_specs=[spec_f, spec_f, spec_f, log_f, spec_b, spec_b, spec_b, log_b],
        out_specs=[spec_f, spec_b],
        out_shape=[out, out],
        scratch_shapes=[pltpu.VMEM((2, heads, dh, dh), F32)],
        compiler_params=_params("arbitrary", "arbitrary"),
        name="hgrn_scan",
    )(hq, hi, kf, lf, hq, hi, kb, lb)


def _attn_kernel(q_ref, kc_ref, vc_ref, kl_ref, vl_ref, o_ref, *, n_lat_q_steps, k_tile):
    tq = ATT_Q_TILE
    is_latent = pl.program_id(1) < n_lat_q_steps

    def units(with_latent_keys):
        for g in range(ATT_KV_HEADS):
            kv = slice(g * HEAD_DIM, (g + 1) * HEAD_DIM)
            for r0 in range(0, q_ref.shape[0], tq):
                heads = [(g * ATT_GROUP + i) * HEAD_DIM for i in range(ATT_GROUP)]
                q = jnp.concatenate([q_ref[r0:r0 + tq, c:c + HEAD_DIM] for c in heads], axis=0)
                carry = _softmax_tile(q, _softmax_init(ATT_GROUP * tq), kc_ref[:, kv], vc_ref[:, kv])
                if with_latent_keys:
                    for t in range(0, kl_ref.shape[0], k_tile):
                        carry = _softmax_tile(q, carry, kl_ref[t:t + k_tile, kv], vl_ref[t:t + k_tile, kv])
                out = _softmax_result(carry)
                for i, c in enumerate(heads):
                    o_ref[r0:r0 + tq, c:c + HEAD_DIM] = out[i * tq:(i + 1) * tq].astype(BF16)

    @pl.when(is_latent)
    def _():
        units(True)

    @pl.when(jnp.logical_not(is_latent))
    def _():
        units(False)


def _attention(aq, ak, av, *, batch, seq, ctx_len, with_ctx_queries):
    t_rows, width = aq.shape
    tq = ATT_Q_STEP
    nq_lat = seq // tq
    nq_ctx = ctx_len // tq
    ctx_q0 = batch * nq_lat
    ctx_k0 = batch * seq // ctx_len

    def q_blk(b, i):
        return jnp.where(i < nq_lat, b * nq_lat + i, ctx_q0 + b * nq_ctx + i - nq_lat)

    q_spec = pl.BlockSpec((tq, width), lambda b, i: (q_blk(b, i), 0))
    kc_spec = pl.BlockSpec((ctx_len, KV_WIDTH), lambda b, i: (ctx_k0 + b, 0))
    kl_spec = pl.BlockSpec((seq, KV_WIDTH), lambda b, i: (b, 0))
    return pl.pallas_call(
        functools.partial(_attn_kernel, n_lat_q_steps=nq_lat, k_tile=ATT_K_TILE),
        grid=(batch, nq_lat + (nq_ctx if with_ctx_queries else 0)),
        in_specs=[q_spec, kc_spec, kc_spec, kl_spec, kl_spec],
        out_specs=q_spec,
        out_shape=jax.ShapeDtypeStruct((t_rows if with_ctx_queries else batch * seq, width), BF16),
        compiler_params=_params("arbitrary", "arbitrary"),
        name="gqa_attention",
    )(aq, ak, av, ak, av)


def _merge_kernel(xl_ref, xc_ref, att_ref, of_ref, ob_ref, hg_ref, ga_ref, gh_ref, g1_ref, hgn_ref, post_ref,
                  wa_ref, wh_ref, wo_ref, o_ref, hg_scr, *, n_lat_tiles):
    ya = _dot(att_ref[...], wa_ref[...])
    hgn = hgn_ref[...]
    for i in range(HGRN_HEADS):
        o = _rms(of_ref[i] + ob_ref[i], hgn)
        hg_scr[:, i * HEAD_DIM:(i + 1) * HEAD_DIM] = (o * hg_ref[i].astype(F32)).astype(BF16)
    y = ga_ref[...].astype(F32) * ya + gh_ref[...].astype(F32) * _dot(hg_scr[...], wh_ref[...])
    z = _rms(_dot(y.astype(BF16), wo_ref[...]), post_ref[...])
    x = jnp.where(pl.program_id(0) < n_lat_tiles, xl_ref[...], xc_ref[...])
    o_ref[...] = x + g1_ref[...] * z


def _merge(x_lat, x_ctx, ctx_row0, att, o_f, o_b, hg, ga, gh, mod, hg_norm, post_norm, w_att, w_hg, w_out, *,
           rows, n_lat, seq):
    d = x_lat.shape[1]
    tm = TOKEN_TILE
    n_lat_tiles = n_lat // tm
    per_seq = seq // tm

    def mod_row(i):
        return jnp.where(i < n_lat_tiles, i // per_seq, mod.shape[0] - 1 - 3)

    tok = pl.BlockSpec((tm, d), lambda i: (i, 0))
    hm = pl.BlockSpec((HGRN_HEADS, tm, HEAD_DIM), lambda i: (0, i, 0))
    vec = lambda w: pl.BlockSpec((1, w), lambda i: (0, 0))
    wgt = _resident((d, d), lambda i: (0, 0))
    return pl.pallas_call(
        functools.partial(_merge_kernel, n_lat_tiles=n_lat_tiles),
        grid=(rows // tm,),
        in_specs=[pl.BlockSpec((tm, d), lambda i: (jnp.minimum(i, n_lat_tiles - 1), 0)),
                  pl.BlockSpec((tm, d), lambda i: (ctx_row0 // tm + jnp.maximum(i - n_lat_tiles, 0), 0)),
                  tok, hm, hm, hm, tok, tok,
                  pl.BlockSpec((None, 1, d), lambda i: (mod_row(i), 0, 2)),
                  vec(HEAD_DIM), vec(d), wgt, wgt, wgt],
        out_specs=tok,
        out_shape=jax.ShapeDtypeStruct((rows, d), F32),
        scratch_shapes=[pltpu.VMEM((tm, d), BF16)],
        compiler_params=_params("arbitrary"),
        name="branch_merge",
    )(x_lat, x_ctx, att, o_f, o_b, hg, ga, gh, mod, hg_norm, post_norm, w_att, w_hg, w_out)


def _ffn_kernel(x_ref, sh_ref, sc_ref, g2_ref, pre_ref, post_ref, wg_ref, wu_ref, wd_ref, o_ref, a_scr,
                *, ff_tile):
    x = x_ref[...]
    f = (_rms(x, pre_ref[...]) * (1.0 + sc_ref[...]) + sh_ref[...]).astype(BF16)
    for c in range(0, wg_ref.shape[1], ff_tile):
        g = _dot(f, wg_ref[:, c:c + ff_tile])
        u = _dot(f, wu_ref[:, c:c + ff_tile])
        a_scr[:, c:c + ff_tile] = (_silu(g) * u).astype(BF16)
    y = _rms(_dot(a_scr[...], wd_ref[...]), post_ref[...])
    o_ref[...] = x + g2_ref[...] * y


def _dense_ffn(x, mod, pre_norm, post_norm, w_gate, w_up, w_down, *, n_lat, seq):
    rows, d = x.shape
    d_ff = w_gate.shape[1]
    tm = TOKEN_TILE
    n_lat_tiles = n_lat // tm
    per_seq = seq // tm

    def mod_row(i):
        return jnp.where(i < n_lat_tiles, i // per_seq, mod.shape[0] - 1 - 3)

    tok = pl.BlockSpec((tm, d), lambda i: (i, 0))
    vec = pl.BlockSpec((1, d), lambda i: (0, 0))
    mod_spec = lambda col: pl.BlockSpec((None, 1, d), lambda i: (mod_row(i), 0, col))
    return pl.pallas_call(
        functools.partial(_ffn_kernel, ff_tile=256),
        grid=(rows // tm,),
        in_specs=[tok, mod_spec(3), mod_spec(4), mod_spec(5), vec, vec,
                  _resident((d, d_ff), lambda i: (0, 0)), _resident((d, d_ff), lambda i: (0, 0)),
                  _resident((d_ff, d), lambda i: (0, 0))],
        out_specs=tok,
        out_shape=jax.ShapeDtypeStruct((rows, d), F32),
        scratch_shapes=[pltpu.VMEM((tm, d_ff), BF16)],
        compiler_params=_params("arbitrary"),
        name="dense_swiglu",
    )(x, mod, mod, mod, pre_norm, post_norm, w_gate, w_up, w_down)


def _router_kernel(x_ref, sh_ref, sc_ref, pre_ref, wr_ref, f_ref, route_ref):
    f = _rms(x_ref[...], pre_ref[...]) * (1.0 + sc_ref[...]) + sh_ref[...]
    f_ref[...] = f
    lane = lax.broadcasted_iota(jnp.int32, (f.shape[0], LANES), 1)
    neg = jnp.float32(-jnp.inf)
    logits = jnp.full((f.shape[0], LANES), neg, F32)
    for e in range(wr_ref.shape[0]):
        logits = jnp.where(lane == e, jnp.sum(f * wr_ref[e:e + 1, :], axis=-1, keepdims=True), logits)
    m1 = jnp.max(logits, axis=-1, keepdims=True)
    i1 = jnp.min(jnp.where(logits == m1, lane, LANES), axis=-1, keepdims=True)
    rest = jnp.where(lane == i1, neg, logits)
    m2 = jnp.max(rest, axis=-1, keepdims=True)
    i2 = jnp.min(jnp.where(rest == m2, lane, LANES), axis=-1, keepdims=True)
    w1 = 1.0 / (1.0 + jnp.exp(m2 - m1))
    w2 = 1.0 - w1
    route = jnp.where(lane == 0, i1.astype(F32),
                      jnp.where(lane == 1, i2.astype(F32),
                                jnp.where(lane == 2, w1, jnp.where(lane == 3, w2, 0.0))))
    route_ref[...] = route


def _router(x, mod, pre_norm, w_router_t, *, seq):
    rows, d = x.shape
    tm = TOKEN_TILE
    per_seq = seq // tm
    tok = pl.BlockSpec((tm, d), lambda i: (i, 0))
    mod_spec = lambda col: pl.BlockSpec((None, 1, d), lambda i: (i // per_seq, 0, col))
    return pl.pallas_call(
        _router_kernel,
        grid=(rows // tm,),
        in_specs=[tok, mod_spec(3), mod_spec(4), pl.BlockSpec((1, d), lambda i: (0, 0)),
                  pl.BlockSpec(w_router_t.shape, lambda i: (0, 0))],
        out_specs=[tok, pl.BlockSpec((tm, LANES), lambda i: (i, 0))],
        out_shape=[jax.ShapeDtypeStruct((rows, d), F32), jax.ShapeDtypeStruct((rows, LANES), F32)],
        compiler_params=_params("arbitrary"),
        name="moe_router",
    )(x, mod, mod, pre_norm, w_router_t)


def _expert_kernel(be_ref, tok_ref, dst_ref, f_hbm, wg_ref, wu_ref, wd_ref, ysel_hbm,
                   xbuf, ybuf, a_scr, gsem, ssem, *, ff_tile, n_blocks, spare_row0):
    del be_ref
    s = pl.program_id(0)
    rows = xbuf.shape[1]

    def gather_row(r, slot):
        pltpu.make_async_copy(f_hbm.at[pl.ds(tok_ref[r], 1)], xbuf.at[slot, pl.ds(r, 1)],
                              gsem.at[slot]).start()

    def scatter_row(r, slot, dst_row):
        pltpu.make_async_copy(ybuf.at[slot, pl.ds(r, 1)], ysel_hbm.at[pl.ds(dst_row, 1)],
                              ssem.at[slot]).start()

    def wait_gather(slot):
        pltpu.make_async_copy(f_hbm.at[pl.ds(0, rows)], xbuf.at[slot], gsem.at[slot]).wait()

    def wait_scatter(slot):
        pltpu.make_async_copy(ybuf.at[slot], ysel_hbm.at[pl.ds(0, rows)], ssem.at[slot]).wait()

    @pl.when(s == 0)
    def _():
        ybuf[...] = jnp.zeros(ybuf.shape, F32)

        def body(r, carry):
            gather_row(r, 0)
            scatter_row(r, 0, spare_row0 + r)
            return carry
        lax.fori_loop(0, rows, body, 0)

    @pl.when((s >= 1) & (s <= n_blocks))
    def _():
        slot = (s - 1) % 2
        wait_gather(slot)
        wait_scatter(slot)
        xb = xbuf[slot].astype(BF16)
        chunks = list(range(0, wg_ref.shape[1], ff_tile))
        per = -(-rows // (2 * len(chunks) + 1))

        def move_rows(j):
            for r in range(j * per, min((j + 1) * per, rows)):
                gather_row(r, 1 - slot)
                scatter_row(r, 1 - slot, dst_ref[r])

        for j, c in enumerate(chunks):
            move_rows(2 * j)
            g = _dot(xb, wg_ref[:, c:c + ff_tile])
            move_rows(2 * j + 1)
            u = _dot(xb, wu_ref[:, c:c + ff_tile])
            a_scr[:, c:c + ff_tile] = (_silu(g) * u).astype(BF16)
        move_rows(2 * len(chunks))
        ybuf[slot] = _dot(a_scr[...], wd_ref[...])

    @pl.when(s == n_blocks + 1)
    def _():
        slot = (n_blocks - 1) % 2

        def body(r, carry):
            scatter_row(r, slot, dst_ref[r])
            return carry
        lax.fori_loop(0, rows, body, 0)
        wait_scatter(1 - slot)
        wait_scatter(slot)
        wait_gather(1 - slot)


def _expert_ffn(f, row_tok, row_dst_ext, block_expert, w_gate, w_up, w_down, *, n_out_rows, spare_row0):
    n_rows = row_tok.shape[0]
    d = f.shape[1]
    d_ff = w_gate.shape[2]
    n_blocks = n_rows // MOE_BLOCK
    idx = lambda fn: pl.BlockSpec((MOE_BLOCK,), fn, memory_space=pltpu.SMEM)
    wgt = lambda shape: pl.BlockSpec((None,) + shape,
                                     lambda s, be: (be[jnp.clip(s - 1, 0, n_blocks - 1)], 0, 0),
                                     pipeline_mode=pl.Buffered(1))
    grid_spec = pltpu.PrefetchScalarGridSpec(
        num_scalar_prefetch=1,
        grid=(n_blocks + 2,),
        in_specs=[idx(lambda s, be: (jnp.minimum(s, n_blocks - 1),)),
                  idx(lambda s, be: (jnp.maximum(s - 1, 0),)),
                  pl.BlockSpec(memory_space=pl.ANY),
                  wgt((d, d_ff)), wgt((d, d_ff)), wgt((d_ff, d))],
        out_specs=pl.BlockSpec(memory_space=pl.ANY),
        scratch_shapes=[pltpu.VMEM((2, MOE_BLOCK, d), F32), pltpu.VMEM((2, MOE_BLOCK, d), F32),
                        pltpu.VMEM((MOE_BLOCK, d_ff), BF16),
                        pltpu.SemaphoreType.DMA((2,)), pltpu.SemaphoreType.DMA((2,))],
    )
    return pl.pallas_call(
        functools.partial(_expert_kernel, ff_tile=256, n_blocks=n_blocks, spare_row0=spare_row0),
        grid_spec=grid_spec,
        out_shape=jax.ShapeDtypeStruct((n_out_rows, d), F32),
        compiler_params=_params("arbitrary"),
        name="moe_expert_swiglu",
    )(block_expert, row_tok, row_dst_ext, f, w_gate, w_up, w_down)


def _combine_kernel(y0_ref, y1_ref, x_ref, route_ref, g2_ref, post_ref, o_ref):
    route = route_ref[...]
    y = route[:, 2:3] * y0_ref[...] + route[:, 3:4] * y1_ref[...]
    o_ref[...] = x_ref[...] + g2_ref[...] * _rms(y, post_ref[...])


def _combine(y_sel, x, route, mod, post_norm, *, seq, choice_stride):
    rows, d = x.shape
    tm = TOKEN_TILE
    per_seq = seq // tm
    tok = pl.BlockSpec((tm, d), lambda i: (i, 0))
    return pl.pallas_call(
        _combine_kernel,
        grid=(rows // tm,),
        in_specs=[tok, pl.BlockSpec((tm, d), lambda i: (choice_stride // tm + i, 0)), tok,
                  pl.BlockSpec((tm, LANES), lambda i: (i, 0)),
                  pl.BlockSpec((None, 1, d), lambda i: (i // per_seq, 0, 5)),
                  pl.BlockSpec((1, d), lambda i: (0, 0))],
        out_specs=tok,
        out_shape=jax.ShapeDtypeStruct((rows, d), F32),
        compiler_params=_params("arbitrary"),
        name="moe_combine",
    )(y_sel, y_sel, x, route, mod, post_norm)


def _moe_ffn(x, mod, pre_norm, post_norm, w_router, w_gate, w_up, w_down, *, seq):
    n_tok, d = x.shape
    n_assign = n_tok * TOP_K
    n_blocks = -(-n_assign // MOE_BLOCK) + N_EXPERTS
    n_rows = n_blocks * MOE_BLOCK
    n_pad = n_rows - n_assign
    assert TOP_K == 2 and n_pad % (TOP_K * TOKEN_TILE) == 0
    f, route = _router(x, mod, pre_norm, w_router.T, seq=seq)

    flat_e = route[:, :TOP_K].astype(jnp.int32).reshape(-1)
    onehot = (flat_e[:, None] == jnp.arange(N_EXPERTS, dtype=jnp.int32)[None, :]).astype(jnp.int32)
    csum = jnp.cumsum(onehot, axis=0)
    counts = csum[-1]
    rank = jnp.sum((csum - onehot) * onehot, axis=1)
    padded = (counts + MOE_BLOCK - 1) // MOE_BLOCK * MOE_BLOCK
    pend = jnp.cumsum(padded)
    dest = ((pend - padded)[flat_e] + rank).astype(jnp.int32)
    block_expert = jnp.minimum(
        jnp.searchsorted(pend, jnp.arange(n_blocks, dtype=jnp.int32) * MOE_BLOCK, side='right'),
        N_EXPERTS - 1).astype(jnp.int32)
    choice_stride = n_tok + n_pad // TOP_K
    row_assign = jnp.full((n_rows,), -1, jnp.int32).at[dest].set(jnp.arange(n_assign, dtype=jnp.int32),
                                                                 unique_indices=True)
    is_pad = row_assign < 0
    pad_id = jnp.cumsum(is_pad.astype(jnp.int32)) - 1
    row_tok = jnp.where(is_pad, 0, row_assign // TOP_K)
    row_dst = jnp.where(is_pad, (pad_id % TOP_K) * choice_stride + n_tok + pad_id // TOP_K,
                        (row_assign % TOP_K) * choice_stride + row_assign // TOP_K)
    row_dst_ext = jnp.concatenate([n_rows + jnp.arange(MOE_BLOCK, dtype=jnp.int32), row_dst])

    y_sel = _expert_ffn(f, row_tok, row_dst_ext, block_expert, w_gate, w_up, w_down,
                        n_out_rows=n_rows + 2 * MOE_BLOCK, spare_row0=n_rows + MOE_BLOCK)
    return _combine(y_sel, x, route, mod, post_norm, seq=seq, choice_stride=choice_stride)


def _rope_tables(seq, pad_rows):
    pos = jnp.arange(seq)
    nfreq = HEAD_DIM // 4
    inv_freq = ROPE_THETA ** (-jnp.arange(nfreq, dtype=F32) / nfreq)
    ang = jnp.concatenate([(pos // GRID_W).astype(F32)[:, None] * inv_freq,
                           (pos % GRID_W).astype(F32)[:, None] * inv_freq], axis=-1)
    ang = jnp.concatenate([ang, ang], axis=-1)
    sign = jnp.where(jnp.arange(HEAD_DIM) < HEAD_DIM // 2, -1.0, 1.0).astype(F32)
    cos = jnp.concatenate([jnp.cos(ang), jnp.ones((pad_rows, HEAD_DIM), F32)], axis=0)
    sin = jnp.concatenate([jnp.sin(ang) * sign, jnp.zeros((pad_rows, HEAD_DIM), F32)], axis=0)
    return cos, sin


def kernel(x, c, ctx, c_ctx, w_mod, b_mod, pre_mix_norm, post_mix_norm, pre_ffn_norm, post_ffn_norm, w_in, q_norm, k_norm, hg_norm, hg_lb_logits, w_att_branch, w_hg_branch, w_out, ffn_w_gate, ffn_w_up, ffn_w_down, moe_router, moe_w_gate, moe_w_up, moe_w_down):
    batch, seq, d = x.shape
    ctx_len = ctx.shape[1]
    depth = w_mod.shape[0]
    n_lat = batch * seq
    assert d == D_MODEL and w_in.shape[2] == PROJ_WIDTH
    assert seq % HGRN_STEP == 0 and ctx_len % HGRN_STEP == 0 and n_lat % TOKEN_TILE == 0
    assert seq % ctx_len == 0 and ctx_len % TOKEN_TILE == 0 and seq % ATT_K_TILE == 0

    c_rows = jnp.concatenate([c, c_ctx[None, :], jnp.zeros((3, d), F32)], axis=0)
    mod_all = _mod_vectors(c_rows, w_mod, b_mod)
    cos_tab, sin_tab = _rope_tables(seq, INPROJ_TILE)
    row = lambda v: v.reshape(1, -1)

    t_rows = n_lat + batch * ctx_len
    x_lat, x_ctx, ctx_row0 = x.reshape(n_lat, d), ctx.reshape(batch * ctx_len, d), 0
    for layer in range(depth):
        last = layer == depth - 1
        mod = mod_all[layer].reshape(batch + 4, 1, N_MOD * d)
        (aq, ak, av, hq, kf, lf, kb, lb, hi, hg, ga, gh) = _in_projection(
            x_lat, x_ctx, ctx_row0, mod, row(pre_mix_norm[layer]), w_in[layer].astype(BF16), cos_tab, sin_tab,
            row(q_norm[layer]), row(k_norm[layer]), hg_lb_logits, layer=layer, n_lat=n_lat, t_rows=t_rows,
            seq=seq)
        o_f, o_b = _hgrn_scan(hq, hi, kf, lf, kb, lb, batch=batch, seq=seq, ctx_len=ctx_len)
        att = _attention(aq, ak, av, batch=batch, seq=seq, ctx_len=ctx_len, with_ctx_queries=not last)
        rows = n_lat if last else t_rows
        xt = _merge(x_lat, x_ctx, ctx_row0, att, o_f, o_b, hg, ga, gh, mod, row(hg_norm[layer]), row(post_mix_norm[layer]),
                    w_att_branch[layer].astype(BF16), w_hg_branch[layer].astype(BF16),
                    w_out[layer].astype(BF16), rows=rows, n_lat=n_lat, seq=seq)
        idx = layer // 2
        if layer % 2 == 0:
            xt = _dense_ffn(xt, mod, row(pre_ffn_norm[layer]), row(post_ffn_norm[layer]),
                            ffn_w_gate[idx].astype(BF16), ffn_w_up[idx].astype(BF16),
                            ffn_w_down[idx].astype(BF16), n_lat=n_lat, seq=seq)
            x_lat, x_ctx, ctx_row0 = xt, xt, n_lat
        else:
            assert last, "the expert mixer is implemented for latent tokens only"
            xt = _moe_ffn(xt, mod, row(pre_ffn_norm[layer]), row(post_ffn_norm[layer]), moe_router[idx],
                          moe_w_gate[idx].astype(BF16), moe_w_up[idx].astype(BF16),
                          moe_w_down[idx].astype(BF16), seq=seq)
    return xt[:n_lat].reshape(batch, seq, d)
```

```python
import functools

import jax
import jax.numpy as jnp
from jax import lax
from jax.experimental import pallas as pl
from jax.experimental.pallas import tpu as pltpu

F32 = jnp.float32
BF16 = jnp.bfloat16

D_MODEL = 1024
NORM_EPS = 1e-6
LOG2_E = 1.4426950408889634
N_MOD = 6
GRID_W = 64
ROPE_THETA = 10000.0

HEAD_DIM = 128
ATT_HEADS = 8
ATT_KV_HEADS = 2
ATT_GROUP = ATT_HEADS // ATT_KV_HEADS
KV_WIDTH = ATT_KV_HEADS * HEAD_DIM
HGRN_HEADS = 8

N_EXPERTS = 8
TOP_K = 2
MOE_BLOCK = 256

SUBLANES = 8
LANES = 128
BF16_ROWS = 16
VMEM_LIMIT_BYTES = 56 * 1024 * 1024

TOKEN_TILE = 256
INPROJ_TILE = 256
HGRN_CHUNK = 128
HGRN_STEP = 256
HGRN_STAGGER = 2
LOG_F_TERMS = 2
ATT_Q_TILE = 128
ATT_Q_STEP = 256
ATT_K_TILE = 256

_C_AQ = 0
_C_AK = _C_AQ + ATT_HEADS * HEAD_DIM
_C_AV = _C_AK + KV_WIDTH
_C_HQ = _C_AV + KV_WIDTH
_C_HFF = _C_HQ + D_MODEL
_C_HFB = _C_HFF + D_MODEL
_C_HI = _C_HFB + D_MODEL
_C_HG = _C_HI + D_MODEL
_C_GA = _C_HG + D_MODEL
_C_GH = _C_GA + D_MODEL
PROJ_WIDTH = _C_GH + D_MODEL


def _params(*sem):
    return pltpu.CompilerParams(dimension_semantics=sem, vmem_limit_bytes=VMEM_LIMIT_BYTES)


def _resident(shape, index_map):
    return pl.BlockSpec(shape, index_map, pipeline_mode=pl.Buffered(1))


def _rms(t, gain):
    return t * lax.rsqrt(jnp.mean(t * t, axis=-1, keepdims=True) + NORM_EPS) * gain


def _sigmoid(t):
    return 0.5 * jnp.tanh(0.5 * t) + 0.5


def _silu(t):
    return t * _sigmoid(t)


def _dot(a, b):
    return jnp.dot(a, b, preferred_element_type=F32)


def _dot_nt(a, b):
    return lax.dot_general(a, b, (((1,), (1,)), ((), ())), preferred_element_type=F32)


def _dot_tn(a, b):
    return lax.dot_general(a, b, (((0,), (0,)), ((), ())), preferred_element_type=F32)


def _mod_kernel(c_ref, w_ref, b_ref, o_ref):
    a = _silu(c_ref[...])
    o_ref[...] = jnp.dot(a, w_ref[...], preferred_element_type=F32,
                         precision=lax.Precision.HIGHEST) + b_ref[...]


def _mod_vectors(c_rows, w_mod, b_mod):
    depth, d, width = w_mod.shape
    rows = c_rows.shape[0]
    tn = 1536
    return pl.pallas_call(
        _mod_kernel,
        grid=(depth, width // tn),
        in_specs=[
            pl.BlockSpec((rows, d), lambda l, j: (0, 0)),
            pl.BlockSpec((None, d, tn), lambda l, j: (l, 0, j)),
            pl.BlockSpec((None, 1, tn), lambda l, j: (l, 0, j)),
        ],
        out_specs=pl.BlockSpec((None, rows, tn), lambda l, j: (l, 0, j)),
        out_shape=jax.ShapeDtypeStruct((depth, rows, width), F32),
        compiler_params=_params("arbitrary", "arbitrary"),
        name="mod_vectors",
    )(c_rows, w_mod, b_mod.reshape(depth, 1, width))


def _inproj_kernel(xl_ref, xc_ref, sh_ref, sc_ref, gain_ref, w_ref, cos_ref, sin_ref, qn_ref, kn_ref, lbl_ref,
                   aq_ref, ak_ref, av_ref, hq_ref, kf_ref, lf_ref, kb_ref, lb_ref, hi_ref, hg_ref,
                   ga_ref, gh_ref, *, layer, n_lat_tiles):
    x = jnp.where(pl.program_id(0) < n_lat_tiles, xl_ref[...], xc_ref[...])
    h = _rms(x, gain_ref[...])
    hb = (h * (1.0 + sc_ref[...]) + sh_ref[...]).astype(BF16)
    cos = cos_ref[...]
    sin = sin_ref[...]

    def proj(c0, width):
        return _dot(hb, w_ref[:, c0:c0 + width])

    def head(t, i):
        return t[:, i * HEAD_DIM:(i + 1) * HEAD_DIM]

    def norm_rope(t, gain):
        r = _rms(t, gain)
        return r * cos + pltpu.roll(r, HEAD_DIM // 2, 1) * sin

    lg = lbl_ref[...]
    depth = lg.shape[0]
    mx = lg[0:1]
    for j in range(1, depth):
        mx = jnp.maximum(mx, lg[j:j + 1])
    es = [jnp.exp(lg[j:j + 1] - mx) for j in range(depth)]
    tot = es[0]
    for j in range(1, depth):
        tot = tot + es[j]
    low = jnp.zeros_like(tot)
    for j in range(1, layer + 1):
        low = low + es[j] / tot
    qn = qn_ref[...]
    kn = kn_ref[...]
    half = 4 * HEAD_DIM

    def queries(c):
        def epilogue(t):
            for i in range(4):
                hh = c * 4 + i
                aq_ref[:, hh * HEAD_DIM:(hh + 1) * HEAD_DIM] = (
                    norm_rope(head(t, i), qn) * (HEAD_DIM ** -0.5 * LOG2_E)).astype(BF16)
        return epilogue

    def keys_values(t):
        for i in range(ATT_KV_HEADS):
            ak_ref[:, i * HEAD_DIM:(i + 1) * HEAD_DIM] = norm_rope(head(t, i), kn).astype(BF16)
        av_ref[...] = t[:, KV_WIDTH:].astype(BF16)

    def per_head(ref, c, fn):
        def epilogue(t):
            for i in range(4):
                ref[c * 4 + i] = fn(head(t, i)).astype(BF16)
        return epilogue

    def forget_gate(k_ref, l_ref, c):
        def epilogue(t):
            for i in range(4):
                hh = c * 4 + i
                lo = low[:, hh * HEAD_DIM:(hh + 1) * HEAD_DIM]
                f = lo + (1.0 - lo) * jax.nn.sigmoid(head(t, i))
                k_ref[hh] = (1.0 - f).astype(BF16)
                rest = jnp.log(f) * LOG2_E
                terms = []
                for _ in range(LOG_F_TERMS):
                    terms.append(rest.astype(BF16))
                    rest = rest - terms[-1].astype(F32)
                l_ref[hh] = jnp.concatenate(terms, axis=1)
        return epilogue

    def branch_gate(ref, c):
        def epilogue(t):
            ref[:, c * half:(c + 1) * half] = _sigmoid(t).astype(BF16)
        return epilogue

    groups = [(_C_AQ + c * half, queries(c)) for c in range(2)]
    groups += [(_C_AK, keys_values)]
    groups += [(_C_HQ + c * half, per_head(hq_ref, c, _silu)) for c in range(2)]
    groups += [(_C_HFF + c * half, forget_gate(kf_ref, lf_ref, c)) for c in range(2)]
    groups += [(_C_HFB + c * half, forget_gate(kb_ref, lb_ref, c)) for c in range(2)]
    groups += [(_C_HG + c * half, per_head(hg_ref, c, _silu)) for c in range(2)]
    groups += [(_C_GA + c * half, branch_gate(ga_ref, c)) for c in range(2)]
    groups += [(_C_GH + c * half, branch_gate(gh_ref, c)) for c in range(2)]
    groups += [(_C_HI + c * half, per_head(hi_ref, c, lambda t: t)) for c in range(2)]

    pending = None
    for c0, epilogue in groups:
        t = proj(c0, half)
        if pending is not None:
            pending[1](pending[0])
        pending = (t, epilogue)
    pending[1](pending[0])


def _in_projection(x_lat, x_ctx, ctx_row0, mod, gain, w_in, cos_tab, sin_tab, q_norm, k_norm, lb_logits, *,
                   layer, n_lat, t_rows, seq):
    d = x_lat.shape[1]
    tm = INPROJ_TILE
    assert n_lat % tm == 0 and t_rows % tm == 0 and seq % tm == 0 and ctx_row0 % tm == 0
    n_lat_tiles = n_lat // tm
    per_seq = seq // tm

    def mod_row(i):
        return jnp.where(i < n_lat_tiles, i // per_seq, mod.shape[0] - 1 - 3)

    def pos_blk(i):
        return jnp.where(i < n_lat_tiles, i % per_seq, per_seq)

    tok = lambda w: pl.BlockSpec((tm, w), lambda i: (i, 0))
    hm = pl.BlockSpec((HGRN_HEADS, tm, HEAD_DIM), lambda i: (0, i, 0))
    hm3 = pl.BlockSpec((HGRN_HEADS, tm, LOG_F_TERMS * HEAD_DIM), lambda i: (0, i, 0))
    hm_shape = lambda terms: jax.ShapeDtypeStruct((HGRN_HEADS, t_rows, terms * HEAD_DIM), BF16)
    tm_shape = lambda w: jax.ShapeDtypeStruct((t_rows, w), BF16)
    return pl.pallas_call(
        functools.partial(_inproj_kernel, layer=layer, n_lat_tiles=n_lat_tiles),
        grid=(t_rows // tm,),
        in_specs=[
            pl.BlockSpec((tm, d), lambda i: (jnp.minimum(i, n_lat_tiles - 1), 0)),
            pl.BlockSpec((tm, d), lambda i: (ctx_row0 // tm + jnp.maximum(i - n_lat_tiles, 0), 0)),
            pl.BlockSpec((None, 1, d), lambda i: (mod_row(i), 0, 0)),
            pl.BlockSpec((None, 1, d), lambda i: (mod_row(i), 0, 1)),
            pl.BlockSpec((1, d), lambda i: (0, 0)),
            _resident((d, PROJ_WIDTH), lambda i: (0, 0)),
            pl.BlockSpec((tm, HEAD_DIM), lambda i: (pos_blk(i), 0)),
            pl.BlockSpec((tm, HEAD_DIM), lambda i: (pos_blk(i), 0)),
            pl.BlockSpec((1, HEAD_DIM), lambda i: (0, 0)),
            pl.BlockSpec((1, HEAD_DIM), lambda i: (0, 0)),
            pl.BlockSpec(lb_logits.shape, lambda i: (0, 0)),
        ],
        out_specs=[tok(d), tok(KV_WIDTH), tok(KV_WIDTH), hm, hm, hm3, hm, hm3, hm, hm, tok(d), tok(d)],
        out_shape=[tm_shape(d), tm_shape(KV_WIDTH), tm_shape(KV_WIDTH), hm_shape(1), hm_shape(1),
                   hm_shape(LOG_F_TERMS), hm_shape(1), hm_shape(LOG_F_TERMS), hm_shape(1), hm_shape(1),
                   tm_shape(d), tm_shape(d)],
        compiler_params=_params("arbitrary"),
        name=f"in_projection_l{layer}",
    )(x_lat, x_ctx, mod, mod, gain, w_in, cos_tab, sin_tab, q_norm, k_norm, lb_logits)


def _hgrn_constants(n, c):
    t = lax.broadcasted_iota(jnp.int32, (n, n), 0)
    s = lax.broadcasted_iota(jnp.int32, (n, n), 1)
    tri = ((t >= s).astype(BF16), (t <= s).astype(BF16))
    t = lax.broadcasted_iota(jnp.int32, (c, c), 0)
    s = lax.broadcasted_iota(jnp.int32, (c, c), 1)
    levels = []
    blk = 2
    while blk <= c:
        half = blk // 2
        same = (t // blk) == (s // blk)
        t_hi = (t % blk) >= half
        s_hi = (s % blk) >= half
        levels.append((blk, (same & t_hi & ~s_hi, same & ~t_hi & s_hi)))
        blk *= 2
    return tri, t == s, levels


def _boundary_decay(cum, blk, rev):
    n = cum.shape[0]
    half = blk // 2
    off = half if rev else half - 1
    if blk >= 2 * SUBLANES:
        pieces = []
        for a0 in range(0, n, blk):
            mid = cum[a0 + off:a0 + off + 1, :]
            first, second = cum[a0:a0 + half], cum[a0 + half:a0 + blk]
            pieces += [first - mid, mid - second] if rev else [mid - first, second - mid]
        return jnp.exp2(jnp.concatenate(pieces, axis=0))
    c3 = cum.reshape(n // SUBLANES, SUBLANES, LANES)
    sub = lax.broadcasted_iota(jnp.int32, c3.shape, 1)
    mid = jnp.broadcast_to(c3[:, off:off + 1, :], c3.shape)
    for a0 in range(blk, SUBLANES, blk):
        mid = jnp.where(sub >= a0, jnp.broadcast_to(c3[:, a0 + off:a0 + off + 1, :], c3.shape), mid)
    x = lax.bitcast_convert_type(cum - mid.reshape(n, LANES), jnp.uint32) | jnp.uint32(0x80000000)
    return jnp.exp2(lax.bitcast_convert_type(x, F32))


def _hgrn_block(load, consts, chunk, rev, done, state_only=False):
    tri, eye, levels = consts
    q, k, v, g3, st_t = load()
    n = q.shape[0]
    parts = _dot(tri[rev], g3)
    yield
    cum = parts[:, :LANES]
    for i in range(1, g3.shape[1] // LANES):
        cum = cum + parts[:, i * LANES:(i + 1) * LANES]
    tail = cum[0:1] if rev else cum[n - 1:n]

    qf = q.astype(F32)
    kf = k.astype(F32)
    scaled = lambda t, e: (t * e).astype(BF16)
    out = _dot_nt(scaled(qf, jnp.exp2(cum)), st_t.astype(BF16))
    yield
    k_end = scaled(kf, jnp.exp2(tail - cum))
    st_new = st_t * jnp.exp2(tail) + _dot_tn(v, k_end)
    yield
    if state_only:
        done(None, st_new)
        return

    blk = n
    far = []
    while blk > chunk:
        half = blk // 2
        e = _boundary_decay(cum, blk, rev)
        for a0 in range(0, n, blk):
            qr = a0 if rev else a0 + half
            kr = a0 + half if rev else a0
            sc = _dot_nt(scaled(qf[qr:qr + half], e[qr:qr + half]), scaled(kf[kr:kr + half], e[kr:kr + half]))
            far.append((qr, half, _dot(sc.astype(BF16), v[kr:kr + half])))
        blk //= 2
    yield

    sides = [(q, k)]
    for blk, _ in levels:
        if blk == 2:
            sides.append((scaled(qf, jnp.exp2(g3[:, :LANES].astype(F32))), k))
        elif blk < 2 * BF16_ROWS:
            e = _boundary_decay(cum, blk, rev)
            sides.append((scaled(qf, e), scaled(kf, e)))
            yield
        else:
            e = _boundary_decay(cum, blk, rev)
            half = blk // 2
            q_l, k_l = [], []
            for a0 in range(0, n, blk):
                first, second = slice(a0, a0 + half), slice(a0 + half, a0 + blk)
                q_rows, k_rows = (first, second) if rev else (second, first)
                scaled_q, scaled_k = scaled(qf[q_rows], e[q_rows]), scaled(kf[k_rows], e[k_rows])
                q_l += [scaled_q, q[second]] if rev else [q[first], scaled_q]
                k_l += [k[first], scaled_k] if rev else [scaled_k, k[second]]
            sides.append((jnp.concatenate(q_l, axis=0), jnp.concatenate(k_l, axis=0)))
            yield
    masks = [eye] + [m[rev] for _, m in levels]
    outs = []
    for c0 in range(0, n, chunk):
        yield
        a = jnp.zeros((chunk, chunk), F32)
        rows = slice(c0, c0 + chunk)
        zero = jnp.zeros((chunk, LANES), BF16)
        for i in range(0, len(sides) - 1, 2):
            (qa, ka), (qb, kb) = sides[i], sides[i + 1]
            keys = jnp.concatenate([jnp.concatenate([ka[rows], zero], axis=1),
                                    jnp.concatenate([zero, kb[rows]], axis=1)], axis=0)
            sc = _dot_nt(jnp.concatenate([qa[rows], qb[rows]], axis=1), keys)
            a = jnp.where(masks[i], sc[:, :chunk], a)
            a = jnp.where(masks[i + 1], sc[:, chunk:], a)
            yield
        if len(sides) % 2:
            a = jnp.where(masks[-1], _dot_nt(sides[-1][0][rows], sides[-1][1][rows]), a)
            yield
        o = out[c0:c0 + chunk] + _dot(a.astype(BF16), v[c0:c0 + chunk])
        for r0, rows, val in far:
            if r0 <= c0 < r0 + rows:
                o = o + val[c0 - r0:c0 - r0 + chunk]
        outs.append(o)
    done(outs[0] if len(outs) == 1 else jnp.concatenate(outs, axis=0), st_new)


def _softmax_tile(q, carry, k, v):
    m, acc = carry
    ones_col = (lax.broadcasted_iota(jnp.int32, v.shape, 1) == 0).astype(BF16)
    s = _dot_nt(q, k)
    m_new = jnp.maximum(m, jnp.max(s, axis=-1, keepdims=True))
    p = jnp.exp2((s - m_new).astype(BF16))
    acc = jnp.exp2(m - m_new) * acc + _dot(p, jnp.concatenate([v, ones_col], axis=1))
    return m_new, acc


def _softmax_init(rows):
    return jnp.full((rows, 1), -1e30, F32), jnp.zeros((rows, 2 * HEAD_DIM), F32)


def _softmax_result(carry):
    acc = carry[1]
    return acc[:, :HEAD_DIM] / acc[:, HEAD_DIM:HEAD_DIM + 1]


def _hgrn_kernel(qf_ref, vf_ref, kf_ref, lf_ref, qb_ref, vb_ref, kb_ref, lb_ref, of_ref, ob_ref,
                 st_ref, *, chunk, state_only_steps):
    step = pl.program_id(1)

    @pl.when(step == 0)
    def _():
        st_ref[...] = jnp.zeros(st_ref.shape, F32)

    consts = _hgrn_constants(qf_ref.shape[1], chunk)

    def block(h, rev, state_only):
        q_ref, k_ref, v_ref, l_ref, o_ref = ((qf_ref, kf_ref, vf_ref, lf_ref, of_ref),
                                             (qb_ref, kb_ref, vb_ref, lb_ref, ob_ref))[rev]

        def done(o, st):
            o_ref[h] = jnp.zeros(o_ref.shape[1:], o_ref.dtype) if o is None else o
            st_ref[rev, h] = st
        load = lambda: (q_ref[h], k_ref[h], v_ref[h], l_ref[h], st_ref[rev, h])
        return _hgrn_block(load, consts, chunk, rev, done, state_only)

    def run(state_only):
        blocks = [block(h, rev, state_only) for h in range(qf_ref.shape[0]) for rev in (0, 1)]
        live = []
        turn = 0
        while blocks or live:
            if blocks and turn % HGRN_STAGGER == 0:
                live.append(blocks.pop(0))
            turn += 1
            for gen in list(live):
                if next(gen, StopIteration) is StopIteration:
                    live.remove(gen)

    if state_only_steps:
        pl.when(step < state_only_steps)(lambda: run(True))
        pl.when(step >= state_only_steps)(lambda: run(False))
    else:
        run(False)


def _hgrn_scan(hq, hi, kf, lf, kb, lb, *, batch, seq, ctx_len, ctx_outputs):
    heads, t_rows, dh = hq.shape
    ts = HGRN_STEP
    n_ctx = ctx_len // ts
    n_seq = seq // ts
    ctx0 = batch * n_seq

    def fwd(b, s):
        return jnp.where(s < n_ctx, ctx0 + b * n_ctx + s, b * n_seq + s - n_ctx)

    def bwd(b, s):
        return jnp.where(s < n_ctx, ctx0 + b * n_ctx + n_ctx - 1 - s, b * n_seq + n_seq - 1 - (s - n_ctx))

    spec_f = pl.BlockSpec((heads, ts, dh), lambda b, s: (0, fwd(b, s), 0))
    spec_b = pl.BlockSpec((heads, ts, dh), lambda b, s: (0, bwd(b, s), 0))
    log_f = pl.BlockSpec((heads, ts, lf.shape[2]), lambda b, s: (0, fwd(b, s), 0))
    log_b = pl.BlockSpec((heads, ts, lb.shape[2]), lambda b, s: (0, bwd(b, s), 0))
    out = jax.ShapeDtypeStruct((heads, t_rows, dh), F32)
    return pl.pallas_call(
        functools.partial(_hgrn_kernel, chunk=HGRN_CHUNK, state_only_steps=0 if ctx_outputs else n_ctx),
        grid=(batch, n_ctx + n_seq),
        in_specs=[spec_f, spec_f, spec_f, log_f, spec_b, spec_b, spec_b, log_b],
        out_specs=[spec_f, spec_b],
        out_shape=[out, out],
        scratch_shapes=[pltpu.VMEM((2, heads, dh, dh), F32)],
        compiler_params=_params("arbitrary", "arbitrary"),
        name="hgrn_scan",
    )(hq, hi, kf, lf, hq, hi, kb, lb)


def _attn_kernel(q_ref, kc_ref, vc_ref, kl_ref, vl_ref, o_ref, *, n_lat_q_steps, k_tile):
    tq = ATT_Q_TILE
    is_latent = pl.program_id(1) < n_lat_q_steps

    def units(with_latent_keys):
        for g in range(ATT_KV_HEADS):
            kv = slice(g * HEAD_DIM, (g + 1) * HEAD_DIM)
            for r0 in range(0, q_ref.shape[0], tq):
                heads = [(g * ATT_GROUP + i) * HEAD_DIM for i in range(ATT_GROUP)]
                q = jnp.concatenate([q_ref[r0:r0 + tq, c:c + HEAD_DIM] for c in heads], axis=0)
                carry = _softmax_tile(q, _softmax_init(ATT_GROUP * tq), kc_ref[:, kv], vc_ref[:, kv])
                if with_latent_keys:
                    for t in range(0, kl_ref.shape[0], k_tile):
                        carry = _softmax_tile(q, carry, kl_ref[t:t + k_tile, kv], vl_ref[t:t + k_tile, kv])
                out = _softmax_result(carry)
                for i, c in enumerate(heads):
                    o_ref[r0:r0 + tq, c:c + HEAD_DIM] = out[i * tq:(i + 1) * tq].astype(BF16)

    @pl.when(is_latent)
    def _():
        units(True)

    @pl.when(jnp.logical_not(is_latent))
    def _():
        units(False)


def _attention(aq, ak, av, *, batch, seq, ctx_len, with_ctx_queries):
    t_rows, width = aq.shape
    tq = ATT_Q_STEP
    nq_lat = seq // tq
    nq_ctx = ctx_len // tq
    ctx_q0 = batch * nq_lat
    ctx_k0 = batch * seq // ctx_len

    def q_blk(b, i):
        return jnp.where(i < nq_lat, b * nq_lat + i, ctx_q0 + b * nq_ctx + i - nq_lat)

    q_spec = pl.BlockSpec((tq, width), lambda b, i: (q_blk(b, i), 0))
    kc_spec = pl.BlockSpec((ctx_len, KV_WIDTH), lambda b, i: (ctx_k0 + b, 0))
    kl_spec = pl.BlockSpec((seq, KV_WIDTH), lambda b, i: (b, 0))
    return pl.pallas_call(
        functools.partial(_attn_kernel, n_lat_q_steps=nq_lat, k_tile=ATT_K_TILE),
        grid=(batch, nq_lat + (nq_ctx if with_ctx_queries else 0)),
        in_specs=[q_spec, kc_spec, kc_spec, kl_spec, kl_spec],
        out_specs=q_spec,
        out_shape=jax.ShapeDtypeStruct((t_rows if with_ctx_queries else batch * seq, width), BF16),
        compiler_params=_params("arbitrary", "arbitrary"),
        name="gqa_attention",
    )(aq, ak, av, ak, av)


def _merge_kernel(xl_ref, xc_ref, att_ref, of_ref, ob_ref, hg_ref, ga_ref, gh_ref, g1_ref, hgn_ref, post_ref,
                  wa_ref, wh_ref, wo_ref, o_ref, hg_scr, *, n_lat_tiles):
    ya = _dot(att_ref[...], wa_ref[...])
    hgn = hgn_ref[...]
    for i in range(HGRN_HEADS):
        o = _rms(of_ref[i] + ob_ref[i], hgn)
        hg_scr[:, i * HEAD_DIM:(i + 1) * HEAD_DIM] = (o * hg_ref[i].astype(F32)).astype(BF16)
    y = ga_ref[...].astype(F32) * ya + gh_ref[...].astype(F32) * _dot(hg_scr[...], wh_ref[...])
    z = _rms(_dot(y.astype(BF16), wo_ref[...]), post_ref[...])
    x = jnp.where(pl.program_id(0) < n_lat_tiles, xl_ref[...], xc_ref[...])
    o_ref[...] = x + g1_ref[...] * z


def _merge(x_lat, x_ctx, ctx_row0, att, o_f, o_b, hg, ga, gh, mod, hg_norm, post_norm, w_att, w_hg, w_out, *,
           rows, n_lat, seq):
    d = x_lat.shape[1]
    tm = TOKEN_TILE
    n_lat_tiles = n_lat // tm
    per_seq = seq // tm

    def mod_row(i):
        return jnp.where(i < n_lat_tiles, i // per_seq, mod.shape[0] - 1 - 3)

    tok = pl.BlockSpec((tm, d), lambda i: (i, 0))
    hm = pl.BlockSpec((HGRN_HEADS, tm, HEAD_DIM), lambda i: (0, i, 0))
    vec = lambda w: pl.BlockSpec((1, w), lambda i: (0, 0))
    wgt = _resident((d, d), lambda i: (0, 0))
    return pl.pallas_call(
        functools.partial(_merge_kernel, n_lat_tiles=n_lat_tiles),
        grid=(rows // tm,),
        in_specs=[pl.BlockSpec((tm, d), lambda i: (jnp.minimum(i, n_lat_tiles - 1), 0)),
                  pl.BlockSpec((tm, d), lambda i: (ctx_row0 // tm + jnp.maximum(i - n_lat_tiles, 0), 0)),
                  tok, hm, hm, hm, tok, tok,
                  pl.BlockSpec((None, 1, d), lambda i: (mod_row(i), 0, 2)),
                  vec(HEAD_DIM), vec(d), wgt, wgt, wgt],
        out_specs=tok,
        out_shape=jax.ShapeDtypeStruct((rows, d), F32),
        scratch_shapes=[pltpu.VMEM((tm, d), BF16)],
        compiler_params=_params("arbitrary"),
        name="branch_merge",
    )(x_lat, x_ctx, att, o_f, o_b, hg, ga, gh, mod, hg_norm, post_norm, w_att, w_hg, w_out)


def _ffn_kernel(x_ref, sh_ref, sc_ref, g2_ref, pre_ref, post_ref, wg_ref, wu_ref, wd_ref, o_ref, a_scr,
                *, ff_tile):
    x = x_ref[...]
    f = (_rms(x, pre_ref[...]) * (1.0 + sc_ref[...]) + sh_ref[...]).astype(BF16)
    for c in range(0, wg_ref.shape[1], ff_tile):
        g = _dot(f, wg_ref[:, c:c + ff_tile])
        u = _dot(f, wu_ref[:, c:c + ff_tile])
        a_scr[:, c:c + ff_tile] = (_silu(g) * u).astype(BF16)
    y = _rms(_dot(a_scr[...], wd_ref[...]), post_ref[...])
    o_ref[...] = x + g2_ref[...] * y


def _dense_ffn(x, mod, pre_norm, post_norm, w_gate, w_up, w_down, *, n_lat, seq):
    rows, d = x.shape
    d_ff = w_gate.shape[1]
    tm = TOKEN_TILE
    n_lat_tiles = n_lat // tm
    per_seq = seq // tm

    def mod_row(i):
        return jnp.where(i < n_lat_tiles, i // per_seq, mod.shape[0] - 1 - 3)

    tok = pl.BlockSpec((tm, d), lambda i: (i, 0))
    vec = pl.BlockSpec((1, d), lambda i: (0, 0))
    mod_spec = lambda col: pl.BlockSpec((None, 1, d), lambda i: (mod_row(i), 0, col))
    return pl.pallas_call(
        functools.partial(_ffn_kernel, ff_tile=256),
        grid=(rows // tm,),
        in_specs=[tok, mod_spec(3), mod_spec(4), mod_spec(5), vec, vec,
                  _resident((d, d_ff), lambda i: (0, 0)), _resident((d, d_ff), lambda i: (0, 0)),
                  _resident((d_ff, d), lambda i: (0, 0))],
        out_specs=tok,
        out_shape=jax.ShapeDtypeStruct((rows, d), F32),
        scratch_shapes=[pltpu.VMEM((tm, d_ff), BF16)],
        compiler_params=_params("arbitrary"),
        name="dense_swiglu",
    )(x, mod, mod, mod, pre_norm, post_norm, w_gate, w_up, w_down)


def _router_kernel(x_ref, sh_ref, sc_ref, pre_ref, wr_ref, f_ref, route_ref):
    f = _rms(x_ref[...], pre_ref[...]) * (1.0 + sc_ref[...]) + sh_ref[...]
    f_ref[...] = f
    lane = lax.broadcasted_iota(jnp.int32, (f.shape[0], LANES), 1)
    neg = jnp.float32(-jnp.inf)
    logits = jnp.full((f.shape[0], LANES), neg, F32)
    for e in range(wr_ref.shape[0]):
        logits = jnp.where(lane == e, jnp.sum(f * wr_ref[e:e + 1, :], axis=-1, keepdims=True), logits)
    m1 = jnp.max(logits, axis=-1, keepdims=True)
    i1 = jnp.min(jnp.where(logits == m1, lane, LANES), axis=-1, keepdims=True)
    rest = jnp.where(lane == i1, neg, logits)
    m2 = jnp.max(rest, axis=-1, keepdims=True)
    i2 = jnp.min(jnp.where(rest == m2, lane, LANES), axis=-1, keepdims=True)
    w1 = 1.0 / (1.0 + jnp.exp(m2 - m1))
    w2 = 1.0 - w1
    route = jnp.where(lane == 0, i1.astype(F32),
                      jnp.where(lane == 1, i2.astype(F32),
                                jnp.where(lane == 2, w1, jnp.where(lane == 3, w2, 0.0))))
    route_ref[...] = route


def _router(x, mod, pre_norm, w_router_t, *, seq):
    rows, d = x.shape
    tm = TOKEN_TILE
    per_seq = seq // tm
    tok = pl.BlockSpec((tm, d), lambda i: (i, 0))
    mod_spec = lambda col: pl.BlockSpec((None, 1, d), lambda i: (i // per_seq, 0, col))
    return pl.pallas_call(
        _router_kernel,
        grid=(rows // tm,),
        in_specs=[tok, mod_spec(3), mod_spec(4), pl.BlockSpec((1, d), lambda i: (0, 0)),
                  pl.BlockSpec(w_router_t.shape, lambda i: (0, 0))],
        out_specs=[tok, pl.BlockSpec((tm, LANES), lambda i: (i, 0))],
        out_shape=[jax.ShapeDtypeStruct((rows, d), F32), jax.ShapeDtypeStruct((rows, LANES), F32)],
        compiler_params=_params("arbitrary"),
        name="moe_router",
    )(x, mod, mod, pre_norm, w_router_t)


def _expert_kernel(be_ref, tok_ref, dst_ref, f_hbm, wg_ref, wu_ref, wd_ref, ysel_hbm,
                   xbuf, ybuf, a_scr, gsem, ssem, *, ff_tile, n_blocks, spare_row0):
    del be_ref
    s = pl.program_id(0)
    rows = xbuf.shape[1]

    def gather_row(r, slot):
        pltpu.make_async_copy(f_hbm.at[pl.ds(tok_ref[r], 1)], xbuf.at[slot, pl.ds(r, 1)],
                              gsem.at[slot]).start()

    def scatter_row(r, slot, dst_row):
        pltpu.make_async_copy(ybuf.at[slot, pl.ds(r, 1)], ysel_hbm.at[pl.ds(dst_row, 1)],
                              ssem.at[slot]).start()

    def wait_gather(slot):
        pltpu.make_async_copy(f_hbm.at[pl.ds(0, rows)], xbuf.at[slot], gsem.at[slot]).wait()

    def wait_scatter(slot):
        pltpu.make_async_copy(ybuf.at[slot], ysel_hbm.at[pl.ds(0, rows)], ssem.at[slot]).wait()

    @pl.when(s == 0)
    def _():
        ybuf[...] = jnp.zeros(ybuf.shape, F32)

        def body(r, carry):
            gather_row(r, 0)
            scatter_row(r, 0, spare_row0 + r)
            return carry
        lax.fori_loop(0, rows, body, 0)

    @pl.when((s >= 1) & (s <= n_blocks))
    def _():
        slot = (s - 1) % 2
        wait_gather(slot)
        wait_scatter(slot)
        xb = xbuf[slot].astype(BF16)
        chunks = list(range(0, wg_ref.shape[1], ff_tile))
        per = -(-rows // (len(chunks) + 1))

        def move_rows(j):
            for r in range(j * per, min((j + 1) * per, rows)):
                gather_row(r, 1 - slot)
                scatter_row(r, 1 - slot, dst_ref[r])

        for j, c in enumerate(chunks):
            move_rows(j)
            g = _dot(xb, wg_ref[:, c:c + ff_tile])
            u = _dot(xb, wu_ref[:, c:c + ff_tile])
            a_scr[:, c:c + ff_tile] = (_silu(g) * u).astype(BF16)
        move_rows(len(chunks))
        ybuf[slot] = _dot(a_scr[...], wd_ref[...])

    @pl.when(s == n_blocks + 1)
    def _():
        slot = (n_blocks - 1) % 2

        def body(r, carry):
            scatter_row(r, slot, dst_ref[r])
            return carry
        lax.fori_loop(0, rows, body, 0)
        wait_scatter(1 - slot)
        wait_scatter(slot)
        wait_gather(1 - slot)


def _expert_ffn(f, row_tok, row_dst_ext, block_expert, w_gate, w_up, w_down, *, n_out_rows, spare_row0):
    n_rows = row_tok.shape[0]
    d = f.shape[1]
    d_ff = w_gate.shape[2]
    n_blocks = n_rows // MOE_BLOCK
    idx = lambda fn: pl.BlockSpec((MOE_BLOCK,), fn, memory_space=pltpu.SMEM)
    wgt = lambda shape: pl.BlockSpec((None,) + shape,
                                     lambda s, be: (be[jnp.clip(s - 1, 0, n_blocks - 1)], 0, 0),
                                     pipeline_mode=pl.Buffered(1))
    grid_spec = pltpu.PrefetchScalarGridSpec(
        num_scalar_prefetch=1,
        grid=(n_blocks + 2,),
        in_specs=[idx(lambda s, be: (jnp.minimum(s, n_blocks - 1),)),
                  idx(lambda s, be: (jnp.maximum(s - 1, 0),)),
                  pl.BlockSpec(memory_space=pl.ANY),
                  wgt((d, d_ff)), wgt((d, d_ff)), wgt((d_ff, d))],
        out_specs=pl.BlockSpec(memory_space=pl.ANY),
        scratch_shapes=[pltpu.VMEM((2, MOE_BLOCK, d), F32), pltpu.VMEM((2, MOE_BLOCK, d), F32),
                        pltpu.VMEM((MOE_BLOCK, d_ff), BF16),
                        pltpu.SemaphoreType.DMA((2,)), pltpu.SemaphoreType.DMA((2,))],
    )
    return pl.pallas_call(
        functools.partial(_expert_kernel, ff_tile=256, n_blocks=n_blocks, spare_row0=spare_row0),
        grid_spec=grid_spec,
        out_shape=jax.ShapeDtypeStruct((n_out_rows, d), F32),
        compiler_params=_params("arbitrary"),
        name="moe_expert_swiglu",
    )(block_expert, row_tok, row_dst_ext, f, w_gate, w_up, w_down)


def _combine_kernel(y0_ref, y1_ref, x_ref, route_ref, g2_ref, post_ref, o_ref):
    route = route_ref[...]
    y = route[:, 2:3] * y0_ref[...] + route[:, 3:4] * y1_ref[...]
    o_ref[...] = x_ref[...] + g2_ref[...] * _rms(y, post_ref[...])


def _combine(y_sel, x, route, mod, post_norm, *, seq, choice_stride):
    rows, d = x.shape
    tm = TOKEN_TILE
    per_seq = seq // tm
    tok = pl.BlockSpec((tm, d), lambda i: (i, 0))
    return pl.pallas_call(
        _combine_kernel,
        grid=(rows // tm,),
        in_specs=[tok, pl.BlockSpec((tm, d), lambda i: (choice_stride // tm + i, 0)), tok,
                  pl.BlockSpec((tm, LANES), lambda i: (i, 0)),
                  pl.BlockSpec((None, 1, d), lambda i: (i // per_seq, 0, 5)),
                  pl.BlockSpec((1, d), lambda i: (0, 0))],
        out_specs=tok,
        out_shape=jax.ShapeDtypeStruct((rows, d), F32),
        compiler_params=_params("arbitrary"),
        name="moe_combine",
    )(y_sel, y_sel, x, route, mod, post_norm)


def _moe_ffn(x, mod, pre_norm, post_norm, w_router, w_gate, w_up, w_down, *, seq):
    n_tok, d = x.shape
    n_assign = n_tok * TOP_K
    n_blocks = -(-n_assign // MOE_BLOCK) + N_EXPERTS
    n_rows = n_blocks * MOE_BLOCK
    n_pad = n_rows - n_assign
    assert TOP_K == 2 and n_pad % (TOP_K * TOKEN_TILE) == 0
    f, route = _router(x, mod, pre_norm, w_router.T, seq=seq)

    flat_e = route[:, :TOP_K].astype(jnp.int32).reshape(-1)
    onehot = (flat_e[:, None] == jnp.arange(N_EXPERTS, dtype=jnp.int32)[None, :]).astype(jnp.int32)
    csum = jnp.cumsum(onehot, axis=0)
    counts = csum[-1]
    rank = jnp.sum((csum - onehot) * onehot, axis=1)
    padded = (counts + MOE_BLOCK - 1) // MOE_BLOCK * MOE_BLOCK
    pend = jnp.cumsum(padded)
    dest = ((pend - padded)[flat_e] + rank).astype(jnp.int32)
    block_expert = jnp.minimum(
        jnp.searchsorted(pend, jnp.arange(n_blocks, dtype=jnp.int32) * MOE_BLOCK, side='right'),
        N_EXPERTS - 1).astype(jnp.int32)
    choice_stride = n_tok + n_pad // TOP_K
    row_assign = jnp.full((n_rows,), -1, jnp.int32).at[dest].set(jnp.arange(n_assign, dtype=jnp.int32),
                                                                 unique_indices=True)
    is_pad = row_assign < 0
    pad_id = jnp.cumsum(is_pad.astype(jnp.int32)) - 1
    row_tok = jnp.where(is_pad, 0, row_assign // TOP_K)
    row_dst = jnp.where(is_pad, (pad_id % TOP_K) * choice_stride + n_tok + pad_id // TOP_K,
                        (row_assign % TOP_K) * choice_stride + row_assign // TOP_K)
    row_dst_ext = jnp.concatenate([n_rows + jnp.arange(MOE_BLOCK, dtype=jnp.int32), row_dst])

    y_sel = _expert_ffn(f, row_tok, row_dst_ext, block_expert, w_gate, w_up, w_down,
                        n_out_rows=n_rows + 2 * MOE_BLOCK, spare_row0=n_rows + MOE_BLOCK)
    return _combine(y_sel, x, route, mod, post_norm, seq=seq, choice_stride=choice_stride)


def _rope_tables(seq, pad_rows):
    pos = jnp.arange(seq)
    nfreq = HEAD_DIM // 4
    inv_freq = ROPE_THETA ** (-jnp.arange(nfreq, dtype=F32) / nfreq)
    ang = jnp.concatenate([(pos // GRID_W).astype(F32)[:, None] * inv_freq,
                           (pos % GRID_W).astype(F32)[:, None] * inv_freq], axis=-1)
    ang = jnp.concatenate([ang, ang], axis=-1)
    sign = jnp.where(jnp.arange(HEAD_DIM) < HEAD_DIM // 2, -1.0, 1.0).astype(F32)
    cos = jnp.concatenate([jnp.cos(ang), jnp.ones((pad_rows, HEAD_DIM), F32)], axis=0)
    sin = jnp.concatenate([jnp.sin(ang) * sign, jnp.zeros((pad_rows, HEAD_DIM), F32)], axis=0)
    return cos, sin


def kernel(x, c, ctx, c_ctx, w_mod, b_mod, pre_mix_norm, post_mix_norm, pre_ffn_norm, post_ffn_norm, w_in, q_norm, k_norm, hg_norm, hg_lb_logits, w_att_branch, w_hg_branch, w_out, ffn_w_gate, ffn_w_up, ffn_w_down, moe_router, moe_w_gate, moe_w_up, moe_w_down):
    batch, seq, d = x.shape
    ctx_len = ctx.shape[1]
    depth = w_mod.shape[0]
    n_lat = batch * seq
    assert d == D_MODEL and w_in.shape[2] == PROJ_WIDTH
    assert seq % HGRN_STEP == 0 and ctx_len % HGRN_STEP == 0 and n_lat % TOKEN_TILE == 0
    assert seq % ctx_len == 0 and ctx_len % TOKEN_TILE == 0 and seq % ATT_K_TILE == 0

    c_rows = jnp.concatenate([c, c_ctx[None, :], jnp.zeros((3, d), F32)], axis=0)
    mod_all = _mod_vectors(c_rows, w_mod, b_mod)
    cos_tab, sin_tab = _rope_tables(seq, INPROJ_TILE)
    row = lambda v: v.reshape(1, -1)

    t_rows = n_lat + batch * ctx_len
    x_lat, x_ctx, ctx_row0 = x.reshape(n_lat, d), ctx.reshape(batch * ctx_len, d), 0
    for layer in range(depth):
        last = layer == depth - 1
        mod = mod_all[layer].reshape(batch + 4, 1, N_MOD * d)
        (aq, ak, av, hq, kf, lf, kb, lb, hi, hg, ga, gh) = _in_projection(
            x_lat, x_ctx, ctx_row0, mod, row(pre_mix_norm[layer]), w_in[layer].astype(BF16), cos_tab, sin_tab,
            row(q_norm[layer]), row(k_norm[layer]), hg_lb_logits, layer=layer, n_lat=n_lat, t_rows=t_rows,
            seq=seq)
        o_f, o_b = _hgrn_scan(hq, hi, kf, lf, kb, lb, batch=batch, seq=seq, ctx_len=ctx_len,
                              ctx_outputs=not last)
        att = _attention(aq, ak, av, batch=batch, seq=seq, ctx_len=ctx_len, with_ctx_queries=not last)
        rows = n_lat if last else t_rows
        xt = _merge(x_lat, x_ctx, ctx_row0, att, o_f, o_b, hg, ga, gh, mod, row(hg_norm[layer]), row(post_mix_norm[layer]),
                    w_att_branch[layer].astype(BF16), w_hg_branch[layer].astype(BF16),
                    w_out[layer].astype(BF16), rows=rows, n_lat=n_lat, seq=seq)
        idx = layer // 2
        if layer % 2 == 0:
            xt = _dense_ffn(xt, mod, row(pre_ffn_norm[layer]), row(post_ffn_norm[layer]),
                            ffn_w_gate[idx].astype(BF16), ffn_w_up[idx].astype(BF16),
                            ffn_w_down[idx].astype(BF16), n_lat=n_lat, seq=seq)
            x_lat, x_ctx, ctx_row0 = xt, xt, n_lat
        else:
            assert last, "the expert mixer is implemented for latent tokens only"
            xt = _moe_ffn(xt, mod, row(pre_ffn_norm[layer]), row(post_ffn_norm[layer]), moe_router[idx],
                          moe_w_gate[idx].astype(BF16), moe_w_up[idx].astype(BF16),
                          moe_w_down[idx].astype(BF16), seq=seq)
    return xt[:n_lat].reshape(batch, seq, d)
```

```python
import functools

import jax
import jax.numpy as jnp
from jax import lax
from jax.experimental import pallas as pl
from jax.experimental.pallas import tpu as pltpu

F32 = jnp.float32
BF16 = jnp.bfloat16

D_MODEL = 1024
NORM_EPS = 1e-6
LOG2_E = 1.4426950408889634
N_MOD = 6
GRID_W = 64
ROPE_THETA = 10000.0

HEAD_DIM = 128
ATT_HEADS = 8
ATT_KV_HEADS = 2
ATT_GROUP = ATT_HEADS // ATT_KV_HEADS
KV_WIDTH = ATT_KV_HEADS * HEAD_DIM
HGRN_HEADS = 8

N_EXPERTS = 8
TOP_K = 2
MOE_BLOCK = 256

SUBLANES = 8
LANES = 128
BF16_ROWS = 16
VMEM_LIMIT_BYTES = 56 * 1024 * 1024

TOKEN_TILE = 256
INPROJ_TILE = 256
HGRN_CHUNK = 128
HGRN_STEP = 256
HGRN_STAGGER = 2
LOG_F_TERMS = 2
ATT_Q_TILE = 128
ATT_Q_STEP = 256
ATT_K_TILE = 256

_C_AQ = 0
_C_AK = _C_AQ + ATT_HEADS * HEAD_DIM
_C_AV = _C_AK + KV_WIDTH
_C_HQ = _C_AV + KV_WIDTH
_C_HFF = _C_HQ + D_MODEL
_C_HFB = _C_HFF + D_MODEL
_C_HI = _C_HFB + D_MODEL
_C_HG = _C_HI + D_MODEL
_C_GA = _C_HG + D_MODEL
_C_GH = _C_GA + D_MODEL
PROJ_WIDTH = _C_GH + D_MODEL


def _params(*sem):
    return pltpu.CompilerParams(dimension_semantics=sem, vmem_limit_bytes=VMEM_LIMIT_BYTES)


def _resident(shape, index_map):
    return pl.BlockSpec(shape, index_map, pipeline_mode=pl.Buffered(1))


def _rms(t, gain):
    return t * lax.rsqrt(jnp.mean(t * t, axis=-1, keepdims=True) + NORM_EPS) * gain


def _sigmoid(t):
    return 0.5 * jnp.tanh(0.5 * t) + 0.5


def _silu(t):
    return t * _sigmoid(t)


def _dot(a, b):
    return jnp.dot(a, b, preferred_element_type=F32)


def _dot_nt(a, b):
    return lax.dot_general(a, b, (((1,), (1,)), ((), ())), preferred_element_type=F32)


def _dot_tn(a, b):
    return lax.dot_general(a, b, (((0,), (0,)), ((), ())), preferred_element_type=F32)


def _mod_kernel(c_ref, w_ref, b_ref, o_ref):
    a = _silu(c_ref[...])
    o_ref[...] = jnp.dot(a, w_ref[...], preferred_element_type=F32,
                         precision=lax.Precision.HIGHEST) + b_ref[...]


def _mod_vectors(c_rows, w_mod, b_mod):
    depth, d, width = w_mod.shape
    rows = c_rows.shape[0]
    tn = 1536
    return pl.pallas_call(
        _mod_kernel,
        grid=(depth, width // tn),
        in_specs=[
            pl.BlockSpec((rows, d), lambda l, j: (0, 0)),
            pl.BlockSpec((None, d, tn), lambda l, j: (l, 0, j)),
            pl.BlockSpec((None, 1, tn), lambda l, j: (l, 0, j)),
        ],
        out_specs=pl.BlockSpec((None, rows, tn), lambda l, j: (l, 0, j)),
        out_shape=jax.ShapeDtypeStruct((depth, rows, width), F32),
        compiler_params=_params("arbitrary", "arbitrary"),
        name="mod_vectors",
    )(c_rows, w_mod, b_mod.reshape(depth, 1, width))


def _inproj_kernel(xl_ref, xc_ref, sh_ref, sc_ref, gain_ref, w_ref, cos_ref, sin_ref, qn_ref, kn_ref, lbl_ref,
                   aq_ref, ak_ref, av_ref, hq_ref, kf_ref, lf_ref, kb_ref, lb_ref, hi_ref, hg_ref,
                   ga_ref, gh_ref, *, layer, n_lat_tiles):
    x = jnp.where(pl.program_id(0) < n_lat_tiles, xl_ref[...], xc_ref[...])
    h = _rms(x, gain_ref[...])
    hb = (h * (1.0 + sc_ref[...]) + sh_ref[...]).astype(BF16)
    cos = cos_ref[...]
    sin = sin_ref[...]

    def proj(c0, width):
        return _dot(hb, w_ref[:, c0:c0 + width])

    def head(t, i):
        return t[:, i * HEAD_DIM:(i + 1) * HEAD_DIM]

    def norm_rope(t, gain):
        r = _rms(t, gain)
        return r * cos + pltpu.roll(r, HEAD_DIM // 2, 1) * sin

    lg = lbl_ref[...]
    depth = lg.shape[0]
    mx = lg[0:1]
    for j in range(1, depth):
        mx = jnp.maximum(mx, lg[j:j + 1])
    es = [jnp.exp(lg[j:j + 1] - mx) for j in range(depth)]
    tot = es[0]
    for j in range(1, depth):
        tot = tot + es[j]
    low = jnp.zeros_like(tot)
    for j in range(1, layer + 1):
        low = low + es[j] / tot
    qn = qn_ref[...]
    kn = kn_ref[...]
    half = 4 * HEAD_DIM

    def queries(c):
        def epilogue(t):
            for i in range(4):
                hh = c * 4 + i
                aq_ref[:, hh * HEAD_DIM:(hh + 1) * HEAD_DIM] = (
                    norm_rope(head(t, i), qn) * (HEAD_DIM ** -0.5 * LOG2_E)).astype(BF16)
        return epilogue

    def keys_values(t):
        for i in range(ATT_KV_HEADS):
            ak_ref[:, i * HEAD_DIM:(i + 1) * HEAD_DIM] = norm_rope(head(t, i), kn).astype(BF16)
        av_ref[...] = t[:, KV_WIDTH:].astype(BF16)

    def per_head(ref, c, fn):
        def epilogue(t):
            for i in range(4):
                ref[c * 4 + i] = fn(head(t, i)).astype(BF16)
        return epilogue

    def forget_gate(k_ref, l_ref, c):
        def epilogue(t):
            for i in range(4):
                hh = c * 4 + i
                lo = low[:, hh * HEAD_DIM:(hh + 1) * HEAD_DIM]
                f = lo + (1.0 - lo) * jax.nn.sigmoid(head(t, i))
                k_ref[hh] = (1.0 - f).astype(BF16)
                rest = jnp.log(f) * LOG2_E
                terms = []
                for _ in range(LOG_F_TERMS):
                    terms.append(rest.astype(BF16))
                    rest = rest - terms[-1].astype(F32)
                l_ref[hh] = jnp.concatenate(terms, axis=1)
        return epilogue

    def branch_gate(ref, c):
        def epilogue(t):
            ref[:, c * half:(c + 1) * half] = _sigmoid(t).astype(BF16)
        return epilogue

    groups = [(_C_AQ + c * half, queries(c)) for c in range(2)]
    groups += [(_C_AK, keys_values)]
    groups += [(_C_HQ + c * half, per_head(hq_ref, c, _silu)) for c in range(2)]
    groups += [(_C_HFF + c * half, forget_gate(kf_ref, lf_ref, c)) for c in range(2)]
    groups += [(_C_HFB + c * half, forget_gate(kb_ref, lb_ref, c)) for c in range(2)]
    groups += [(_C_HG + c * half, per_head(hg_ref, c, _silu)) for c in range(2)]
    groups += [(_C_GA + c * half, branch_gate(ga_ref, c)) for c in range(2)]
    groups += [(_C_GH + c * half, branch_gate(gh_ref, c)) for c in range(2)]
    groups += [(_C_HI + c * half, per_head(hi_ref, c, lambda t: t)) for c in range(2)]

    pending = None
    for c0, epilogue in groups:
        t = proj(c0, half)
        if pending is not None:
            pending[1](pending[0])
        pending = (t, epilogue)
    pending[1](pending[0])


def _in_projection(x_lat, x_ctx, ctx_row0, mod, gain, w_in, cos_tab, sin_tab, q_norm, k_norm, lb_logits, *,
                   layer, n_lat, t_rows, seq):
    d = x_lat.shape[1]
    tm = INPROJ_TILE
    assert n_lat % tm == 0 and t_rows % tm == 0 and seq % tm == 0 and ctx_row0 % tm == 0
    n_lat_tiles = n_lat // tm
    per_seq = seq // tm

    def mod_row(i):
        return jnp.where(i < n_lat_tiles, i // per_seq, mod.shape[0] - 1 - 3)

    def pos_blk(i):
        return jnp.where(i < n_lat_tiles, i % per_seq, per_seq)

    tok = lambda w: pl.BlockSpec((tm, w), lambda i: (i, 0))
    hm = pl.BlockSpec((HGRN_HEADS, tm, HEAD_DIM), lambda i: (0, i, 0))
    hm3 = pl.BlockSpec((HGRN_HEADS, tm, LOG_F_TERMS * HEAD_DIM), lambda i: (0, i, 0))
    hm_shape = lambda terms: jax.ShapeDtypeStruct((HGRN_HEADS, t_rows, terms * HEAD_DIM), BF16)
    tm_shape = lambda w: jax.ShapeDtypeStruct((t_rows, w), BF16)
    return pl.pallas_call(
        functools.partial(_inproj_kernel, layer=layer, n_lat_tiles=n_lat_tiles),
        grid=(t_rows // tm,),
        in_specs=[
            pl.BlockSpec((tm, d), lambda i: (jnp.minimum(i, n_lat_tiles - 1), 0)),
            pl.BlockSpec((tm, d), lambda i: (ctx_row0 // tm + jnp.maximum(i - n_lat_tiles, 0), 0)),
            pl.BlockSpec((None, 1, d), lambda i: (mod_row(i), 0, 0)),
            pl.BlockSpec((None, 1, d), lambda i: (mod_row(i), 0, 1)),
            pl.BlockSpec((1, d), lambda i: (0, 0)),
            _resident((d, PROJ_WIDTH), lambda i: (0, 0)),
            pl.BlockSpec((tm, HEAD_DIM), lambda i: (pos_blk(i), 0)),
            pl.BlockSpec((tm, HEAD_DIM), lambda i: (pos_blk(i), 0)),
            pl.BlockSpec((1, HEAD_DIM), lambda i: (0, 0)),
            pl.BlockSpec((1, HEAD_DIM), lambda i: (0, 0)),
            pl.BlockSpec(lb_logits.shape, lambda i: (0, 0)),
        ],
        out_specs=[tok(d), tok(KV_WIDTH), tok(KV_WIDTH), hm, hm, hm3, hm, hm3, hm, hm, tok(d), tok(d)],
        out_shape=[tm_shape(d), tm_shape(KV_WIDTH), tm_shape(KV_WIDTH), hm_shape(1), hm_shape(1),
                   hm_shape(LOG_F_TERMS), hm_shape(1), hm_shape(LOG_F_TERMS), hm_shape(1), hm_shape(1),
                   tm_shape(d), tm_shape(d)],
        compiler_params=_params("arbitrary"),
        name=f"in_projection_l{layer}",
    )(x_lat, x_ctx, mod, mod, gain, w_in, cos_tab, sin_tab, q_norm, k_norm, lb_logits)


def _hgrn_constants(n, c):
    t = lax.broadcasted_iota(jnp.int32, (n, n), 0)
    s = lax.broadcasted_iota(jnp.int32, (n, n), 1)
    tri = ((t >= s).astype(BF16), (t <= s).astype(BF16))
    t = lax.broadcasted_iota(jnp.int32, (c, c), 0)
    s = lax.broadcasted_iota(jnp.int32, (c, c), 1)
    levels = []
    blk = 2
    while blk <= c:
        half = blk // 2
        same = (t // blk) == (s // blk)
        t_hi = (t % blk) >= half
        s_hi = (s % blk) >= half
        levels.append((blk, (same & t_hi & ~s_hi, same & ~t_hi & s_hi)))
        blk *= 2
    return tri, t == s, levels


def _boundary_decay(cum, blk, rev):
    n = cum.shape[0]
    half = blk // 2
    off = half if rev else half - 1
    if blk >= 2 * SUBLANES:
        pieces = []
        for a0 in range(0, n, blk):
            mid = cum[a0 + off:a0 + off + 1, :]
            first, second = cum[a0:a0 + half], cum[a0 + half:a0 + blk]
            pieces += [first - mid, mid - second] if rev else [mid - first, second - mid]
        return jnp.exp2(jnp.concatenate(pieces, axis=0))
    c3 = cum.reshape(n // SUBLANES, SUBLANES, LANES)
    sub = lax.broadcasted_iota(jnp.int32, c3.shape, 1)
    mid = jnp.broadcast_to(c3[:, off:off + 1, :], c3.shape)
    for a0 in range(blk, SUBLANES, blk):
        mid = jnp.where(sub >= a0, jnp.broadcast_to(c3[:, a0 + off:a0 + off + 1, :], c3.shape), mid)
    x = lax.bitcast_convert_type(cum - mid.reshape(n, LANES), jnp.uint32) | jnp.uint32(0x80000000)
    return jnp.exp2(lax.bitcast_convert_type(x, F32))


def _hgrn_block(load, consts, chunk, rev, done):
    tri, eye, levels = consts
    q, k, v, g3, st_t = load()
    n = q.shape[0]
    parts = _dot(tri[rev], g3)
    yield
    cum = parts[:, :LANES]
    for i in range(1, g3.shape[1] // LANES):
        cum = cum + parts[:, i * LANES:(i + 1) * LANES]
    tail = cum[0:1] if rev else cum[n - 1:n]

    qf = q.astype(F32)
    kf = k.astype(F32)
    scaled = lambda t, e: (t * e).astype(BF16)
    out = _dot_nt(scaled(qf, jnp.exp2(cum)), st_t.astype(BF16))
    yield
    k_end = scaled(kf, jnp.exp2(tail - cum))
    st_new = st_t * jnp.exp2(tail) + _dot_tn(v, k_end)
    yield

    blk = n
    far = []
    while blk > chunk:
        half = blk // 2
        e = _boundary_decay(cum, blk, rev)
        for a0 in range(0, n, blk):
            qr = a0 if rev else a0 + half
            kr = a0 + half if rev else a0
            sc = _dot_nt(scaled(qf[qr:qr + half], e[qr:qr + half]), scaled(kf[kr:kr + half], e[kr:kr + half]))
            far.append((qr, half, _dot(sc.astype(BF16), v[kr:kr + half])))
        blk //= 2
    yield

    sides = [(q, k)]
    for blk, _ in levels:
        if blk == 2:
            sides.append((scaled(qf, jnp.exp2(g3[:, :LANES].astype(F32))), k))
        elif blk < 2 * BF16_ROWS:
            e = _boundary_decay(cum, blk, rev)
            sides.append((scaled(qf, e), scaled(kf, e)))
            yield
        else:
            e = _boundary_decay(cum, blk, rev)
            half = blk // 2
            q_l, k_l = [], []
            for a0 in range(0, n, blk):
                first, second = slice(a0, a0 + half), slice(a0 + half, a0 + blk)
                q_rows, k_rows = (first, second) if rev else (second, first)
                scaled_q, scaled_k = scaled(qf[q_rows], e[q_rows]), scaled(kf[k_rows], e[k_rows])
                q_l += [scaled_q, q[second]] if rev else [q[first], scaled_q]
                k_l += [k[first], scaled_k] if rev else [scaled_k, k[second]]
            sides.append((jnp.concatenate(q_l, axis=0), jnp.concatenate(k_l, axis=0)))
            yield
    masks = [eye] + [m[rev] for _, m in levels]
    outs = []
    for c0 in range(0, n, chunk):
        yield
        a = jnp.zeros((chunk, chunk), F32)
        rows = slice(c0, c0 + chunk)
        zero = jnp.zeros((chunk, LANES), BF16)
        for i in range(0, len(sides) - 1, 2):
            (qa, ka), (qb, kb) = sides[i], sides[i + 1]
            keys = jnp.concatenate([jnp.concatenate([ka[rows], zero], axis=1),
                                    jnp.concatenate([zero, kb[rows]], axis=1)], axis=0)
            sc = _dot_nt(jnp.concatenate([qa[rows], qb[rows]], axis=1), keys)
            a = jnp.where(masks[i], sc[:, :chunk], a)
            a = jnp.where(masks[i + 1], sc[:, chunk:], a)
            yield
        if len(sides) % 2:
            a = jnp.where(masks[-1], _dot_nt(sides[-1][0][rows], sides[-1][1][rows]), a)
            yield
        o = out[c0:c0 + chunk] + _dot(a.astype(BF16), v[c0:c0 + chunk])
        for r0, rows, val in far:
            if r0 <= c0 < r0 + rows:
                o = o + val[c0 - r0:c0 - r0 + chunk]
        outs.append(o)
    done(outs[0] if len(outs) == 1 else jnp.concatenate(outs, axis=0), st_new)


def _softmax_tile(q, carry, k, v):
    m, acc = carry
    ones_col = (lax.broadcasted_iota(jnp.int32, v.shape, 1) == 0).astype(BF16)
    s = _dot_nt(q, k)
    m_new = jnp.maximum(m, jnp.max(s, axis=-1, keepdims=True))
    p = jnp.exp2((s - m_new).astype(BF16))
    acc = jnp.exp2(m - m_new) * acc + _dot(p, jnp.concatenate([v, ones_col], axis=1))
    return m_new, acc


def _softmax_init(rows):
    return jnp.full((rows, 1), -1e30, F32), jnp.zeros((rows, 2 * HEAD_DIM), F32)


def _softmax_result(carry):
    acc = carry[1]
    return acc[:, :HEAD_DIM] / acc[:, HEAD_DIM:HEAD_DIM + 1]


def _hgrn_kernel(qf_ref, vf_ref, kf_ref, lf_ref, qb_ref, vb_ref, kb_ref, lb_ref, of_ref, ob_ref,
                 st_ref, *, chunk):
    @pl.when(pl.program_id(1) == 0)
    def _():
        st_ref[...] = jnp.zeros(st_ref.shape, F32)

    consts = _hgrn_constants(qf_ref.shape[1], chunk)

    def block(h, rev):
        q_ref, k_ref, v_ref, l_ref, o_ref = ((qf_ref, kf_ref, vf_ref, lf_ref, of_ref),
                                             (qb_ref, kb_ref, vb_ref, lb_ref, ob_ref))[rev]

        def done(o, st):
            o_ref[h] = o
            st_ref[rev, h] = st
        load = lambda: (q_ref[h], k_ref[h], v_ref[h], l_ref[h], st_ref[rev, h])
        return _hgrn_block(load, consts, chunk, rev, done)

    blocks = [block(h, rev) for h in range(qf_ref.shape[0]) for rev in (0, 1)]
    live = []
    turn = 0
    while blocks or live:
        if blocks and turn % HGRN_STAGGER == 0:
            live.append(blocks.pop(0))
        turn += 1
        for gen in list(live):
            if next(gen, StopIteration) is StopIteration:
                live.remove(gen)


def _hgrn_scan(hq, hi, kf, lf, kb, lb, *, batch, seq, ctx_len):
    heads, t_rows, dh = hq.shape
    ts = HGRN_STEP
    n_ctx = ctx_len // ts
    n_seq = seq // ts
    ctx0 = batch * n_seq

    def fwd(b, s):
        return jnp.where(s < n_ctx, ctx0 + b * n_ctx + s, b * n_seq + s - n_ctx)

    def bwd(b, s):
        return jnp.where(s < n_ctx, ctx0 + b * n_ctx + n_ctx - 1 - s, b * n_seq + n_seq - 1 - (s - n_ctx))

    spec_f = pl.BlockSpec((heads, ts, dh), lambda b, s: (0, fwd(b, s), 0))
    spec_b = pl.BlockSpec((heads, ts, dh), lambda b, s: (0, bwd(b, s), 0))
    log_f = pl.BlockSpec((heads, ts, lf.shape[2]), lambda b, s: (0, fwd(b, s), 0))
    log_b = pl.BlockSpec((heads, ts, lb.shape[2]), lambda b, s: (0, bwd(b, s), 0))
    out = jax.ShapeDtypeStruct((heads, t_rows, dh), F32)
    return pl.pallas_call(
        functools.partial(_hgrn_kernel, chunk=HGRN_CHUNK),
        grid=(batch, n_ctx + n_seq),
        in_specs=[spec_f, spec_f, spec_f, log_f, spec_b, spec_b, spec_b, log_b],
        out_specs=[spec_f, spec_b],
        out_shape=[out, out],
        scratch_shapes=[pltpu.VMEM((2, heads, dh, dh), F32)],
        compiler_params=_params("arbitrary", "arbitrary"),
        name="hgrn_scan",
    )(hq, hi, kf, lf, hq, hi, kb, lb)


def _attn_kernel(q_ref, kc_ref, vc_ref, kl_ref, vl_ref, o_ref, *, n_lat_q_steps, k_tile):
    tq = ATT_Q_TILE
    is_latent = pl.program_id(1) < n_lat_q_steps

    def units(with_latent_keys):
        for g in range(ATT_KV_HEADS):
            kv = slice(g * HEAD_DIM, (g + 1) * HEAD_DIM)
            for r0 in range(0, q_ref.shape[0], tq):
                heads = [(g * ATT_GROUP + i) * HEAD_DIM for i in range(ATT_GROUP)]
                q = jnp.concatenate([q_ref[r0:r0 + tq, c:c + HEAD_DIM] for c in heads], axis=0)
                carry = _softmax_tile(q, _softmax_init(ATT_GROUP * tq), kc_ref[:, kv], vc_ref[:, kv])
                if with_latent_keys:
                    for t in range(0, kl_ref.shape[0], k_tile):
                        carry = _softmax_tile(q, carry, kl_ref[t:t + k_tile, kv], vl_ref[t:t + k_tile, kv])
                out = _softmax_result(carry)
                for i, c in enumerate(heads):
                    o_ref[r0:r0 + tq, c:c + HEAD_DIM] = out[i * tq:(i + 1) * tq].astype(BF16)

    @pl.when(is_latent)
    def _():
        units(True)

    @pl.when(jnp.logical_not(is_latent))
    def _():
        units(False)


def _attention(aq, ak, av, *, batch, seq, ctx_len, with_ctx_queries):
    t_rows, width = aq.shape
    tq = ATT_Q_STEP
    nq_lat = seq // tq
    nq_ctx = ctx_len // tq
    ctx_q0 = batch * nq_lat
    ctx_k0 = batch * seq // ctx_len

    def q_blk(b, i):
        return jnp.where(i < nq_lat, b * nq_lat + i, ctx_q0 + b * nq_ctx + i - nq_lat)

    q_spec = pl.BlockSpec((tq, width), lambda b, i: (q_blk(b, i), 0))
    kc_spec = pl.BlockSpec((ctx_len, KV_WIDTH), lambda b, i: (ctx_k0 + b, 0))
    kl_spec = pl.BlockSpec((seq, KV_WIDTH), lambda b, i: (b, 0))
    return pl.pallas_call(
        functools.partial(_attn_kernel, n_lat_q_steps=nq_lat, k_tile=ATT_K_TILE),
        grid=(batch, nq_lat + (nq_ctx if with_ctx_queries else 0)),
        in_specs=[q_spec, kc_spec, kc_spec, kl_spec, kl_spec],
        out_specs=q_spec,
        out_shape=jax.ShapeDtypeStruct((t_rows if with_ctx_queries else batch * seq, width), BF16),
        compiler_params=_params("arbitrary", "arbitrary"),
        name="gqa_attention",
    )(aq, ak, av, ak, av)


def _merge_kernel(xl_ref, xc_ref, att_ref, of_ref, ob_ref, hg_ref, ga_ref, gh_ref, g1_ref, hgn_ref, post_ref,
                  wa_ref, wh_ref, wo_ref, o_ref, hg_scr, *, n_lat_tiles):
    ya = _dot(att_ref[...], wa_ref[...])
    hgn = hgn_ref[...]
    for i in range(HGRN_HEADS):
        o = _rms(of_ref[i] + ob_ref[i], hgn)
        hg_scr[:, i * HEAD_DIM:(i + 1) * HEAD_DIM] = (o * hg_ref[i].astype(F32)).astype(BF16)
    y = ga_ref[...].astype(F32) * ya + gh_ref[...].astype(F32) * _dot(hg_scr[...], wh_ref[...])
    z = _rms(_dot(y.astype(BF16), wo_ref[...]), post_ref[...])
    x = jnp.where(pl.program_id(0) < n_lat_tiles, xl_ref[...], xc_ref[...])
    o_ref[...] = x + g1_ref[...] * z


def _merge(x_lat, x_ctx, ctx_row0, att, o_f, o_b, hg, ga, gh, mod, hg_norm, post_norm, w_att, w_hg, w_out, *,
           rows, n_lat, seq):
    d = x_lat.shape[1]
    tm = TOKEN_TILE
    n_lat_tiles = n_lat // tm
    per_seq = seq // tm

    def mod_row(i):
        return jnp.where(i < n_lat_tiles, i // per_seq, mod.shape[0] - 1 - 3)

    tok = pl.BlockSpec((tm, d), lambda i: (i, 0))
    hm = pl.BlockSpec((HGRN_HEADS, tm, HEAD_DIM), lambda i: (0, i, 0))
    vec = lambda w: pl.BlockSpec((1, w), lambda i: (0, 0))
    wgt = _resident((d, d), lambda i: (0, 0))
    return pl.pallas_call(
        functools.partial(_merge_kernel, n_lat_tiles=n_lat_tiles),
        grid=(rows // tm,),
        in_specs=[pl.BlockSpec((tm, d), lambda i: (jnp.minimum(i, n_lat_tiles - 1), 0)),
                  pl.BlockSpec((tm, d), lambda i: (ctx_row0 // tm + jnp.maximum(i - n_lat_tiles, 0), 0)),
                  tok, hm, hm, hm, tok, tok,
                  pl.BlockSpec((None, 1, d), lambda i: (mod_row(i), 0, 2)),
                  vec(HEAD_DIM), vec(d), wgt, wgt, wgt],
        out_specs=tok,
        out_shape=jax.ShapeDtypeStruct((rows, d), F32),
        scratch_shapes=[pltpu.VMEM((tm, d), BF16)],
        compiler_params=_params("arbitrary"),
        name="branch_merge",
    )(x_lat, x_ctx, att, o_f, o_b, hg, ga, gh, mod, hg_norm, post_norm, w_att, w_hg, w_out)


def _ffn_kernel(x_ref, sh_ref, sc_ref, g2_ref, pre_ref, post_ref, wg_ref, wu_ref, wd_ref, o_ref, a_scr,
                *, ff_tile):
    x = x_ref[...]
    f = (_rms(x, pre_ref[...]) * (1.0 + sc_ref[...]) + sh_ref[...]).astype(BF16)
    for c in range(0, wg_ref.shape[1], ff_tile):
        g = _dot(f, wg_ref[:, c:c + ff_tile])
        u = _dot(f, wu_ref[:, c:c + ff_tile])
        a_scr[:, c:c + ff_tile] = (_silu(g) * u).astype(BF16)
    y = _rms(_dot(a_scr[...], wd_ref[...]), post_ref[...])
    o_ref[...] = x + g2_ref[...] * y


def _dense_ffn(x, mod, pre_norm, post_norm, w_gate, w_up, w_down, *, n_lat, seq):
    rows, d = x.shape
    d_ff = w_gate.shape[1]
    tm = TOKEN_TILE
    n_lat_tiles = n_lat // tm
    per_seq = seq // tm

    def mod_row(i):
        return jnp.where(i < n_lat_tiles, i // per_seq, mod.shape[0] - 1 - 3)

    tok = pl.BlockSpec((tm, d), lambda i: (i, 0))
    vec = pl.BlockSpec((1, d), lambda i: (0, 0))
    mod_spec = lambda col: pl.BlockSpec((None, 1, d), lambda i: (mod_row(i), 0, col))
    return pl.pallas_call(
        functools.partial(_ffn_kernel, ff_tile=256),
        grid=(rows // tm,),
        in_specs=[tok, mod_spec(3), mod_spec(4), mod_spec(5), vec, vec,
                  _resident((d, d_ff), lambda i: (0, 0)), _resident((d, d_ff), lambda i: (0, 0)),
                  _resident((d_ff, d), lambda i: (0, 0))],
        out_specs=tok,
        out_shape=jax.ShapeDtypeStruct((rows, d), F32),
        scratch_shapes=[pltpu.VMEM((tm, d_ff), BF16)],
        compiler_params=_params("arbitrary"),
        name="dense_swiglu",
    )(x, mod, mod, mod, pre_norm, post_norm, w_gate, w_up, w_down)


def _router_kernel(x_ref, sh_ref, sc_ref, pre_ref, wr_ref, f_ref, route_ref):
    f = _rms(x_ref[...], pre_ref[...]) * (1.0 + sc_ref[...]) + sh_ref[...]
    f_ref[...] = f
    lane = lax.broadcasted_iota(jnp.int32, (f.shape[0], LANES), 1)
    neg = jnp.float32(-jnp.inf)
    logits = jnp.full((f.shape[0], LANES), neg, F32)
    for e in range(wr_ref.shape[0]):
        logits = jnp.where(lane == e, jnp.sum(f * wr_ref[e:e + 1, :], axis=-1, keepdims=True), logits)
    m1 = jnp.max(logits, axis=-1, keepdims=True)
    i1 = jnp.min(jnp.where(logits == m1, lane, LANES), axis=-1, keepdims=True)
    rest = jnp.where(lane == i1, neg, logits)
    m2 = jnp.max(rest, axis=-1, keepdims=True)
    i2 = jnp.min(jnp.where(rest == m2, lane, LANES), axis=-1, keepdims=True)
    w1 = 1.0 / (1.0 + jnp.exp(m2 - m1))
    w2 = 1.0 - w1
    route = jnp.where(lane == 0, i1.astype(F32),
                      jnp.where(lane == 1, i2.astype(F32),
                                jnp.where(lane == 2, w1, jnp.where(lane == 3, w2, 0.0))))
    route_ref[...] = route


def _router(x, mod, pre_norm, w_router_t, *, seq):
    rows, d = x.shape
    tm = TOKEN_TILE
    per_seq = seq // tm
    tok = pl.BlockSpec((tm, d), lambda i: (i, 0))
    mod_spec = lambda col: pl.BlockSpec((None, 1, d), lambda i: (i // per_seq, 0, col))
    return pl.pallas_call(
        _router_kernel,
        grid=(rows // tm,),
        in_specs=[tok, mod_spec(3), mod_spec(4), pl.BlockSpec((1, d), lambda i: (0, 0)),
                  pl.BlockSpec(w_router_t.shape, lambda i: (0, 0))],
        out_specs=[tok, pl.BlockSpec((tm, LANES), lambda i: (i, 0))],
        out_shape=[jax.ShapeDtypeStruct((rows, d), F32), jax.ShapeDtypeStruct((rows, LANES), F32)],
        compiler_params=_params("arbitrary"),
        name="moe_router",
    )(x, mod, mod, pre_norm, w_router_t)


def _expert_kernel(be_ref, tok_ref, dst_ref, f_hbm, wg_ref, wu_ref, wd_ref, ysel_hbm,
                   xbuf, ybuf, a_scr, gsem, ssem, *, ff_tile, n_blocks, spare_row0):
    del be_ref
    s = pl.program_id(0)
    rows = xbuf.shape[1]

    def gather_row(r, slot):
        pltpu.make_async_copy(f_hbm.at[pl.ds(tok_ref[r], 1)], xbuf.at[slot, pl.ds(r, 1)],
                              gsem.at[slot]).start()

    def scatter_row(r, slot, dst_row):
        pltpu.make_async_copy(ybuf.at[slot, pl.ds(r, 1)], ysel_hbm.at[pl.ds(dst_row, 1)],
                              ssem.at[slot]).start(priority=1)

    def wait_gather(slot):
        pltpu.make_async_copy(f_hbm.at[pl.ds(0, rows)], xbuf.at[slot], gsem.at[slot]).wait()

    def wait_scatter(slot):
        pltpu.make_async_copy(ybuf.at[slot], ysel_hbm.at[pl.ds(0, rows)], ssem.at[slot]).wait()

    @pl.when(s == 0)
    def _():
        ybuf[...] = jnp.zeros(ybuf.shape, F32)

        def body(r, carry):
            gather_row(r, 0)
            scatter_row(r, 0, spare_row0 + r)
            return carry
        lax.fori_loop(0, rows, body, 0)

    @pl.when((s >= 1) & (s <= n_blocks))
    def _():
        slot = (s - 1) % 2
        wait_gather(slot)
        wait_scatter(slot)
        xb = xbuf[slot].astype(BF16)
        chunks = list(range(0, wg_ref.shape[1], ff_tile))
        per = -(-rows // (len(chunks) + 1))

        def move_rows(j):
            for r in range(j * per, min((j + 1) * per, rows)):
                gather_row(r, 1 - slot)
                scatter_row(r, 1 - slot, dst_ref[r])

        for j, c in enumerate(chunks):
            move_rows(j)
            g = _dot(xb, wg_ref[:, c:c + ff_tile])
            u = _dot(xb, wu_ref[:, c:c + ff_tile])
            a_scr[:, c:c + ff_tile] = (_silu(g) * u).astype(BF16)
        move_rows(len(chunks))
        ybuf[slot] = _dot(a_scr[...], wd_ref[...])

    @pl.when(s == n_blocks + 1)
    def _():
        slot = (n_blocks - 1) % 2

        def body(r, carry):
            scatter_row(r, slot, dst_ref[r])
            return carry
        lax.fori_loop(0, rows, body, 0)
        wait_scatter(1 - slot)
        wait_scatter(slot)
        wait_gather(1 - slot)


def _expert_ffn(f, row_tok, row_dst_ext, block_expert, w_gate, w_up, w_down, *, n_out_rows, spare_row0):
    n_rows = row_tok.shape[0]
    d = f.shape[1]
    d_ff = w_gate.shape[2]
    n_blocks = n_rows // MOE_BLOCK
    idx = lambda fn: pl.BlockSpec((MOE_BLOCK,), fn, memory_space=pltpu.SMEM)
    wgt = lambda shape: pl.BlockSpec((None,) + shape,
                                     lambda s, be: (be[jnp.clip(s - 1, 0, n_blocks - 1)], 0, 0),
                                     pipeline_mode=pl.Buffered(1))
    grid_spec = pltpu.PrefetchScalarGridSpec(
        num_scalar_prefetch=1,
        grid=(n_blocks + 2,),
        in_specs=[idx(lambda s, be: (jnp.minimum(s, n_blocks - 1),)),
                  idx(lambda s, be: (jnp.maximum(s - 1, 0),)),
                  pl.BlockSpec(memory_space=pl.ANY),
                  wgt((d, d_ff)), wgt((d, d_ff)), wgt((d_ff, d))],
        out_specs=pl.BlockSpec(memory_space=pl.ANY),
        scratch_shapes=[pltpu.VMEM((2, MOE_BLOCK, d), F32), pltpu.VMEM((2, MOE_BLOCK, d), F32),
                        pltpu.VMEM((MOE_BLOCK, d_ff), BF16),
                        pltpu.SemaphoreType.DMA((2,)), pltpu.SemaphoreType.DMA((2,))],
    )
    return pl.pallas_call(
        functools.partial(_expert_kernel, ff_tile=256, n_blocks=n_blocks, spare_row0=spare_row0),
        grid_spec=grid_spec,
        out_shape=jax.ShapeDtypeStruct((n_out_rows, d), F32),
        compiler_params=_params("arbitrary"),
        name="moe_expert_swiglu",
    )(block_expert, row_tok, row_dst_ext, f, w_gate, w_up, w_down)


def _combine_kernel(y0_ref, y1_ref, x_ref, route_ref, g2_ref, post_ref, o_ref):
    route = route_ref[...]
    y = route[:, 2:3] * y0_ref[...] + route[:, 3:4] * y1_ref[...]
    o_ref[...] = x_ref[...] + g2_ref[...] * _rms(y, post_ref[...])


def _combine(y_sel, x, route, mod, post_norm, *, seq, choice_stride):
    rows, d = x.shape
    tm = TOKEN_TILE
    per_seq = seq // tm
    tok = pl.BlockSpec((tm, d), lambda i: (i, 0))
    return pl.pallas_call(
        _combine_kernel,
        grid=(rows // tm,),
        in_specs=[tok, pl.BlockSpec((tm, d), lambda i: (choice_stride // tm + i, 0)), tok,
                  pl.BlockSpec((tm, LANES), lambda i: (i, 0)),
                  pl.BlockSpec((None, 1, d), lambda i: (i // per_seq, 0, 5)),
                  pl.BlockSpec((1, d), lambda i: (0, 0))],
        out_specs=tok,
        out_shape=jax.ShapeDtypeStruct((rows, d), F32),
        compiler_params=_params("arbitrary"),
        name="moe_combine",
    )(y_sel, y_sel, x, route, mod, post_norm)


def _moe_ffn(x, mod, pre_norm, post_norm, w_router, w_gate, w_up, w_down, *, seq):
    n_tok, d = x.shape
    n_assign = n_tok * TOP_K
    n_blocks = -(-n_assign // MOE_BLOCK) + N_EXPERTS
    n_rows = n_blocks * MOE_BLOCK
    n_pad = n_rows - n_assign
    assert TOP_K == 2 and n_pad % (TOP_K * TOKEN_TILE) == 0
    f, route = _router(x, mod, pre_norm, w_router.T, seq=seq)

    flat_e = route[:, :TOP_K].astype(jnp.int32).reshape(-1)
    onehot = (flat_e[:, None] == jnp.arange(N_EXPERTS, dtype=jnp.int32)[None, :]).astype(jnp.int32)
    csum = jnp.cumsum(onehot, axis=0)
    counts = csum[-1]
    rank = jnp.sum((csum - onehot) * onehot, axis=1)
    padded = (counts + MOE_BLOCK - 1) // MOE_BLOCK * MOE_BLOCK
    pend = jnp.cumsum(padded)
    dest = ((pend - padded)[flat_e] + rank).astype(jnp.int32)
    block_expert = jnp.minimum(
        jnp.searchsorted(pend, jnp.arange(n_blocks, dtype=jnp.int32) * MOE_BLOCK, side='right'),
        N_EXPERTS - 1).astype(jnp.int32)
    choice_stride = n_tok + n_pad // TOP_K
    row_assign = jnp.full((n_rows,), -1, jnp.int32).at[dest].set(jnp.arange(n_assign, dtype=jnp.int32),
                                                                 unique_indices=True)
    is_pad = row_assign < 0
    pad_id = jnp.cumsum(is_pad.astype(jnp.int32)) - 1
    row_tok = jnp.where(is_pad, 0, row_assign // TOP_K)
    row_dst = jnp.where(is_pad, (pad_id % TOP_K) * choice_stride + n_tok + pad_id // TOP_K,
                        (row_assign % TOP_K) * choice_stride + row_assign // TOP_K)
    row_dst_ext = jnp.concatenate([n_rows + jnp.arange(MOE_BLOCK, dtype=jnp.int32), row_dst])

    y_sel = _expert_ffn(f, row_tok, row_dst_ext, block_expert, w_gate, w_up, w_down,
                        n_out_rows=n_rows + 2 * MOE_BLOCK, spare_row0=n_rows + MOE_BLOCK)
    return _combine(y_sel, x, route, mod, post_norm, seq=seq, choice_stride=choice_stride)


def _rope_tables(seq, pad_rows):
    pos = jnp.arange(seq)
    nfreq = HEAD_DIM // 4
    inv_freq = ROPE_THETA ** (-jnp.arange(nfreq, dtype=F32) / nfreq)
    ang = jnp.concatenate([(pos // GRID_W).astype(F32)[:, None] * inv_freq,
                           (pos % GRID_W).astype(F32)[:, None] * inv_freq], axis=-1)
    ang = jnp.concatenate([ang, ang], axis=-1)
    sign = jnp.where(jnp.arange(HEAD_DIM) < HEAD_DIM // 2, -1.0, 1.0).astype(F32)
    cos = jnp.concatenate([jnp.cos(ang), jnp.ones((pad_rows, HEAD_DIM), F32)], axis=0)
    sin = jnp.concatenate([jnp.sin(ang) * sign, jnp.zeros((pad_rows, HEAD_DIM), F32)], axis=0)
    return cos, sin


def kernel(x, c, ctx, c_ctx, w_mod, b_mod, pre_mix_norm, post_mix_norm, pre_ffn_norm, post_ffn_norm, w_in, q_norm, k_norm, hg_norm, hg_lb_logits, w_att_branch, w_hg_branch, w_out, ffn_w_gate, ffn_w_up, ffn_w_down, moe_router, moe_w_gate, moe_w_up, moe_w_down):
    batch, seq, d = x.shape
    ctx_len = ctx.shape[1]
    depth = w_mod.shape[0]
    n_lat = batch * seq
    assert d == D_MODEL and w_in.shape[2] == PROJ_WIDTH
    assert seq % HGRN_STEP == 0 and ctx_len % HGRN_STEP == 0 and n_lat % TOKEN_TILE == 0
    assert seq % ctx_len == 0 and ctx_len % TOKEN_TILE == 0 and seq % ATT_K_TILE == 0

    c_rows = jnp.concatenate([c, c_ctx[None, :], jnp.zeros((3, d), F32)], axis=0)
    mod_all = _mod_vectors(c_rows, w_mod, b_mod)
    cos_tab, sin_tab = _rope_tables(seq, INPROJ_TILE)
    row = lambda v: v.reshape(1, -1)

    t_rows = n_lat + batch * ctx_len
    x_lat, x_ctx, ctx_row0 = x.reshape(n_lat, d), ctx.reshape(batch * ctx_len, d), 0
    for layer in range(depth):
        last = layer == depth - 1
        mod = mod_all[layer].reshape(batch + 4, 1, N_MOD * d)
        (aq, ak, av, hq, kf, lf, kb, lb, hi, hg, ga, gh) = _in_projection(
            x_lat, x_ctx, ctx_row0, mod, row(pre_mix_norm[layer]), w_in[layer].astype(BF16), cos_tab, sin_tab,
            row(q_norm[layer]), row(k_norm[layer]), hg_lb_logits, layer=layer, n_lat=n_lat, t_rows=t_rows,
            seq=seq)
        o_f, o_b = _hgrn_scan(hq, hi, kf, lf, kb, lb, batch=batch, seq=seq, ctx_len=ctx_len)
        att = _attention(aq, ak, av, batch=batch, seq=seq, ctx_len=ctx_len, with_ctx_queries=not last)
        rows = n_lat if last else t_rows
        xt = _merge(x_lat, x_ctx, ctx_row0, att, o_f, o_b, hg, ga, gh, mod, row(hg_norm[layer]), row(post_mix_norm[layer]),
                    w_att_branch[layer].astype(BF16), w_hg_branch[layer].astype(BF16),
                    w_out[layer].astype(BF16), rows=rows, n_lat=n_lat, seq=seq)
        idx = layer // 2
        if layer % 2 == 0:
            xt = _dense_ffn(xt, mod, row(pre_ffn_norm[layer]), row(post_ffn_norm[layer]),
                            ffn_w_gate[idx].astype(BF16), ffn_w_up[idx].astype(BF16),
                            ffn_w_down[idx].astype(BF16), n_lat=n_lat, seq=seq)
            x_lat, x_ctx, ctx_row0 = xt, xt, n_lat
        else:
            assert last, "the expert mixer is implemented for latent tokens only"
            xt = _moe_ffn(xt, mod, row(pre_ffn_norm[layer]), row(post_ffn_norm[layer]), moe_router[idx],
                          moe_w_gate[idx].astype(BF16), moe_w_up[idx].astype(BF16),
                          moe_w_down[idx].astype(BF16), seq=seq)
    return xt[:n_lat].reshape(batch, seq, d)
```
